```python
import math
import jax, jax.numpy as jnp
from jax import lax
import numpy as np

D_MODEL = 1024
BATCH = 1
SEQ = 16384
DEPTH = 4

N_MIXERS = 2
GRID_W = 64
S5_WIDTH = D_MODEL // 2
S5_GROUP = 16
S5_STATE = 64
S5_GROUPS = S5_WIDTH // S5_GROUP
DT_MIN = 1e-3
DT_MAX = 1e-1
N_HEADS = 16
HEAD_DIM = D_MODEL // N_HEADS
WIN_ROWS_MAX = 8
WIN_COLS = 16
Q_COL_BLOCK = 16
COL_BAND = Q_COL_BLOCK + WIN_COLS
N_COL_BLOCKS = GRID_W // Q_COL_BLOCK
N_GROUPS = 4
EXPERTS_PER_GROUP = 8
TOP_K_IN_GROUP = 2
EXPERT_FF = D_MODEL // 4
PLE_DIM = 256

N_S5_LAYERS = (DEPTH + 1) // 2
N_NAT_LAYERS = DEPTH // 2
RMS_EPS = 1e-6
MASK_VALUE = -1e30

kernel_name = "hybrid_s5_natten_hmoe_encoder"


def _rmsnorm(x, g):
    xf = x.astype(jnp.float32)
    y = xf * lax.rsqrt(jnp.mean(xf * xf, axis=-1, keepdims=True) + RMS_EPS)
    return (y * g.astype(jnp.float32)).astype(x.dtype)


def _ssm_combine(left, right):
    a_l, b_l = left
    a_r, b_r = right
    return a_r * a_l, a_r * b_l + b_r


def _s5_mixer(h, w_in, lam_re, lam_im, log_dt, b_re, b_im, c_re, c_im, d_skip, w_out):
    bsz, seq, _ = h.shape
    f32 = jnp.float32
    u = (h @ w_in).astype(f32)
    ug = u.reshape(bsz, seq, S5_GROUPS, S5_GROUP)
    y = d_skip.astype(f32) * u
    for direction, reverse in ((0, False), (1, True)):
        lam = lax.complex(lam_re[direction].astype(f32), lam_im[direction].astype(f32))
        dt = jnp.exp(log_dt[direction].astype(f32))[:, None]
        lam_bar = jnp.exp(lam * dt)
        b = lax.complex(b_re[direction].astype(f32), b_im[direction].astype(f32))
        c = lax.complex(c_re[direction].astype(f32), c_im[direction].astype(f32))
        b_bar = ((lam_bar - 1.0) / lam)[..., None] * b
        bu = jnp.einsum('gph,blgh->blgp', b_bar, ug)
        a = jnp.broadcast_to(lam_bar, bu.shape)
        _, states = lax.associative_scan(_ssm_combine, (a, bu), axis=1, reverse=reverse)
        y = y + jnp.einsum('ghp,blgp->blgh', c, states).real.reshape(bsz, seq, S5_WIDTH)
    act = jax.nn.gelu(y).astype(h.dtype)
    val, gate = jnp.split(act @ w_out, 2, axis=-1)
    return val * jax.nn.sigmoid(gate)


def _nat_mixer(h, w_qkv, q_norm, k_norm, rpb, w_o):
    bsz, seq, _ = h.shape
    rows = seq // GRID_W
    kh = min(WIN_ROWS_MAX, rows)
    qkv = (h @ w_qkv).reshape(bsz, seq, 3, N_HEADS, HEAD_DIM)
    q = _rmsnorm(qkv[:, :, 0], q_norm).astype(jnp.float32)
    k = _rmsnorm(qkv[:, :, 1], k_norm).astype(jnp.float32)
    v = qkv[:, :, 2]
    qg = q.reshape(bsz, rows, GRID_W, N_HEADS, HEAD_DIM)
    kg = k.reshape(bsz, rows, GRID_W, N_HEADS, HEAD_DIM)
    vg = v.reshape(bsz, rows, GRID_W, N_HEADS, HEAD_DIM)

    q_cols = np.arange(N_COL_BLOCKS)[:, None] * Q_COL_BLOCK + np.arange(Q_COL_BLOCK)[None, :]
    band_start = np.clip(np.arange(N_COL_BLOCKS) * Q_COL_BLOCK - WIN_COLS // 2, 0, GRID_W - COL_BAND)
    band_cols = band_start[:, None] + np.arange(COL_BAND)[None, :]
    win_start = np.clip(q_cols - WIN_COLS // 2, 0, GRID_W - WIN_COLS)
    rel_c = band_cols[:, None, :] - q_cols[:, :, None]
    col_mask = jnp.asarray((band_cols[:, None, :] >= win_start[:, :, None])
                           & (band_cols[:, None, :] < win_start[:, :, None] + WIN_COLS))
    col_idx = jnp.asarray(np.clip(rel_c + WIN_COLS - 1, 0, 2 * WIN_COLS - 2))
    band_cols_j = jnp.asarray(band_cols)
    rpb_f = rpb.astype(jnp.float32)
    scale = 1.0 / math.sqrt(HEAD_DIM)

    def row_fn(r):
        rs = jnp.clip(r - kh // 2, 0, rows - kh)
        q_row = lax.dynamic_index_in_dim(qg, r, axis=1, keepdims=False)
        q_blk = q_row.reshape(bsz, N_COL_BLOCKS, Q_COL_BLOCK, N_HEADS, HEAD_DIM)
        k_rows = lax.dynamic_slice_in_dim(kg, rs, kh, axis=1)
        v_rows = lax.dynamic_slice_in_dim(vg, rs, kh, axis=1)
        k_band = k_rows[:, :, band_cols_j]
        v_band = v_rows[:, :, band_cols_j]
        s = jnp.einsum('bjqhd,brjchd->bhjqrc', q_blk, k_band) * scale
        row_idx = rs + jnp.arange(kh) - r + (WIN_ROWS_MAX - 1)
        bias = rpb_f[:, row_idx][:, :, col_idx]
        s = s + jnp.transpose(bias, (0, 2, 3, 1, 4))[None]
        s = jnp.where(col_mask[None, None, :, :, None, :], s, MASK_VALUE)
        prob = jax.nn.softmax(s.reshape(s.shape[:4] + (kh * COL_BAND,)), axis=-1)
        prob = prob.reshape(s.shape).astype(v.dtype)
        o = jnp.einsum('bhjqrc,brjchd->bjqhd', prob, v_band)
        return o.reshape(bsz, GRID_W, N_HEADS, HEAD_DIM)

    out = lax.map(row_fn, jnp.arange(rows))
    out = jnp.moveaxis(out, 0, 1).reshape(bsz, seq, D_MODEL)
    return out @ w_o


def _hier_moe(h, w_group, b_group, w_expert, b_expert, w_gate, w_up, w_down):
    bsz, seq, d = h.shape
    t = h.reshape(bsz * seq, d)
    g_logits = (t @ w_group + b_group).astype(jnp.float32)
    g_prob = jax.nn.softmax(g_logits, axis=-1)
    g_val, g_idx = lax.top_k(g_prob, 1)
    e_logits = (t @ w_expert + b_expert).astype(jnp.float32).reshape(-1, N_GROUPS, EXPERTS_PER_GROUP)
    chosen = jnp.take_along_axis(e_logits, g_idx[:, :, None], axis=1)[:, 0]
    e_prob = jax.nn.softmax(chosen, axis=-1)
    e_val, e_idx = lax.top_k(e_prob, TOP_K_IN_GROUP)
    e_val = e_val / jnp.sum(e_val, axis=-1, keepdims=True)
    within = jnp.sum(jax.nn.one_hot(e_idx, EXPERTS_PER_GROUP, dtype=jnp.float32) * e_val[..., None], axis=1)
    combine = (jax.nn.one_hot(g_idx[:, 0], N_GROUPS, dtype=jnp.float32)[:, :, None]
               * g_val[:, :, None] * within[:, None, :]).astype(h.dtype)
    y = jnp.zeros_like(t)
    for g in range(N_GROUPS):
        a = jnp.einsum('td,edf->tef', t, w_gate[g])
        u = jnp.einsum('td,edf->tef', t, w_up[g])
        act = jax.nn.silu(a) * u * combine[:, g, :, None]
        y = y + jnp.einsum('tef,efd->td', act, w_down[g])
    return y.reshape(bsz, seq, d)


def setup_inputs(seed: int = 0) -> dict:
    key = jax.random.key(seed)
    ks = jax.random.split(key, 32)
    f32 = jnp.float32

    def nrm(k, shape, scale):
        return jax.random.normal(k, shape, f32) * scale

    n_idx = jnp.arange(S5_STATE, dtype=f32)
    s5_shape = (N_S5_LAYERS, 2, S5_GROUPS, S5_STATE)
    inp = {}
    inp["x"] = nrm(ks[0], (BATCH, SEQ, D_MODEL), 1.0)
    inp["p"] = nrm(ks[1], (DEPTH, BATCH, SEQ, PLE_DIM), 1.0)
    inp["norm_mix"] = 1.0 + nrm(ks[2], (DEPTH, D_MODEL), 0.01)
    inp["norm_ffn"] = 1.0 + nrm(ks[3], (DEPTH, D_MODEL), 0.01)
    inp["norm_ple"] = 1.0 + nrm(ks[4], (DEPTH, D_MODEL), 0.01)
    inp["s5_w_in"] = nrm(ks[5], (N_S5_LAYERS, D_MODEL, S5_WIDTH), D_MODEL ** -0.5)
    inp["s5_lam_re"] = -0.5 + nrm(ks[6], s5_shape, 0.01)
    inp["s5_lam_im"] = math.pi * n_idx + nrm(ks[7], s5_shape, 0.01)
    inp["s5_log_dt"] = jax.random.uniform(ks[8], (N_S5_LAYERS, 2, S5_GROUPS), f32,
                                          math.log(DT_MIN), math.log(DT_MAX))
    b_scale = (2.0 * S5_GROUP) ** -0.5
    inp["s5_b_re"] = nrm(ks[9], s5_shape + (S5_GROUP,), b_scale)
    inp["s5_b_im"] = nrm(ks[10], s5_shape + (S5_GROUP,), b_scale)
    c_shape = (N_S5_LAYERS, 2, S5_GROUPS, S5_GROUP, S5_STATE)
    inp["s5_c_re"] = nrm(ks[11], c_shape, 0.5 ** 0.5)
    inp["s5_c_im"] = nrm(ks[12], c_shape, 0.5 ** 0.5)
    inp["s5_d"] = nrm(ks[13], (N_S5_LAYERS, S5_WIDTH), 1.0)
    inp["s5_w_out"] = nrm(ks[14], (N_S5_LAYERS, S5_WIDTH, 2 * D_MODEL), S5_WIDTH ** -0.5)
    inp["nat_w_qkv"] = nrm(ks[15], (N_NAT_LAYERS, D_MODEL, 3 * D_MODEL), D_MODEL ** -0.5)
    inp["nat_q_norm"] = 1.0 + nrm(ks[16], (N_NAT_LAYERS, HEAD_DIM), 0.01)
    inp["nat_k_norm"] = 1.0 + nrm(ks[17], (N_NAT_LAYERS, HEAD_DIM), 0.01)
    inp["nat_rpb"] = nrm(ks[18], (N_NAT_LAYERS, N_HEADS, 2 * WIN_ROWS_MAX - 1, 2 * WIN_COLS - 1), 0.02)
    inp["nat_w_o"] = nrm(ks[19], (N_NAT_LAYERS, D_MODEL, D_MODEL), D_MODEL ** -0.5)
    inp["moe_w_group"] = nrm(ks[20], (DEPTH, D_MODEL, N_GROUPS), D_MODEL ** -0.5)
    inp["moe_b_group"] = nrm(ks[21], (DEPTH, N_GROUPS), 0.01)
    inp["moe_w_expert"] = nrm(ks[22], (DEPTH, D_MODEL, N_GROUPS * EXPERTS_PER_GROUP), D_MODEL ** -0.5)
    inp["moe_b_expert"] = nrm(ks[23], (DEPTH, N_GROUPS * EXPERTS_PER_GROUP), 0.01)
    e_shape = (DEPTH, N_GROUPS, EXPERTS_PER_GROUP)
    inp["moe_w_gate"] = nrm(ks[24], e_shape + (D_MODEL, EXPERT_FF), D_MODEL ** -0.5)
    inp["moe_w_up"] = nrm(ks[25], e_shape + (D_MODEL, EXPERT_FF), D_MODEL ** -0.5)
    inp["moe_w_down"] = nrm(ks[26], e_shape + (EXPERT_FF, D_MODEL), EXPERT_FF ** -0.5)
    inp["ple_w_proj"] = nrm(ks[27], (DEPTH, PLE_DIM, D_MODEL), PLE_DIM ** -0.5)
    inp["ple_w_gate"] = nrm(ks[28], (DEPTH, D_MODEL, D_MODEL), D_MODEL ** -0.5)
    return inp


def reference(x, p, norm_mix, norm_ffn, norm_ple,
              s5_w_in, s5_lam_re, s5_lam_im, s5_log_dt, s5_b_re, s5_b_im,
              s5_c_re, s5_c_im, s5_d, s5_w_out,
              nat_w_qkv, nat_q_norm, nat_k_norm, nat_rpb, nat_w_o,
              moe_w_group, moe_b_group, moe_w_expert, moe_b_expert,
              moe_w_gate, moe_w_up, moe_w_down,
              ple_w_proj, ple_w_gate):
    for i in range(DEPTH):
        j = i // N_MIXERS
        h = _rmsnorm(x, norm_mix[i])
        if i % N_MIXERS == 0:
            x = x + _s5_mixer(h, s5_w_in[j], s5_lam_re[j], s5_lam_im[j], s5_log_dt[j],
                              s5_b_re[j], s5_b_im[j], s5_c_re[j], s5_c_im[j],
                              s5_d[j], s5_w_out[j])
        else:
            x = x + _nat_mixer(h, nat_w_qkv[j], nat_q_norm[j], nat_k_norm[j],
                               nat_rpb[j], nat_w_o[j])
        h = _rmsnorm(x, norm_ffn[i])
        x = x + _hier_moe(h, moe_w_group[i], moe_b_group[i], moe_w_expert[i], moe_b_expert[i],
                          moe_w_gate[i], moe_w_up[i], moe_w_down[i])
        gate = jax.nn.sigmoid(_rmsnorm(x, norm_ple[i]) @ ple_w_gate[i])
        x = x + gate * (p[i] @ ple_w_proj[i])
    return x
```

```python
import functools
import math

import jax
import jax.numpy as jnp
from jax import lax
from jax.experimental import pallas as pl
from jax.experimental.pallas import tpu as pltpu

F32 = jnp.float32
BF16 = jnp.bfloat16

D_MODEL = 1024
GRID_W = 64
S5_GROUP = 16
S5_STATE = 64
S5_GROUPS = 32
N_HEADS = 16
HEAD_DIM = 64
WIN_ROWS = 8
WIN_COLS = 16
N_GROUPS = 4
EXPERTS_PER_GROUP = 8
N_EXPERTS = N_GROUPS * EXPERTS_PER_GROUP
EXPERT_FF = 256
RMS_EPS = 1e-6
MASK_VALUE = -1e30

V7X_LANES = 128
V7X_VMEM_BYTES = 64 * 1024 * 1024

S5_CHUNK = 64
S5_CHUNK_WIDTH = S5_CHUNK * S5_GROUP
S5_LAGS = 2 * S5_CHUNK
S5_LAG_LANES = S5_LAGS * S5_GROUP


def _cparams(n_axes, vmem_mib):
    return pltpu.CompilerParams(
        dimension_semantics=("arbitrary",) * n_axes,
        vmem_limit_bytes=min(vmem_mib * 1024 * 1024, V7X_VMEM_BYTES - 4 * 1024 * 1024),
    )


def _dot(a, b):
    return jnp.dot(a, b, preferred_element_type=F32)


def _rms(x, g):
    ms = jnp.mean(x * x, axis=-1, keepdims=True)
    return x * lax.rsqrt(ms + RMS_EPS) * g


def _norm_matmul_kernel(x_ref, g_ref, w_ref, o_ref):
    h = _rms(x_ref[...], g_ref[...]).astype(BF16)
    o_ref[...] = _dot(h, w_ref[...]).astype(o_ref.dtype)


def _norm_matmul(x, g, w, out_dtype, tm=512):
    t, d = x.shape
    n = w.shape[1]
    return pl.pallas_call(
        _norm_matmul_kernel,
        grid=(t // tm,),
        in_specs=[
            pl.BlockSpec((tm, d), lambda i: (i, 0)),
            pl.BlockSpec((1, d), lambda i: (0, 0)),
            pl.BlockSpec((d, n), lambda i: (0, 0)),
        ],
        out_specs=pl.BlockSpec((tm, n), lambda i: (i, 0)),
        out_shape=jax.ShapeDtypeStruct((t, n), out_dtype),
        compiler_params=_cparams(1, 32),
        name="norm_matmul",
    )(x, g.reshape(1, d), w)


def _s5_tables(lam_re, lam_im, log_dt, b_re, b_im, c_re, c_im, n_steps):
    q = S5_CHUNK
    g, p, h = S5_GROUPS, S5_STATE, S5_GROUP
    dt = jnp.exp(log_dt.astype(F32))[..., None]
    lam_re = lam_re.astype(F32)
    lam_im = lam_im.astype(F32)
    zr = lam_re * dt
    zi = lam_im * dt
    k = jnp.arange(q + 1, dtype=F32)[:, None]
    mag = jnp.exp(zr[:, :, None, :] * k)
    ang = zi[:, :, None, :] * k
    tr = mag * jnp.cos(ang)
    ti = mag * jnp.sin(ang)
    nr = tr[:, :, 1] - 1.0
    ni = ti[:, :, 1]
    den = lam_re * lam_re + lam_im * lam_im
    cr = (nr * lam_re + ni * lam_im) / den
    ci = (ni * lam_re - nr * lam_im) / den
    b_re = b_re.astype(F32)
    b_im = b_im.astype(F32)
    bbr = cr[..., None] * b_re - ci[..., None] * b_im
    bbi = cr[..., None] * b_im + ci[..., None] * b_re
    c_re = c_re.astype(F32)
    c_im = c_im.astype(F32)

    def lagtab(x, reverse):
        xt = jnp.swapaxes(x[:, :q, :], 1, 2)
        if reverse:
            return jnp.pad(xt[:, :, ::-1], ((0, 0), (0, 0), (0, q)))
        return jnp.pad(xt, ((0, 0), (0, 0), (q - 1, 1)))
    tt = jnp.concatenate([lagtab(tr[0], False), lagtab(ti[0], False),
                          lagtab(tr[1], True), lagtab(ti[1], True)], axis=1)
    cc = jnp.concatenate([jnp.swapaxes(c_re[0], 1, 2), jnp.swapaxes(c_im[0], 1, 2),
                          jnp.swapaxes(c_re[1], 1, 2), jnp.swapaxes(c_im[1], 1, 2)], axis=1)
    cc = jnp.pad(cc, ((0, 0), (0, 0), (0, V7X_LANES - h)))
    bm = jnp.concatenate([jnp.swapaxes(bbr[0], 1, 2), jnp.swapaxes(bbi[0], 1, 2),
                          jnp.swapaxes(bbr[1], 1, 2), jnp.swapaxes(bbi[1], 1, 2)], axis=2)

    def fmat(pr, pi, d):
        fr = pr[:, :, None, :] * jnp.swapaxes(bbr[d], 1, 2)[:, None] - pi[:, :, None, :] * jnp.swapaxes(bbi[d], 1, 2)[:, None]
        fi = pr[:, :, None, :] * jnp.swapaxes(bbi[d], 1, 2)[:, None] + pi[:, :, None, :] * jnp.swapaxes(bbr[d], 1, 2)[:, None]
        return fr.reshape(g, q * h, p), fi.reshape(g, q * h, p)
    ffr, ffi = fmat(tr[0][:, :q][:, ::-1], ti[0][:, :q][:, ::-1], 0)
    frr, fri = fmat(tr[1][:, :q], ti[1][:, :q], 1)
    slot = jax.nn.one_hot(jnp.arange(g) % 2, 2, dtype=F32)
    fcomp = jnp.stack([ffr, ffi, frr, fri], axis=2)
    f_pair = (fcomp[:, :, :, None, :] * slot[:, None, None, :, None]).reshape(g, q * h, 8 * p)

    def emat(pr, pi, d):
        cre =jnp.swapaxes(c_re[d], 1, 2)[:, :, None, :]
        cim = jnp.swapaxes(c_im[d], 1, 2)[:, :, None, :]
        prt = jnp.swapaxes(pr, 1, 2)[:, :, :, None]
        pit = jnp.swapaxes(pi, 1, 2)[:, :, :, None]
        er = cre * prt - cim * pit
        ei = cre * pit + cim * prt
        return er.reshape(g, p, q * h), (-ei).reshape(g, p, q * h)
    efr, efi = emat(tr[0][:, 1:q + 1], ti[0][:, 1:q + 1], 0)
    err, eri = emat(tr[1][:, 1:q + 1][:, ::-1], ti[1][:, 1:q + 1][:, ::-1], 1)
    ecomp = jnp.stack([efr, efi, err, eri], axis=1)
    e_pair = (ecomp[:, :, None, :, :] * slot[:, None, :, None, None]).reshape(g, 8 * p, q * h)

    steps = (q * 2.0 ** jnp.arange(n_steps, dtype=F32))[:, None, None, None]
    smag = jnp.exp(zr[None] * steps)
    sang = zi[None] * steps
    aq = jnp.stack([smag[:, 0] * jnp.cos(sang[:, 0]), smag[:, 0] * jnp.sin(sang[:, 0]),
                    smag[:, 1] * jnp.cos(sang[:, 1]), smag[:, 1] * jnp.sin(sang[:, 1])], axis=1)
    a_pair = aq.reshape(n_steps, 4, g // 2, 2 * p).transpose(2, 0, 1, 3)
    return tt, cc, bm, f_pair.astype(BF16), e_pair.astype(BF16), a_pair


def _s5_lag_kernel(tt_ref, cc_ref, bm_ref, r_ref, r2_ref, m_ref):
    hp = lax.Precision.HIGHEST
    p = S5_STATE
    tx = jnp.dot(tt_ref[0], r_ref[...], precision=hp, preferred_element_type=F32)
    cx = jnp.dot(cc_ref[0], r2_ref[...], precision=hp, preferred_element_type=F32)
    tfr, tfi, trr, tri = tx[0:p], tx[p:2 * p], tx[2 * p:3 * p], tx[3 * p:4 * p]
    cfr, cfi, crr, cri = cx[0:p], cx[p:2 * p], cx[2 * p:3 * p], cx[3 * p:4 * p]
    clt = jnp.concatenate([cfr * tfr - cfi * tfi, -(cfr * tfi + cfi * tfr),
                           crr * trr - cri * tri, -(crr * tri + cri * trr)], axis=0)
    bk = jnp.dot(bm_ref[0], clt, precision=hp, preferred_element_type=F32)
    for m in range(S5_CHUNK):
        off = (S5_CHUNK - 1 - m) * S5_GROUP
        blk = bk if off == 0 else pltpu.roll(bk, S5_LAG_LANES - off, axis=1)
        m_ref[0, m * S5_GROUP:(m + 1) * S5_GROUP, :] = blk[:, :S5_CHUNK_WIDTH].astype(BF16)


def _s5_lag_matrices(tt, cc, bm):
    g = tt.shape[0]
    p4 = 4 * S5_STATE
    rep = jnp.repeat(jnp.eye(S5_LAGS, dtype=F32), S5_GROUP, axis=1)
    til = jnp.pad(jnp.tile(jnp.eye(S5_GROUP, dtype=F32), (1, S5_LAGS)),
                  ((0, V7X_LANES - S5_GROUP), (0, 0)))
    return pl.pallas_call(
        _s5_lag_kernel,
        grid=(g,),
        in_specs=[
            pl.BlockSpec((1, p4, S5_LAGS), lambda i: (i, 0, 0)),
            pl.BlockSpec((1, p4, V7X_LANES), lambda i: (i, 0, 0)),
            pl.BlockSpec((1, S5_GROUP, p4), lambda i: (i, 0, 0)),
            pl.BlockSpec((S5_LAGS, S5_LAG_LANES), lambda i: (0, 0)),
            pl.BlockSpec((V7X_LANES, S5_LAG_LANES), lambda i: (0, 0)),
        ],
        out_specs=pl.BlockSpec((1, S5_CHUNK_WIDTH, S5_CHUNK_WIDTH), lambda i: (i, 0, 0)),
        out_shape=jax.ShapeDtypeStruct((g, S5_CHUNK_WIDTH, S5_CHUNK_WIDTH), BF16),
        compiler_params=_cparams(1, 32),
        name="s5_lag_matrices",
    )(tt, cc, bm, rep, til)


def _s5_core_kernel(u_ref, m_ref, f_ref, e_ref, a_ref, y_ref, sa_ref, sb_ref, *, n_chunks, n_steps):
    w = V7X_LANES
    n = n_chunks
    u0 = u_ref[0]
    u1 = u_ref[1]
    sa_ref[...] = _dot(u0, f_ref[0]) + _dot(u1, f_ref[1])
    src, dst = sa_ref, sb_ref
    for k in range(n_steps):
        s = 1 << k
        a = a_ref[0, k]
        afr, afi, arr, ari = a[0:1], a[1:2], a[2:3], a[3:4]
        pr = src[0:n - s, 0:w]
        pi = src[0:n - s, w:2 * w]
        dst[s:n, 0:w] = src[s:n, 0:w] + afr * pr - afi * pi
        dst[s:n, w:2 * w] = src[s:n, w:2 * w] + afr * pi + afi * pr
        dst[0:s, 0:2 * w] = src[0:s, 0:2 * w]
        nr = src[s:n, 2 * w:3 * w]
        ni = src[s:n, 3 * w:4 * w]
        dst[0:n - s, 2 * w:3 * w] = src[0:n - s, 2 * w:3 * w] + arr * nr - ari * ni
        dst[0:n - s, 3 * w:4 * w] = src[0:n - s, 3 * w:4 * w] + arr * ni + ari * nr
        dst[n - s:n, 2 * w:4 * w] = src[n - s:n, 2 * w:4 * w]
        src, dst = dst, src
    dst[1:n, 0:2 * w] = src[0:n - 1, 0:2 * w]
    dst[0:1, 0:2 * w] = jnp.zeros((1, 2 * w), F32)
    dst[0:n - 1, 2 * w:4 * w] = src[1:n, 2 * w:4 * w]
    dst[n - 1:n, 2 * w:4 * w] = jnp.zeros((1, 2 * w), F32)
    s_in = dst[...].astype(BF16)
    y_ref[0] = _dot(u0, m_ref[0]) + _dot(s_in, e_ref[0])
    y_ref[1] = _dot(u1, m_ref[1]) + _dot(s_in, e_ref[1])


def _s5_core(ug, mg, f_pair, e_pair, a_pair):
    g, n_chunks, cw = ug.shape
    n_steps = a_pair.shape[1]
    assert (1 << n_steps) >= n_chunks
    sw = 4 * V7X_LANES
    return pl.pallas_call(
        functools.partial(_s5_core_kernel, n_chunks=n_chunks, n_steps=n_steps),
        grid=(g // 2,),
        in_specs=[
            pl.BlockSpec((2, n_chunks, cw), lambda i: (i, 0, 0)),
            pl.BlockSpec((2, cw, cw), lambda i: (i, 0, 0)),
            pl.BlockSpec((2, cw, sw), lambda i: (i, 0, 0)),
            pl.BlockSpec((2, sw, cw), lambda i: (i, 0, 0)),
            pl.BlockSpec((1, n_steps, 4, V7X_LANES), lambda i: (i, 0, 0, 0)),
        ],
        out_specs=pl.BlockSpec((2, n_chunks, cw), lambda i: (i, 0, 0)),
        out_shape=jax.ShapeDtypeStruct((g, n_chunks, cw), F32),
        scratch_shapes=[pltpu.VMEM((n_chunks, sw), F32), pltpu.VMEM((n_chunks, sw), F32)],
        compiler_params=_cparams(1, 40),
        name="s5_core",
    )(ug, mg, f_pair, e_pair, a_pair)


def _s5_out_kernel(x_ref, y_ref, u_ref, d_ref, w_ref, o_ref):
    y = y_ref[...] + d_ref[...] * u_ref[...]
    act = jax.nn.gelu(y).astype(BF16)
    vg = _dot(act, w_ref[...])
    o_ref[...] = x_ref[...] + vg[:, :D_MODEL] * jax.nn.sigmoid(vg[:, D_MODEL:])


def _s5_out(x, y, u, d_skip, w_out, tm=512):
    t, d = x.shape
    sw = y.shape[1]
    return pl.pallas_call(
        _s5_out_kernel,
        grid=(t // tm,),
        in_specs=[
            pl.BlockSpec((tm, d), lambda i: (i, 0)),
            pl.BlockSpec((tm, sw), lambda i: (i, 0)),
            pl.BlockSpec((tm, sw), lambda i: (i, 0)),
            pl.BlockSpec((1, sw), lambda i: (0, 0)),
            pl.BlockSpec((sw, 2 * d), lambda i: (0, 0)),
        ],
        out_specs=pl.BlockSpec((tm, d), lambda i: (i, 0)),
        out_shape=jax.ShapeDtypeStruct((t, d), F32),
        compiler_params=_cparams(1, 32),
        name="s5_out",
    )(x, y, u, d_skip.reshape(1, sw).astype(F32), w_out)


def _s5_layer(x, norm_g, w_in, lam_re, lam_im, log_dt, b_re, b_im, c_re, c_im, d_skip, w_out):
    t = x.shape[0]
    n_chunks = t // S5_CHUNK
    u = _norm_matmul(x, norm_g, w_in.astype(BF16), F32)
    n_steps = n_chunks.bit_length() - 1
    assert (1 << n_steps) == n_chunks
    tt, cc, bm, f_pair, e_pair, a_pair = _s5_tables(lam_re, lam_im, log_dt, b_re, b_im, c_re, c_im, n_steps)
    mg = _s5_lag_matrices(tt, cc, bm)
    ug = u.astype(BF16).reshape(n_chunks, S5_CHUNK, S5_GROUPS, S5_GROUP)
    ug = ug.transpose(2, 0, 1, 3).reshape(S5_GROUPS, n_chunks, S5_CHUNK_WIDTH)
    yg = _s5_core(ug, mg, f_pair, e_pair, a_pair)
    y = yg.reshape(S5_GROUPS, n_chunks, S5_CHUNK, S5_GROUP).transpose(1, 2, 0, 3).reshape(t, S5_GROUPS * S5_GROUP)
    return _s5_out(x, y, u, d_skip, w_out.astype(BF16))


def _qkv_kernel(x_ref, g_ref, w_ref, qg_ref, kg_ref, q_ref, k_ref, v_ref):
    h = _rms(x_ref[...], g_ref[...]).astype(BF16)
    qkv = _dot(h, w_ref[...])
    tm = qkv.shape[0]
    lo = lax.broadcasted_iota(jnp.int32, (tm, V7X_LANES), 1) < HEAD_DIM
    scale = 1.0 / math.sqrt(HEAD_DIM)

    def headnorm(xt, gt):
        sq = xt * xt
        s0 = jnp.sum(jnp.where(lo, sq, 0.0), axis=-1, keepdims=True)
        s1 = jnp.sum(jnp.where(lo, 0.0, sq), axis=-1, keepdims=True)
        rs = jnp.where(lo, lax.rsqrt(s0 / HEAD_DIM + RMS_EPS), lax.rsqrt(s1 / HEAD_DIM + RMS_EPS))
        return xt * rs * gt

    for t in range(D_MODEL // V7X_LANES):
        lanes = slice(t * V7X_LANES, (t + 1) * V7X_LANES)
        q_ref[:, lanes] = (headnorm(qkv[:, lanes], qg_ref[...]) * scale).astype(BF16)
        klanes = slice(D_MODEL + t * V7X_LANES, D_MODEL + (t + 1) * V7X_LANES)
        k_ref[:, lanes] = headnorm(qkv[:, klanes], kg_ref[...]).astype(BF16)
    v_ref[...] = qkv[:, 2 * D_MODEL:].astype(BF16)


def _qkv(x, g, w_qkv, q_norm, k_norm, tm=512):
    t, d = x.shape
    qg = jnp.tile(q_norm.astype(F32), 2).reshape(1, V7X_LANES)
    kg = jnp.tile(k_norm.astype(F32), 2).reshape(1, V7X_LANES)
    out = jax.ShapeDtypeStruct((t, d), BF16)
    ospec = pl.BlockSpec((tm, d), lambda i: (i, 0))
    return pl.pallas_call(
        _qkv_kernel,
        grid=(t // tm,),
        in_specs=[
            pl.BlockSpec((tm, d), lambda i: (i, 0)),
            pl.BlockSpec((1, d), lambda i: (0, 0)),
            pl.BlockSpec((d, 3 * d), lambda i: (0, 0)),
            pl.BlockSpec((1, V7X_LANES), lambda i: (0, 0)),
            pl.BlockSpec((1, V7X_LANES), lambda i: (0, 0)),
        ],
        out_specs=[ospec, ospec, ospec],
        out_shape=[out, out, out],
        compiler_params=_cparams(1, 48),
        name="nat_qkv",
    )(x, g.reshape(1, d), w_qkv, qg, kg)


def _nat_bias(rpb):
    c = jnp.arange(GRID_W)[:, None]
    kc = jnp.arange(GRID_W)[None, :]
    ws = jnp.clip(c - WIN_COLS // 2, 0, GRID_W - WIN_COLS)
    valid = (kc >= ws) & (kc < ws + WIN_COLS)
    cidx = jnp.clip(kc - c + WIN_COLS - 1, 0, 2 * WIN_COLS - 2)
    ridx = jnp.arange(WIN_ROWS)[:, None] + jnp.arange(WIN_ROWS)[None, :]
    tab = rpb.astype(F32)[:, ridx][:, :, :, cidx]
    tab = jnp.where(valid, tab, MASK_VALUE)
    return tab.transpose(0, 1, 3, 2, 4).reshape(rpb.shape[0], WIN_ROWS, GRID_W, WIN_ROWS * GRID_W)


def _nat_attn_kernel(q_ref, k_ref, v_ref, b_ref, o_ref, *, rows):
    lo = lax.broadcasted_iota(jnp.int32, (GRID_W, V7X_LANES), 1) < HEAD_DIM
    head_mask = (jnp.where(lo, 1.0, 0.0), jnp.where(lo, 0.0, 1.0))
    nkeys = WIN_ROWS * GRID_W

    def body(r, carry):
        rs = jnp.clip(r - WIN_ROWS // 2, 0, rows - WIN_ROWS)
        off = rs - r + (WIN_ROWS - 1)
        q = q_ref[pl.ds(pl.multiple_of(r * GRID_W, GRID_W), GRID_W), :].astype(F32)
        k = k_ref[pl.ds(pl.multiple_of(rs * GRID_W, GRID_W), nkeys), :]
        v = v_ref[pl.ds(pl.multiple_of(rs * GRID_W, GRID_W), nkeys), :]
        outs = []
        for h in range(2):
            qh = (q * head_mask[h]).astype(BF16)
            s = lax.dot_general(qh, k, (((1,), (1,)), ((), ())), preferred_element_type=F32)
            s = s + b_ref[h, off]
            m = jnp.max(s, axis=-1, keepdims=True)
            p = jnp.exp(s - m)
            l = jnp.sum(p, axis=-1, keepdims=True)
            outs.append(_dot(p.astype(BF16), v) / l)
        o = jnp.where(lo, outs[0], outs[1])
        o_ref[pl.ds(pl.multiple_of(r * GRID_W, GRID_W), GRID_W), :] = o.astype(BF16)
        return carry

    lax.fori_loop(0, rows, body, 0)


def _nat_attn(q, k, v, bias):
    t, d = q.shape
    rows = t // GRID_W
    assert rows >= WIN_ROWS
    spec = pl.BlockSpec((t, V7X_LANES), lambda i: (0, i))
    return pl.pallas_call(
        functools.partial(_nat_attn_kernel, rows=rows),
        grid=(d // V7X_LANES,),
        in_specs=[spec, spec, spec,
                  pl.BlockSpec((2, WIN_ROWS, GRID_W, WIN_ROWS * GRID_W), lambda i: (i, 0, 0, 0))],
        out_specs=spec,
        out_shape=jax.ShapeDtypeStruct((t, d), BF16),
        compiler_params=_cparams(1, 48),
        name="nat_attn",
    )(q, k, v, bias)


def _matmul_residual_kernel(x_ref, a_ref, w_ref, o_ref):
    o_ref[...] = x_ref[...] + _dot(a_ref[...], w_ref[...])


def _matmul_residual(x, a, w, tm=512):
    t, d = x.shape
    kdim = a.shape[1]
    return pl.pallas_call(
        _matmul_residual_kernel,
        grid=(t // tm,),
        in_specs=[
            pl.BlockSpec((tm, d), lambda i: (i, 0)),
            pl.BlockSpec((tm, kdim), lambda i: (i, 0)),
            pl.BlockSpec((kdim, d), lambda i: (0, 0)),
        ],
        out_specs=pl.BlockSpec((tm, d), lambda i: (i, 0)),
        out_shape=jax.ShapeDtypeStruct((t, d), F32),
        compiler_params=_cparams(1, 32),
        name="matmul_residual",
    )(x, a, w)


def _nat_layer(x, norm_g, w_qkv, q_norm, k_norm, rpb, w_o):
    q, k, v = _qkv(x, norm_g, w_qkv.astype(BF16), q_norm, k_norm)
    o = _nat_attn(q, k, v, _nat_bias(rpb))
    return _matmul_residual(x, o, w_o.astype(BF16))


def _router_kernel(x_ref, g_ref, w_ref, b_ref, c_ref):
    h = _rms(x_ref[...], g_ref[...])
    lg = jnp.dot(h, w_ref[...], precision=lax.Precision.HIGHEST, preferred_element_type=F32) + b_ref[...]
    lane = lax.broadcasted_iota(jnp.int32, lg.shape, 1)
    big = 4 * V7X_LANES
    is_g = (lane >= N_EXPERTS) & (lane < N_EXPERTS + N_GROUPS)
    gmax = jnp.max(jnp.where(is_g, lg, MASK_VALUE), axis=-1, keepdims=True)
    gsum = jnp.sum(jnp.where(is_g, jnp.exp(lg - gmax), 0.0), axis=-1, keepdims=True)
    g_val = 1.0 / gsum
    g_idx = jnp.min(jnp.where(is_g & (lg == gmax), lane, big), axis=-1, keepdims=True) - N_EXPERTS
    in_grp = (lane < N_EXPERTS) & ((lane // EXPERTS_PER_GROUP) == g_idx)
    el = jnp.where(in_grp, lg, MASK_VALUE)
    m1 = jnp.max(el, axis=-1, keepdims=True)
    i1 = jnp.min(jnp.where(in_grp & (lg == m1), lane, big), axis=-1, keepdims=True)
    rest = in_grp & (lane != i1)
    m2 = jnp.max(jnp.where(rest, lg, MASK_VALUE), axis=-1, keepdims=True)
    i2 = jnp.min(jnp.where(rest & (lg == m2), lane, big), axis=-1, keepdims=True)
    z = jnp.sum(jnp.where(in_grp, jnp.exp(lg - m1), 0.0), axis=-1, keepdims=True)
    p1 = 1.0 / z
    p2 = jnp.exp(m2 - m1) / z
    den = p1 + p2
    c_ref[...] = g_val * (jnp.where(lane == i1, p1 / den, 0.0) + jnp.where(lane == i2, p2 / den, 0.0))


def _router(x, g, w_group, b_group, w_expert, b_expert, tm=512):
    t, d = x.shape
    pad = V7X_LANES - N_EXPERTS - N_GROUPS
    w = jnp.pad(jnp.concatenate([w_expert, w_group], axis=1).astype(F32), ((0, 0), (0, pad)))
    b = jnp.pad(jnp.concatenate([b_expert, b_group]).astype(F32), (0, pad)).reshape(1, V7X_LANES)
    return pl.pallas_call(
        _router_kernel,
        grid=(t // tm,),
        in_specs=[
            pl.BlockSpec((tm, d), lambda i: (i, 0)),
            pl.BlockSpec((1, d), lambda i: (0, 0)),
            pl.BlockSpec((d, V7X_LANES), lambda i: (0, 0)),
            pl.BlockSpec((1, V7X_LANES), lambda i: (0, 0)),
        ],
        out_specs=pl.BlockSpec((tm, V7X_LANES), lambda i: (i, 0)),
        out_shape=jax.ShapeDtypeStruct((t, V7X_LANES), F32),
        compiler_params=_cparams(1, 32),
        name="moe_router",
    )(x, g.reshape(1, d), w, b)


def _moe_dense_kernel(x_ref, g_ref, c_ref, wg_ref, wu_ref, wd_ref, o_ref, h_ref):
    gi = pl.program_id(1)

    @pl.when(gi == 0)
    def _():
        x = x_ref[...]
        h_ref[...] = _rms(x, g_ref[...]).astype(BF16)
        o_ref[...] = x

    h = h_ref[...]
    comb = c_ref[...]
    lane = lax.broadcasted_iota(jnp.int32, comb.shape, 1)
    acc = jnp.zeros(o_ref.shape, F32)
    for e in range(EXPERTS_PER_GROUP):
        ce = jnp.sum(jnp.where(lane == gi * EXPERTS_PER_GROUP + e, comb, 0.0), axis=-1, keepdims=True)
        a = _dot(h, wg_ref[0, e])
        u = _dot(h, wu_ref[0, e])
        act = (jax.nn.silu(a) * u * ce).astype(BF16)
        acc = acc + _dot(act, wd_ref[0, e])
    o_ref[...] += acc


def _moe_dense(x, g, comb, w_gate, w_up, w_down, tm=1024):
    t, d = x.shape
    ff = w_gate.shape[-1]
    return pl.pallas_call(
        _moe_dense_kernel,
        grid=(t // tm, N_GROUPS),
        in_specs=[
            pl.BlockSpec((tm, d), lambda i, j: (i, 0)),
            pl.BlockSpec((1, d), lambda i, j: (0, 0)),
            pl.BlockSpec((tm, V7X_LANES), lambda i, j: (i, 0)),
            pl.BlockSpec((1, EXPERTS_PER_GROUP, d, ff), lambda i, j: (j, 0, 0, 0)),
            pl.BlockSpec((1, EXPERTS_PER_GROUP, d, ff), lambda i, j: (j, 0, 0, 0)),
            pl.BlockSpec((1, EXPERTS_PER_GROUP, ff, d), lambda i, j: (j, 0, 0, 0)),
        ],
        out_specs=pl.BlockSpec((tm, d), lambda i, j: (i, 0)),
        out_shape=jax.ShapeDtypeStruct((t, d), F32),
        scratch_shapes=[pltpu.VMEM((tm, d), BF16)],
        compiler_params=_cparams(2, 56),
        name="moe_dense",
    )(x, g.reshape(1, d), comb, w_gate, w_up, w_down)


def _moe_layer(x, norm_g, w_group, b_group, w_expert, b_expert, w_gate, w_up, w_down):
    comb = _router(x, norm_g, w_group, b_group, w_expert, b_expert)
    return _moe_dense(x, norm_g, comb, w_gate.astype(BF16), w_up.astype(BF16), w_down.astype(BF16))


def _ple_kernel(x_ref, g_ref, p_ref, wg_ref, wp_ref, o_ref):
    x = x_ref[...]
    h = _rms(x, g_ref[...]).astype(BF16)
    gate = jax.nn.sigmoid(_dot(h, wg_ref[...]))
    o_ref[...] = x + gate * _dot(p_ref[...].astype(BF16), wp_ref[...])


def _ple_layer(x, norm_g, p, w_proj, w_gate, tm=512):
    t, d = x.shape
    pd = p.shape[1]
    return pl.pallas_call(
        _ple_kernel,
        grid=(t // tm,),
        in_specs=[
            pl.BlockSpec((tm, d), lambda i: (i, 0)),
            pl.BlockSpec((1, d), lambda i: (0, 0)),
            pl.BlockSpec((tm, pd), lambda i: (i, 0)),
            pl.BlockSpec((d, d), lambda i: (0, 0)),
            pl.BlockSpec((pd, d), lambda i: (0, 0)),
        ],
        out_specs=pl.BlockSpec((tm, d), lambda i: (i, 0)),
        out_shape=jax.ShapeDtypeStruct((t, d), F32),
        compiler_params=_cparams(1, 32),
        name="ple",
    )(x, norm_g.reshape(1, d), p, w_gate.astype(BF16), w_proj.astype(BF16))


def kernel(x, p, norm_mix, norm_ffn, norm_ple, s5_w_in, s5_lam_re, s5_lam_im, s5_log_dt, s5_b_re, s5_b_im, s5_c_re, s5_c_im, s5_d, s5_w_out, nat_w_qkv, nat_q_norm, nat_k_norm, nat_rpb, nat_w_o, moe_w_group, moe_b_group, moe_w_expert, moe_b_expert, moe_w_gate, moe_w_up, moe_w_down, ple_w_proj, ple_w_gate):
    bsz, seq, d = x.shape
    depth = p.shape[0]
    assert bsz == 1 and d == D_MODEL
    xs = x.reshape(seq, d).astype(F32)
    for i in range(depth):
        j = i // 2
        if i % 2 == 0:
            xs = _s5_layer(xs, norm_mix[i], s5_w_in[j], s5_lam_re[j], s5_lam_im[j], s5_log_dt[j],
                           s5_b_re[j], s5_b_im[j], s5_c_re[j], s5_c_im[j], s5_d[j], s5_w_out[j])
        else:
            xs = _nat_layer(xs, norm_mix[i], nat_w_qkv[j], nat_q_norm[j], nat_k_norm[j], nat_rpb[j], nat_w_o[j])
        xs = _moe_layer(xs, norm_ffn[i], moe_w_group[i], moe_b_group[i], moe_w_expert[i], moe_b_expert[i],
                        moe_w_gate[i], moe_w_up[i], moe_w_down[i])
        xs = _ple_layer(xs, norm_ple[i], p[i].reshape(seq, -1), ple_w_proj[i], ple_w_gate[i])
    return xs.reshape(bsz, seq, d).astype(x.dtype)
```

```python
import functools
import math

import jax
import jax.numpy as jnp
from jax import lax
from jax.experimental import pallas as pl
from jax.experimental.pallas import tpu as pltpu

F32 = jnp.float32
BF16 = jnp.bfloat16

D_MODEL = 1024
GRID_W = 64
S5_GROUP = 16
S5_STATE = 64
S5_GROUPS = 32
N_HEADS = 16
HEAD_DIM = 64
WIN_ROWS = 8
WIN_COLS = 16
N_GROUPS = 4
EXPERTS_PER_GROUP = 8
N_EXPERTS = N_GROUPS * EXPERTS_PER_GROUP
EXPERT_FF = 256
RMS_EPS = 1e-6
MASK_VALUE = -1e30

V7X_LANES = 128
V7X_VMEM_BYTES = 64 * 1024 * 1024

S5_CHUNK = 64
S5_CHUNK_WIDTH = S5_CHUNK * S5_GROUP
S5_LAGS = 2 * S5_CHUNK
S5_LAG_LANES = S5_LAGS * S5_GROUP

NAT_ROW_UNROLL = 8

MOE_ROUTER_ROWS = 40
MOE_ROW_TILE = 256
MOE_TOKEN_TILE = 512


def _cparams(n_axes, vmem_mib):
    return pltpu.CompilerParams(
        dimension_semantics=("arbitrary",) * n_axes,
        vmem_limit_bytes=min(vmem_mib * 1024 * 1024, V7X_VMEM_BYTES - 4 * 1024 * 1024),
    )


def _dot(a, b):
    return jnp.dot(a, b, preferred_element_type=F32)


def _rms(x, g):
    ms = jnp.mean(x * x, axis=-1, keepdims=True)
    return x * lax.rsqrt(ms + RMS_EPS) * g


def _norm_matmul_kernel(x_ref, g_ref, w_ref, o_ref):
    h = _rms(x_ref[...], g_ref[...]).astype(BF16)
    o_ref[...] = _dot(h, w_ref[...]).astype(o_ref.dtype)


def _norm_matmul(x, g, w, out_dtype, tm=512):
    t, d = x.shape
    n = w.shape[1]
    return pl.pallas_call(
        _norm_matmul_kernel,
        grid=(t // tm,),
        in_specs=[
            pl.BlockSpec((tm, d), lambda i: (i, 0)),
            pl.BlockSpec((1, d), lambda i: (0, 0)),
            pl.BlockSpec((d, n), lambda i: (0, 0)),
        ],
        out_specs=pl.BlockSpec((tm, n), lambda i: (i, 0)),
        out_shape=jax.ShapeDtypeStruct((t, n), out_dtype),
        compiler_params=_cparams(1, 32),
        name="norm_matmul",
    )(x, g.reshape(1, d), w)


def _s5_tables(lam_re, lam_im, log_dt, b_re, b_im, c_re, c_im, n_steps):
    q = S5_CHUNK
    g, p, h = S5_GROUPS, S5_STATE, S5_GROUP
    dt = jnp.exp(log_dt.astype(F32))[..., None]
    lam_re = lam_re.astype(F32)
    lam_im = lam_im.astype(F32)
    zr = lam_re * dt
    zi = lam_im * dt
    k = jnp.arange(q + 1, dtype=F32)[:, None]
    mag = jnp.exp(zr[:, :, None, :] * k)
    ang = zi[:, :, None, :] * k
    tr = mag * jnp.cos(ang)
    ti = mag * jnp.sin(ang)
    nr = tr[:, :, 1] - 1.0
    ni = ti[:, :, 1]
    den = lam_re * lam_re + lam_im * lam_im
    cr = (nr * lam_re + ni * lam_im) / den
    ci = (ni * lam_re - nr * lam_im) / den
    b_re = b_re.astype(F32)
    b_im = b_im.astype(F32)
    bbr = cr[..., None] * b_re - ci[..., None] * b_im
    bbi = cr[..., None] * b_im + ci[..., None] * b_re
    c_re = c_re.astype(F32)
    c_im = c_im.astype(F32)

    def lagtab(x, reverse):
        xt = jnp.swapaxes(x[:, :q, :], 1, 2)
        if reverse:
            return jnp.pad(xt[:, :, ::-1], ((0, 0), (0, 0), (0, q)))
        return jnp.pad(xt, ((0, 0), (0, 0), (q - 1, 1)))
    tt = jnp.concatenate([lagtab(tr[0], False), lagtab(ti[0], False),
                          lagtab(tr[1], True), lagtab(ti[1], True)], axis=1)
    cc = jnp.concatenate([jnp.swapaxes(c_re[0], 1, 2), jnp.swapaxes(c_im[0], 1, 2),
                          jnp.swapaxes(c_re[1], 1, 2), jnp.swapaxes(c_im[1], 1, 2)], axis=1)
    cc = jnp.pad(cc, ((0, 0), (0, 0), (0, V7X_LANES - h)))
    bm = jnp.concatenate([jnp.swapaxes(bbr[0], 1, 2), jnp.swapaxes(bbi[0], 1, 2),
                          jnp.swapaxes(bbr[1], 1, 2), jnp.swapaxes(bbi[1], 1, 2)], axis=2)

    def fmat(pr, pi, d):
        fr = pr[:, :, None, :] * jnp.swapaxes(bbr[d], 1, 2)[:, None] - pi[:, :, None, :] * jnp.swapaxes(bbi[d], 1, 2)[:, None]
        fi = pr[:, :, None, :] * jnp.swapaxes(bbi[d], 1, 2)[:, None] + pi[:, :, None, :] * jnp.swapaxes(bbr[d], 1, 2)[:, None]
        return fr.reshape(g, q * h, p), fi.reshape(g, q * h, p)
    ffr, ffi = fmat(tr[0][:, :q][:, ::-1], ti[0][:, :q][:, ::-1], 0)
    frr, fri = fmat(tr[1][:, :q], ti[1][:, :q], 1)
    slot = jax.nn.one_hot(jnp.arange(g) % 2, 2, dtype=F32)
    fcomp = jnp.stack([ffr, ffi, frr, fri], axis=2)
    f_pair = (fcomp[:, :, :, None, :] * slot[:, None, None, :, None]).reshape(g, q * h, 8 * p)

    def emat(pr, pi, d):
        cre =jnp.swapaxes(c_re[d], 1, 2)[:, :, None, :]
        cim = jnp.swapaxes(c_im[d], 1, 2)[:, :, None, :]
        prt = jnp.swapaxes(pr, 1, 2)[:, :, :, None]
        pit = jnp.swapaxes(pi, 1, 2)[:, :, :, None]
        er = cre * prt - cim * pit
        ei = cre * pit + cim * prt
        return er.reshape(g, p, q * h), (-ei).reshape(g, p, q * h)
    efr, efi = emat(tr[0][:, 1:q + 1], ti[0][:, 1:q + 1], 0)
    err, eri = emat(tr[1][:, 1:q + 1][:, ::-1], ti[1][:, 1:q + 1][:, ::-1], 1)
    ecomp = jnp.stack([efr, efi, err, eri], axis=1)
    e_pair = (ecomp[:, :, None, :, :] * slot[:, None, :, None, None]).reshape(g, 8 * p, q * h)

    steps = (q * 2.0 ** jnp.arange(n_steps, dtype=F32))[:, None, None, None]
    smag = jnp.exp(zr[None] * steps)
    sang = zi[None] * steps
    aq = jnp.stack([smag[:, 0] * jnp.cos(sang[:, 0]), smag[:, 0] * jnp.sin(sang[:, 0]),
                    smag[:, 1] * jnp.cos(sang[:, 1]), smag[:, 1] * jnp.sin(sang[:, 1])], axis=1)
    a_pair = aq.reshape(n_steps, 4, g // 2, 2 * p).transpose(2, 0, 1, 3)
    return tt, cc, bm, f_pair.astype(BF16), e_pair.astype(BF16), a_pair


def _s5_lag_kernel(tt_ref, cc_ref, bm_ref, r_ref, r2_ref, m_ref):
    hp = lax.Precision.HIGHEST
    p = S5_STATE
    tx = jnp.dot(tt_ref[0], r_ref[...], precision=hp, preferred_element_type=F32)
    cx = jnp.dot(cc_ref[0], r2_ref[...], precision=hp, preferred_element_type=F32)
    tfr, tfi, trr, tri = tx[0:p], tx[p:2 * p], tx[2 * p:3 * p], tx[3 * p:4 * p]
    cfr, cfi, crr, cri = cx[0:p], cx[p:2 * p], cx[2 * p:3 * p], cx[3 * p:4 * p]
    clt = jnp.concatenate([cfr * tfr - cfi * tfi, -(cfr * tfi + cfi * tfr),
                           crr * trr - cri * tri, -(crr * tri + cri * trr)], axis=0)
    bk = jnp.dot(bm_ref[0], clt, precision=hp, preferred_element_type=F32)
    for m in range(S5_CHUNK):
        off = (S5_CHUNK - 1 - m) * S5_GROUP
        blk = bk if off == 0 else pltpu.roll(bk, S5_LAG_LANES - off, axis=1)
        m_ref[0, m * S5_GROUP:(m + 1) * S5_GROUP, :] = blk[:, :S5_CHUNK_WIDTH].astype(BF16)


def _s5_lag_matrices(tt, cc, bm):
    g = tt.shape[0]
    p4 = 4 * S5_STATE
    rep = jnp.repeat(jnp.eye(S5_LAGS, dtype=F32), S5_GROUP, axis=1)
    til = jnp.pad(jnp.tile(jnp.eye(S5_GROUP, dtype=F32), (1, S5_LAGS)),
                  ((0, V7X_LANES - S5_GROUP), (0, 0)))
    return pl.pallas_call(
        _s5_lag_kernel,
        grid=(g,),
        in_specs=[
            pl.BlockSpec((1, p4, S5_LAGS), lambda i: (i, 0, 0)),
            pl.BlockSpec((1, p4, V7X_LANES), lambda i: (i, 0, 0)),
            pl.BlockSpec((1, S5_GROUP, p4), lambda i: (i, 0, 0)),
            pl.BlockSpec((S5_LAGS, S5_LAG_LANES), lambda i: (0, 0)),
            pl.BlockSpec((V7X_LANES, S5_LAG_LANES), lambda i: (0, 0)),
        ],
        out_specs=pl.BlockSpec((1, S5_CHUNK_WIDTH, S5_CHUNK_WIDTH), lambda i: (i, 0, 0)),
        out_shape=jax.ShapeDtypeStruct((g, S5_CHUNK_WIDTH, S5_CHUNK_WIDTH), BF16),
        compiler_params=_cparams(1, 32),
        name="s5_lag_matrices",
    )(tt, cc, bm, rep, til)


def _s5_core_kernel(u_ref, m_ref, f_ref, e_ref, a_ref, y_ref, sa_ref, sb_ref, *, n_chunks, n_steps):
    w = V7X_LANES
    n = n_chunks
    u0 = u_ref[0]
    u1 = u_ref[1]
    sa_ref[...] = _dot(u0, f_ref[0]) + _dot(u1, f_ref[1])
    src, dst = sa_ref, sb_ref
    for k in range(n_steps):
        s = 1 << k
        a = a_ref[0, k]
        afr, afi, arr, ari = a[0:1], a[1:2], a[2:3], a[3:4]
        pr = src[0:n - s, 0:w]
        pi = src[0:n - s, w:2 * w]
        dst[s:n, 0:w] = src[s:n, 0:w] + afr * pr - afi * pi
        dst[s:n, w:2 * w] = src[s:n, w:2 * w] + afr * pi + afi * pr
        dst[0:s, 0:2 * w] = src[0:s, 0:2 * w]
        nr = src[s:n, 2 * w:3 * w]
        ni = src[s:n, 3 * w:4 * w]
        dst[0:n - s, 2 * w:3 * w] = src[0:n - s, 2 * w:3 * w] + arr * nr - ari * ni
        dst[0:n - s, 3 * w:4 * w] = src[0:n - s, 3 * w:4 * w] + arr * ni + ari * nr
        dst[n - s:n, 2 * w:4 * w] = src[n - s:n, 2 * w:4 * w]
        src, dst = dst, src
    dst[1:n, 0:2 * w] = src[0:n - 1, 0:2 * w]
    dst[0:1, 0:2 * w] = jnp.zeros((1, 2 * w), F32)
    dst[0:n - 1, 2 * w:4 * w] = src[1:n, 2 * w:4 * w]
    dst[n - 1:n, 2 * w:4 * w] = jnp.zeros((1, 2 * w), F32)
    s_in = dst[...].astype(BF16)
    y_ref[0] = _dot(u0, m_ref[0]) + _dot(s_in, e_ref[0])
    y_ref[1] = _dot(u1, m_ref[1]) + _dot(s_in, e_ref[1])


def _s5_core(ug, mg, f_pair, e_pair, a_pair):
    g, n_chunks, cw = ug.shape
    n_steps = a_pair.shape[1]
    assert (1 << n_steps) >= n_chunks
    sw = 4 * V7X_LANES
    return pl.pallas_call(
        functools.partial(_s5_core_kernel, n_chunks=n_chunks, n_steps=n_steps),
        grid=(g // 2,),
        in_specs=[
            pl.BlockSpec((2, n_chunks, cw), lambda i: (i, 0, 0)),
            pl.BlockSpec((2, cw, cw), lambda i: (i, 0, 0)),
            pl.BlockSpec((2, cw, sw), lambda i: (i, 0, 0)),
            pl.BlockSpec((2, sw, cw), lambda i: (i, 0, 0)),
            pl.BlockSpec((1, n_steps, 4, V7X_LANES), lambda i: (i, 0, 0, 0)),
        ],
        out_specs=pl.BlockSpec((2, n_chunks, cw), lambda i: (i, 0, 0)),
        out_shape=jax.ShapeDtypeStruct((g, n_chunks, cw), F32),
        scratch_shapes=[pltpu.VMEM((n_chunks, sw), F32), pltpu.VMEM((n_chunks, sw), F32)],
        compiler_params=_cparams(1, 40),
        name="s5_core",
    )(ug, mg, f_pair, e_pair, a_pair)


def _s5_out_kernel(x_ref, y_ref, u_ref, d_ref, w_ref, o_ref):
    y = y_ref[...] + d_ref[...] * u_ref[...]
    act = jax.nn.gelu(y).astype(BF16)
    vg = _dot(act, w_ref[...])
    o_ref[...] = x_ref[...] + vg[:, :D_MODEL] * jax.nn.sigmoid(vg[:, D_MODEL:])


def _s5_out(x, y, u, d_skip, w_out, tm=512):
    t, d = x.shape
    sw = y.shape[1]
    return pl.pallas_call(
        _s5_out_kernel,
        grid=(t // tm,),
        in_specs=[
            pl.BlockSpec((tm, d), lambda i: (i, 0)),
            pl.BlockSpec((tm, sw), lambda i: (i, 0)),
            pl.BlockSpec((tm, sw), lambda i: (i, 0)),
            pl.BlockSpec((1, sw), lambda i: (0, 0)),
            pl.BlockSpec((sw, 2 * d), lambda i: (0, 0)),
        ],
        out_specs=pl.BlockSpec((tm, d), lambda i: (i, 0)),
        out_shape=jax.ShapeDtypeStruct((t, d), F32),
        compiler_params=_cparams(1, 32),
        name="s5_out",
    )(x, y, u, d_skip.reshape(1, sw).astype(F32), w_out)


def _s5_layer(x, norm_g, w_in, lam_re, lam_im, log_dt, b_re, b_im, c_re, c_im, d_skip, w_out):
    t = x.shape[0]
    n_chunks = t // S5_CHUNK
    u = _norm_matmul(x, norm_g, w_in.astype(BF16), F32)
    n_steps = n_chunks.bit_length() - 1
    assert (1 << n_steps) == n_chunks
    tt, cc, bm, f_pair, e_pair, a_pair = _s5_tables(lam_re, lam_im, log_dt, b_re, b_im, c_re, c_im, n_steps)
    mg = _s5_lag_matrices(tt, cc, bm)
    ug = u.astype(BF16).reshape(n_chunks, S5_CHUNK, S5_GROUPS, S5_GROUP)
    ug = ug.transpose(2, 0, 1, 3).reshape(S5_GROUPS, n_chunks, S5_CHUNK_WIDTH)
    yg = _s5_core(ug, mg, f_pair, e_pair, a_pair)
    y = yg.reshape(S5_GROUPS, n_chunks, S5_CHUNK, S5_GROUP).transpose(1, 2, 0, 3).reshape(t, S5_GROUPS * S5_GROUP)
    return _s5_out(x, y, u, d_skip, w_out.astype(BF16))


def _qkv_kernel(x_ref, g_ref, w_ref, qg_ref, kg_ref, q_ref, k_ref, v_ref):
    h = _rms(x_ref[...], g_ref[...]).astype(BF16)
    qkv = _dot(h, w_ref[...])
    tm = qkv.shape[0]
    lo = lax.broadcasted_iota(jnp.int32, (tm, V7X_LANES), 1) < HEAD_DIM
    scale = 1.0 / math.sqrt(HEAD_DIM)

    def headnorm(xt, gt):
        sq = xt * xt
        s0 = jnp.sum(jnp.where(lo, sq, 0.0), axis=-1, keepdims=True)
        s1 = jnp.sum(jnp.where(lo, 0.0, sq), axis=-1, keepdims=True)
        rs = jnp.where(lo, lax.rsqrt(s0 / HEAD_DIM + RMS_EPS), lax.rsqrt(s1 / HEAD_DIM + RMS_EPS))
        return xt * rs * gt

    for t in range(D_MODEL // V7X_LANES):
        lanes = slice(t * V7X_LANES, (t + 1) * V7X_LANES)
        q_ref[:, lanes] = (headnorm(qkv[:, lanes], qg_ref[...]) * scale).astype(BF16)
        klanes = slice(D_MODEL + t * V7X_LANES, D_MODEL + (t + 1) * V7X_LANES)
        k_ref[:, lanes] = headnorm(qkv[:, klanes], kg_ref[...]).astype(BF16)
    v_ref[...] = qkv[:, 2 * D_MODEL:].astype(BF16)


def _qkv(x, g, w_qkv, q_norm, k_norm, tm=512):
    t, d = x.shape
    qg = jnp.tile(q_norm.astype(F32), 2).reshape(1, V7X_LANES)
    kg = jnp.tile(k_norm.astype(F32), 2).reshape(1, V7X_LANES)
    out = jax.ShapeDtypeStruct((t, d), BF16)
    ospec = pl.BlockSpec((tm, d), lambda i: (i, 0))
    return pl.pallas_call(
        _qkv_kernel,
        grid=(t // tm,),
        in_specs=[
            pl.BlockSpec((tm, d), lambda i: (i, 0)),
            pl.BlockSpec((1, d), lambda i: (0, 0)),
            pl.BlockSpec((d, 3 * d), lambda i: (0, 0)),
            pl.BlockSpec((1, V7X_LANES), lambda i: (0, 0)),
            pl.BlockSpec((1, V7X_LANES), lambda i: (0, 0)),
        ],
        out_specs=[ospec, ospec, ospec],
        out_shape=[out, out, out],
        compiler_params=_cparams(1, 48),
        name="nat_qkv",
    )(x, g.reshape(1, d), w_qkv, qg, kg)


def _nat_bias(rpb):
    c = jnp.arange(GRID_W)[:, None]
    kc = jnp.arange(GRID_W)[None, :]
    ws = jnp.clip(c - WIN_COLS // 2, 0, GRID_W - WIN_COLS)
    valid = (kc >= ws) & (kc < ws + WIN_COLS)
    cidx = jnp.clip(kc - c + WIN_COLS - 1, 0, 2 * WIN_COLS - 2)
    ridx = jnp.arange(WIN_ROWS)[:, None] + jnp.arange(WIN_ROWS)[None, :]
    tab = rpb.astype(F32)[:, ridx][:, :, :, cidx]
    tab = jnp.where(valid, tab, MASK_VALUE)
    return tab.transpose(0, 1, 3, 2, 4).reshape(rpb.shape[0], WIN_ROWS, GRID_W, WIN_ROWS * GRID_W)


def _nat_attn_kernel(q_ref, k_ref, v_ref, b_ref, o_ref, *, rows):
    lo = lax.broadcasted_iota(jnp.int32, (GRID_W, V7X_LANES), 1) < HEAD_DIM
    head_mask = (jnp.where(lo, 1.0, 0.0), jnp.where(lo, 0.0, 1.0))
    nkeys = WIN_ROWS * GRID_W

    def body(rb, carry):
        chains = []
        for u in range(NAT_ROW_UNROLL):
            r = rb * NAT_ROW_UNROLL + u
            rs = jnp.clip(r - WIN_ROWS // 2, 0, rows - WIN_ROWS)
            off = rs - r + (WIN_ROWS - 1)
            q = q_ref[pl.ds(pl.multiple_of(r * GRID_W, GRID_W), GRID_W), :].astype(F32)
            k = k_ref[pl.ds(pl.multiple_of(rs * GRID_W, GRID_W), nkeys), :]
            for h in range(2):
                chains.append(dict(r=r, rs=rs, off=off, h=h, q=(q * head_mask[h]).astype(BF16), k=k))
        for c in chains:
            c["s"] = lax.dot_general(c["q"], c["k"], (((1,), (1,)), ((), ())), preferred_element_type=F32)
        for c in chains:
            c["s"] = c["s"] + b_ref[c["h"], c["off"]]
        for c in chains:
            c["m"] = jnp.max(c["s"], axis=-1, keepdims=True)
        for c in chains:
            c["p"] = jnp.exp(c["s"] - c["m"])
        for c in chains:
            c["l"] = jnp.sum(c["p"], axis=-1, keepdims=True)
        for c in chains:
            v = v_ref[pl.ds(pl.multiple_of(c["rs"] * GRID_W, GRID_W), nkeys), :]
            c["o"] = _dot(c["p"].astype(BF16), v) / c["l"]
        for u in range(NAT_ROW_UNROLL):
            c0, c1 = chains[2 * u], chains[2 * u + 1]
            o = jnp.where(lo, c0["o"], c1["o"])
            o_ref[pl.ds(pl.multiple_of(c0["r"] * GRID_W, GRID_W), GRID_W), :] = o.astype(BF16)
        return carry

    lax.fori_loop(0, rows // NAT_ROW_UNROLL, body, 0)


def _nat_attn(q, k, v, bias):
    t, d = q.shape
    rows = t // GRID_W
    assert rows >= WIN_ROWS
    spec = pl.BlockSpec((t, V7X_LANES), lambda i: (0, i))
    return pl.pallas_call(
        functools.partial(_nat_attn_kernel, rows=rows),
        grid=(d // V7X_LANES,),
        in_specs=[spec, spec, spec,
                  pl.BlockSpec((2, WIN_ROWS, GRID_W, WIN_ROWS * GRID_W), lambda i: (i, 0, 0, 0))],
        out_specs=spec,
        out_shape=jax.ShapeDtypeStruct((t, d), BF16),
        compiler_params=_cparams(1, 48),
        name="nat_attn",
    )(q, k, v, bias)


def _matmul_residual_kernel(x_ref, a_ref, w_ref, o_ref):
    o_ref[...] = x_ref[...] + _dot(a_ref[...], w_ref[...])


def _matmul_residual(x, a, w, tm=512):
    t, d = x.shape
    kdim = a.shape[1]
    return pl.pallas_call(
        _matmul_residual_kernel,
        grid=(t // tm,),
        in_specs=[
            pl.BlockSpec((tm, d), lambda i: (i, 0)),
            pl.BlockSpec((tm, kdim), lambda i: (i, 0)),
            pl.BlockSpec((kdim, d), lambda i: (0, 0)),
        ],
        out_specs=pl.BlockSpec((tm, d), lambda i: (i, 0)),
        out_shape=jax.ShapeDtypeStruct((t, d), F32),
        compiler_params=_cparams(1, 32),
        name="matmul_residual",
    )(x, a, w)


def _nat_layer(x, norm_g, w_qkv, q_norm, k_norm, rpb, w_o):
    q, k, v = _qkv(x, norm_g, w_qkv.astype(BF16), q_norm, k_norm)
    o = _nat_attn(q, k, v, _nat_bias(rpb))
    return _matmul_residual(x, o, w_o.astype(BF16))


def _router_kernel(x_ref, g_ref, w_ref, b_ref, tri_ref, ids_ref, wts_ref, cnt_ref):
    @pl.when(pl.program_id(0) == 0)
    def _():
        cnt_ref[...] = jnp.zeros(cnt_ref.shape, F32)

    h = _rms(x_ref[...], g_ref[...])
    lg = lax.dot_general(w_ref[...], h, (((1,), (1,)), ((), ())), precision=lax.Precision.HIGHEST,
                         preferred_element_type=F32) + b_ref[...]
    row = lax.broadcasted_iota(jnp.int32, lg.shape, 0)
    big = 4 * V7X_LANES
    is_g = (row >= N_EXPERTS) & (row < N_EXPERTS + N_GROUPS)
    gmax = jnp.max(jnp.where(is_g, lg, MASK_VALUE), axis=0, keepdims=True)
    gsum = jnp.sum(jnp.where(is_g, jnp.exp(lg - gmax), 0.0), axis=0, keepdims=True)
    g_val = 1.0 / gsum
    g_idx = jnp.min(jnp.where(is_g & (lg == gmax), row, big), axis=0, keepdims=True) - N_EXPERTS
    in_grp = (row < N_EXPERTS) & ((row // EXPERTS_PER_GROUP) == g_idx)
    m1 = jnp.max(jnp.where(in_grp, lg, MASK_VALUE), axis=0, keepdims=True)
    i1 = jnp.min(jnp.where(in_grp & (lg == m1), row, big), axis=0, keepdims=True)
    rest = in_grp & (row != i1)
    m2 = jnp.max(jnp.where(rest, lg, MASK_VALUE), axis=0, keepdims=True)
    i2 = jnp.min(jnp.where(rest & (lg == m2), row, big), axis=0, keepdims=True)
    z = jnp.sum(jnp.where(in_grp, jnp.exp(lg - m1), 0.0), axis=0, keepdims=True)
    p1 = 1.0 / z
    p2 = jnp.exp(m2 - m1) / z
    den = p1 + p2
    w1 = g_val * (p1 / den)
    w2 = g_val * (p2 / den)

    sel1 = (row == i1)[:N_EXPERTS]
    sel2 = (row == i2)[:N_EXPERTS]
    onehot = jnp.where(sel1 | sel2, 1.0, 0.0)
    before = _dot(onehot.astype(BF16), tri_ref[...]) + cnt_ref[:, 0:1]
    r1 = jnp.sum(jnp.where(sel1, before, 0.0), axis=0, keepdims=True)
    r2 = jnp.sum(jnp.where(sel2, before, 0.0), axis=0, keepdims=True)
    cnt_ref[...] = cnt_ref[...] + jnp.sum(onehot, axis=1, keepdims=True)

    orow = lax.broadcasted_iota(jnp.int32, ids_ref.shape, 0)
    ids_ref[...] = jnp.where(orow == 0, i1, jnp.where(orow == 1, i2, jnp.where(
        orow == 2, r1.astype(jnp.int32), jnp.where(orow == 3, r2.astype(jnp.int32), 0))))
    wts_ref[...] = jnp.where(orow == 0, w1, jnp.where(orow == 1, w2, 0.0))


def _router(x, g, w_group, b_group, w_expert, b_expert, tm=512):
    t, d = x.shape
    rows = MOE_ROUTER_ROWS
    pad = rows - N_EXPERTS - N_GROUPS
    w = jnp.pad(jnp.concatenate([w_expert, w_group], axis=1).astype(F32).T, ((0, pad), (0, 0)))
    b = jnp.pad(jnp.concatenate([b_expert, b_group]).astype(F32), (0, pad)).reshape(rows, 1)
    tri = (jnp.arange(tm)[:, None] < jnp.arange(tm)[None, :]).astype(BF16)
    return pl.pallas_call(
        _router_kernel,
        grid=(t // tm,),
        in_specs=[
            pl.BlockSpec((tm, d), lambda i: (i, 0)),
            pl.BlockSpec((1, d), lambda i: (0, 0)),
            pl.BlockSpec((rows, d), lambda i: (0, 0)),
            pl.BlockSpec((rows, 1), lambda i: (0, 0)),
            pl.BlockSpec((tm, tm), lambda i: (0, 0)),
        ],
        out_specs=[
            pl.BlockSpec((8, tm), lambda i: (0, i)),
            pl.BlockSpec((8, tm), lambda i: (0, i)),
            pl.BlockSpec((N_EXPERTS, V7X_LANES), lambda i: (0, 0)),
        ],
        out_shape=[
            jax.ShapeDtypeStruct((8, t), jnp.int32),
            jax.ShapeDtypeStruct((8, t), F32),
            jax.ShapeDtypeStruct((N_EXPERTS, V7X_LANES), F32),
        ],
        compiler_params=_cparams(1, 32),
        name="moe_router",
    )(x, g.reshape(1, d), w, b, tri)


def _moe_tables(ids, cnt, t):
    tm = MOE_ROW_TILE
    n_rows = 2 * t
    counts = cnt[:, 0].astype(jnp.int32)
    ends = jnp.cumsum(counts)
    starts = ends - counts
    pos = jnp.stack([starts[ids[0]] + ids[2], starts[ids[1]] + ids[3]])
    brk = jnp.concatenate([jnp.arange(n_rows // tm, dtype=jnp.int32) * tm, starts])
    idx = jnp.arange(brk.shape[0])
    before = (brk[None, :] < brk[:, None]) | ((brk[None, :] == brk[:, None]) & (idx[None, :] < idx[:, None]))
    rank = jnp.sum(before.astype(jnp.int32), axis=1)
    lo = jnp.sum(jnp.where(rank[None, :] == idx[:, None], brk[None, :], 0), axis=1)
    hi = jnp.concatenate([lo[1:], jnp.full((1,), n_rows, jnp.int32)])
    anchor = jnp.minimum(lo, n_rows - 1)
    tile = anchor // tm
    expert = jnp.minimum(jnp.sum((ends[None, :] <= anchor[:, None]).astype(jnp.int32), axis=1), N_EXPERTS - 1)
    one = jnp.ones((1,), jnp.int32)
    first = jnp.concatenate([one, (tile[1:] != tile[:-1]).astype(jnp.int32)])
    newexp = jnp.concatenate([one, (expert[1:] != expert[:-1]).astype(jnp.int32)])
    table = jnp.stack([tile, expert, lo - tile * tm, hi - tile * tm, first, newexp]).astype(jnp.int32)
    return pos, table


def _tile_positions(pos, tm):
    return pos.reshape(2, -1, tm).transpose(1, 0, 2)


def _moe_dispatch_kernel(pos_ref, x_ref, xs_ref, sem):
    tm = x_ref.shape[0]

    def issue(j, carry):
        for k in range(2):
            pltpu.make_async_copy(x_ref.at[pl.ds(j, 1), :], xs_ref.at[pl.ds(pos_ref[0, k, j], 1), :],
                                  sem).start(priority=k)
        return carry

    lax.fori_loop(0, tm, issue, 0, unroll=8)
    for k in range(2):
        pltpu.make_async_copy(x_ref, xs_ref.at[pl.ds(0, tm), :], sem).wait()


def _moe_dispatch(x, pos, tm=MOE_TOKEN_TILE):
    t, d = x.shape
    return pl.pallas_call(
        _moe_dispatch_kernel,
        grid=(t // tm,),
        in_specs=[
            pl.BlockSpec((1, 2, tm), lambda i: (i, 0, 0), memory_space=pltpu.SMEM),
            pl.BlockSpec((tm, d), lambda i: (i, 0)),
        ],
        out_specs=pl.BlockSpec(memory_space=pl.ANY),
        out_shape=jax.ShapeDtypeStruct((2 * t, d), F32),
        scratch_shapes=[pltpu.SemaphoreType.DMA(())],
        compiler_params=_cparams(1, 32),
        name="moe_dispatch",
    )(_tile_positions(pos, tm), x)


def _moe_ffn_kernel(tab_ref, xs_ref, g_ref, wg_ref, wu_ref, wd_ref, ys_ref, wgb_ref, wub_ref, wdb_ref):
    i = pl.program_id(0)
    lo = tab_ref[2, i]
    hi = tab_ref[3, i]

    @pl.when(tab_ref[5, i] == 1)
    def _():
        wgb_ref[...] = wg_ref[0, 0].astype(BF16)
        wub_ref[...] = wu_ref[0, 0].astype(BF16)
        wdb_ref[...] = wd_ref[0, 0].astype(BF16)

    @pl.when(tab_ref[4, i] == 1)
    def _():
        ys_ref[...] = jnp.zeros(ys_ref.shape, F32)

    @pl.when(hi > lo)
    def _():
        h = _rms(xs_ref[...], g_ref[...]).astype(BF16)
        a = _dot(h, wgb_ref[...])
        u = _dot(h, wub_ref[...])
        rowid = lax.broadcasted_iota(jnp.int32, a.shape, 0)
        act = jnp.where((rowid >= lo) & (rowid < hi), jax.nn.silu(a) * u, 0.0).astype(BF16)
        ys_ref[...] += _dot(act, wdb_ref[...])


def _moe_ffn(xs, g, table, w_gate, w_up, w_down):
    n_rows, d = xs.shape
    ff = w_gate.shape[-1]
    tm = MOE_ROW_TILE
    n_items = table.shape[1]
    epg = EXPERTS_PER_GROUP
    grid_spec = pltpu.PrefetchScalarGridSpec(
        num_scalar_prefetch=1,
        grid=(n_items,),
        in_specs=[
            pl.BlockSpec((tm, d), lambda i, tab: (tab[0, i], 0)),
            pl.BlockSpec((1, d), lambda i, tab: (0, 0)),
            pl.BlockSpec((1, 1, d, ff), lambda i, tab: (tab[1, i] // epg, tab[1, i] % epg, 0, 0)),
            pl.BlockSpec((1, 1, d, ff), lambda i, tab: (tab[1, i] // epg, tab[1, i] % epg, 0, 0)),
            pl.BlockSpec((1, 1, ff, d), lambda i, tab: (tab[1, i] // epg, tab[1, i] % epg, 0, 0)),
        ],
        out_specs=pl.BlockSpec((tm, d), lambda i, tab: (tab[0, i], 0)),
        scratch_shapes=[pltpu.VMEM((d, ff), BF16), pltpu.VMEM((d, ff), BF16), pltpu.VMEM((ff, d), BF16)],
    )
    return pl.pallas_call(
        _moe_ffn_kernel,
        grid_spec=grid_spec,
        out_shape=jax.ShapeDtypeStruct((n_rows, d), F32),
        compiler_params=_cparams(1, 32),
        name="moe_ffn",
    )(table, xs, g.reshape(1, d), w_gate, w_up, w_down)


def _combine_ple_kernel(pos_ref, x_ref, wc_ref, ys_ref, g_ref, p_ref, wg_ref, wp_ref, o_ref, ya_ref, yb_ref, sem):
    tm = x_ref.shape[0]

    def issue(j, carry):
        pltpu.make_async_copy(ys_ref.at[pl.ds(pos_ref[0, 0, j], 1), :], ya_ref.at[pl.ds(j, 1), :],
                              sem).start(priority=0)
        pltpu.make_async_copy(ys_ref.at[pl.ds(pos_ref[0, 1, j], 1), :], yb_ref.at[pl.ds(j, 1), :],
                              sem).start(priority=1)
        return carry

    lax.fori_loop(0, tm, issue, 0, unroll=8)
    proj = _dot(p_ref[...].astype(BF16), wp_ref[...])
    pltpu.make_async_copy(ys_ref.at[pl.ds(0, tm), :], ya_ref, sem).wait()
    pltpu.make_async_copy(ys_ref.at[pl.ds(0, tm), :], yb_ref, sem).wait()
    w = wc_ref[...]
    x2 = x_ref[...] + w[:, 0:1] * ya_ref[...] + w[:, 1:2] * yb_ref[...]
    h = _rms(x2, g_ref[...]).astype(BF16)
    gate = jax.nn.sigmoid(_dot(h, wg_ref[...]))
    o_ref[...] = x2 + gate * proj


def _combine_ple(x, pos, wts, ys, norm_g, p, w_proj, w_gate, tm=MOE_TOKEN_TILE):
    t, d = x.shape
    pd = p.shape[1]
    wc = wts[0:2].T
    return pl.pallas_call(
        _combine_ple_kernel,
        grid=(t // tm,),
        in_specs=[
            pl.BlockSpec((1, 2, tm), lambda i: (i, 0, 0), memory_space=pltpu.SMEM),
            pl.BlockSpec((tm, d), lambda i: (i, 0)),
            pl.BlockSpec((tm, 2), lambda i: (i, 0)),
            pl.BlockSpec(memory_space=pl.ANY),
            pl.BlockSpec((1, d), lambda i: (0, 0)),
            pl.BlockSpec((tm, pd), lambda i: (i, 0)),
            pl.BlockSpec((d, d), lambda i: (0, 0)),
            pl.BlockSpec((pd, d), lambda i: (0, 0)),
        ],
        out_specs=pl.BlockSpec((tm, d), lambda i: (i, 0)),
        out_shape=jax.ShapeDtypeStruct((t, d), F32),
        scratch_shapes=[pltpu.VMEM((tm, d), F32), pltpu.VMEM((tm, d), F32), pltpu.SemaphoreType.DMA(())],
        compiler_params=_cparams(1, 40),
        name="moe_combine_ple",
    )(_tile_positions(pos, tm), x, wc, ys, norm_g.reshape(1, d), p, w_gate.astype(BF16), w_proj.astype(BF16))


def _moe_ple_layer(x, norm_ffn, w_group, b_group, w_expert, b_expert, w_gate, w_up, w_down,
                   norm_ple, p, ple_w_proj, ple_w_gate):
    t = x.shape[0]
    ids, wts, cnt = _router(x, norm_ffn, w_group, b_group, w_expert, b_expert)
    pos, table = _moe_tables(ids, cnt, t)
    xs = _moe_dispatch(x, pos)
    ys = _moe_ffn(xs, norm_ffn, table, w_gate, w_up, w_down)
    return _combine_ple(x, pos, wts, ys, norm_ple, p, ple_w_proj, ple_w_gate)


def kernel(x, p, norm_mix, norm_ffn, norm_ple, s5_w_in, s5_lam_re, s5_lam_im, s5_log_dt, s5_b_re, s5_b_im, s5_c_re, s5_c_im, s5_d, s5_w_out, nat_w_qkv, nat_q_norm, nat_k_norm, nat_rpb, nat_w_o, moe_w_group, moe_b_group, moe_w_expert, moe_b_expert, moe_w_gate, moe_w_up, moe_w_down, ple_w_proj, ple_w_gate):
    bsz, seq, d = x.shape
    depth = p.shape[0]
    assert bsz == 1 and d == D_MODEL
    xs = x.reshape(seq, d).astype(F32)
    for i in range(depth):
        j = i // 2
        if i % 2 == 0:
            xs = _s5_layer(xs, norm_mix[i], s5_w_in[j], s5_lam_re[j], s5_lam_im[j], s5_log_dt[j],
                           s5_b_re[j], s5_b_im[j], s5_c_re[j], s5_c_im[j], s5_d[j], s5_w_out[j])
        else:
            xs = _nat_layer(xs, norm_mix[i], nat_w_qkv[j], nat_q_norm[j], nat_k_norm[j], nat_rpb[j], nat_w_o[j])
        xs = _moe_ple_layer(xs, norm_ffn[i], moe_w_group[i], moe_b_group[i], moe_w_expert[i], moe_b_expert[i],
                            moe_w_gate[i], moe_w_up[i], moe_w_down[i],
                            norm_ple[i], p[i].reshape(seq, -1), ple_w_proj[i], ple_w_gate[i])
    return xs.reshape(bsz, seq, d).astype(x.dtype)
```

```python
import functools
import math

import jax
import jax.numpy as jnp
from jax import lax
from jax.experimental import pallas as pl
from jax.experimental.pallas import tpu as pltpu

F32 = jnp.float32
BF16 = jnp.bfloat16

D_MODEL = 1024
GRID_W = 64
S5_GROUP = 16
S5_STATE = 64
S5_GROUPS = 32
N_HEADS = 16
HEAD_DIM = 64
WIN_ROWS = 8
WIN_COLS = 16
N_GROUPS = 4
EXPERTS_PER_GROUP = 8
N_EXPERTS = N_GROUPS * EXPERTS_PER_GROUP
EXPERT_FF = 256
RMS_EPS = 1e-6
MASK_VALUE = -1e30

V7X_LANES = 128
V7X_VMEM_BYTES = 64 * 1024 * 1024

S5_CHUNK = 64
S5_CHUNK_WIDTH = S5_CHUNK * S5_GROUP
S5_LAGS = 2 * S5_CHUNK
S5_LAG_LANES = S5_LAGS * S5_GROUP

NAT_ROW_UNROLL = 8

MOE_ROUTER_ROWS = 40
MOE_ROW_TILE = 256
MOE_TOKEN_TILE = 512


def _cparams(n_axes, vmem_mib):
    return pltpu.CompilerParams(
        dimension_semantics=("arbitrary",) * n_axes,
        vmem_limit_bytes=min(vmem_mib * 1024 * 1024, V7X_VMEM_BYTES - 4 * 1024 * 1024),
    )


def _dot(a, b):
    return jnp.dot(a, b, preferred_element_type=F32)


def _rms(x, g):
    ms = jnp.mean(x * x, axis=-1, keepdims=True)
    return x * lax.rsqrt(ms + RMS_EPS) * g


def _layer_weight_spec(w_stack, layer):
    _, k, n = w_stack.shape
    return pl.BlockSpec((1, k, n), lambda i: (layer, 0, 0), pipeline_mode=pl.Buffered(1))


def _cast_weight_once(w_ref, wb_ref):
    @pl.when(pl.program_id(0) == 0)
    def _():
        wb_ref[...] = w_ref[0].astype(BF16)


def _norm_matmul_kernel(x_ref, g_ref, w_ref, o_ref, wb_ref):
    _cast_weight_once(w_ref, wb_ref)
    h = _rms(x_ref[...], g_ref[...]).astype(BF16)
    o_ref[...] = _dot(h, wb_ref[...]).astype(o_ref.dtype)


def _norm_matmul(x, g, w_stack, layer, out_dtype, tm=512):
    t, d = x.shape
    n = w_stack.shape[2]
    return pl.pallas_call(
        _norm_matmul_kernel,
        grid=(t // tm,),
        in_specs=[
            pl.BlockSpec((tm, d), lambda i: (i, 0)),
            pl.BlockSpec((1, d), lambda i: (0, 0)),
            _layer_weight_spec(w_stack, layer),
        ],
        out_specs=pl.BlockSpec((tm, n), lambda i: (i, 0)),
        out_shape=jax.ShapeDtypeStruct((t, n), out_dtype),
        scratch_shapes=[pltpu.VMEM((d, n), BF16)],
        compiler_params=_cparams(1, 32),
        name="norm_matmul",
    )(x, g.reshape(1, d), w_stack)


def _s5_param_tables(lam_re, lam_im, log_dt, b_re, b_im, c_re, c_im, n_steps):
    q = S5_CHUNK
    g, p = S5_GROUPS, S5_STATE
    dt = jnp.exp(log_dt.astype(F32))[..., None]
    lam_re = lam_re.astype(F32)
    lam_im = lam_im.astype(F32)
    zr = lam_re * dt
    zi = lam_im * dt
    k = jnp.arange(q + 1, dtype=F32)[:, None]
    mag = jnp.exp(zr[:, :, None, :] * k)
    ang = zi[:, :, None, :] * k
    tr = mag * jnp.cos(ang)
    ti = mag * jnp.sin(ang)
    nr = tr[:, :, 1] - 1.0
    ni = ti[:, :, 1]
    den = lam_re * lam_re + lam_im * lam_im
    cr = (nr * lam_re + ni * lam_im) / den
    ci = (ni * lam_re - nr * lam_im) / den
    b_re = b_re.astype(F32)
    b_im = b_im.astype(F32)
    bbr = cr[..., None] * b_re - ci[..., None] * b_im
    bbi = cr[..., None] * b_im + ci[..., None] * b_re

    def both(fwd, rev):
        return jnp.concatenate([fwd, rev], axis=-1)

    def lag_rows(x):
        fwd = jnp.pad(x[0][:, :q], ((0, 0), (q - 1, 1), (0, 0)))
        rev = jnp.pad(x[1][:, :q][:, ::-1], ((0, 0), (0, q), (0, 0)))
        return both(fwd, rev)

    def f_rows(x):
        return both(x[0][:, :q][:, ::-1], x[1][:, :q])

    def e_rows(x):
        return both(x[0][:, 1:q + 1], x[1][:, 1:q + 1][:, ::-1])

    tk = jnp.stack([lag_rows(tr), lag_rows(ti)], axis=1)
    ta = jnp.stack([f_rows(tr), f_rows(ti), e_rows(tr), e_rows(ti)], axis=1)
    cq = jnp.stack([both(c_re[0], c_re[1]), both(c_im[0], c_im[1])], axis=1).astype(F32)
    bbr_t = jnp.swapaxes(bbr, 2, 3)
    bbi_t = jnp.swapaxes(bbi, 2, 3)
    bq = jnp.stack([both(bbr_t[0], bbr_t[1]), both(bbi_t[0], bbi_t[1])], axis=1)
    bm = jnp.concatenate([bq[:, 0], bq[:, 1]], axis=-1)

    steps = (q * 2.0 ** jnp.arange(n_steps, dtype=F32))[:, None, None, None]
    smag = jnp.exp(zr[None] * steps)
    sang = zi[None] * steps
    aq = jnp.stack([both(smag[:, 0] * jnp.cos(sang[:, 0]), smag[:, 1] * jnp.cos(sang[:, 1])),
                    both(smag[:, 0] * jnp.sin(sang[:, 0]), smag[:, 1] * jnp.sin(sang[:, 1]))], axis=2)
    return tk, ta, cq, bq, bm, aq.transpose(1, 0, 2, 3)


def _s5_ops_kernel(tk_ref, ta_ref, cq_ref, bq_ref, bm_ref, m_ref, f_ref, et_ref):
    w = V7X_LANES
    hh = S5_GROUP

    def outer(t, c):
        return (t[:, None, :] * c[None, :, :]).reshape(t.shape[0] * hh, w)

    cre, cim = cq_ref[0, 0], cq_ref[0, 1]
    bre, bim = bq_ref[0, 0], bq_ref[0, 1]
    tre, tim = tk_ref[0, 0], tk_ref[0, 1]
    w_re = outer(tre, cre) - outer(tim, cim)
    w_im = outer(tre, cim) + outer(tim, cre)
    clt = jnp.concatenate([w_re, -w_im], axis=1)
    bk = lax.dot_general(bm_ref[0], clt, (((1,), (1,)), ((), ())), precision=lax.Precision.HIGHEST,
                         preferred_element_type=F32)
    for m in range(S5_CHUNK):
        off = (S5_CHUNK - 1 - m) * S5_GROUP
        blk = bk if off == 0 else pltpu.roll(bk, S5_LAG_LANES - off, axis=1)
        m_ref[0, m * S5_GROUP:(m + 1) * S5_GROUP, :] = blk[:, :S5_CHUNK_WIDTH].astype(BF16)

    fr, fi, er, ei = ta_ref[0, 0], ta_ref[0, 1], ta_ref[0, 2], ta_ref[0, 3]
    f_ref[0, :, 0:w] = (outer(fr, bre) - outer(fi, bim)).astype(BF16)
    f_ref[0, :, w:2 * w] = (outer(fr, bim) + outer(fi, bre)).astype(BF16)
    et_ref[0, :, 0:w] = (outer(er, cre) - outer(ei, cim)).astype(BF16)
    et_ref[0, :, w:2 * w] = (-(outer(er, cim) + outer(ei, cre))).astype(BF16)


def _s5_ops(tk, ta, cq, bq, bm):
    g = tk.shape[0]
    p2 = 2 * S5_STATE
    cw = S5_CHUNK_WIDTH

    def spec(a):
        return pl.BlockSpec((1,) + a.shape[1:], lambda i: (i,) + (0,) * (a.ndim - 1))

    return pl.pallas_call(
        _s5_ops_kernel,
        grid=(g,),
        in_specs=[spec(tk), spec(ta), spec(cq), spec(bq), spec(bm)],
        out_specs=[
            pl.BlockSpec((1, cw, cw), lambda i: (i, 0, 0)),
            pl.BlockSpec((1, cw, 2 * p2), lambda i: (i, 0, 0)),
            pl.BlockSpec((1, cw, 2 * p2), lambda i: (i, 0, 0)),
        ],
        out_shape=[
            jax.ShapeDtypeStruct((g, cw, cw), BF16),
            jax.ShapeDtypeStruct((g, cw, 2 * p2), BF16),
            jax.ShapeDtypeStruct((g, cw, 2 * p2), BF16),
        ],
        compiler_params=_cparams(1, 32),
        name="s5_ops",
    )(tk, ta, cq, bq, bm)


def _s5_chunk_core_kernel(u_ref, m_ref, f_ref, et_ref, a_ref, y_ref, sa_ref, sb_ref, *, n_chunks, n_steps):
    w = V7X_LANES
    n = n_chunks

    @pl.when(pl.program_id(0) == 0)
    def _():
        sa_ref[...] = jnp.zeros(sa_ref.shape, F32)
        sb_ref[...] = jnp.zeros(sb_ref.shape, F32)

    fwd = lax.broadcasted_iota(jnp.int32, (n, w), 1) < S5_STATE
    u = u_ref[0]
    sa_ref[n:2 * n, :] = _dot(u, f_ref[0])

    def neighbours(ref, s, lanes):
        return jnp.where(fwd, ref[n - s:2 * n - s, lanes], ref[n + s:2 * n + s, lanes])

    re, im = slice(0, w), slice(w, 2 * w)
    src, dst = sa_ref, sb_ref
    for k in range(n_steps):
        s = 1 << k
        a = a_ref[0, k]
        ar, ai = a[0:1], a[1:2]
        pr = neighbours(src, s, re)
        pi = neighbours(src, s, im)
        dst[n:2 * n, re] = src[n:2 * n, re] + ar * pr - ai * pi
        dst[n:2 * n, im] = src[n:2 * n, im] + ar * pi + ai * pr
        src, dst = dst, src
    s_in = jnp.concatenate([neighbours(src, 1, re), neighbours(src, 1, im)], axis=1).astype(BF16)
    y_ref[0] = _dot(u, m_ref[0]) + lax.dot_general(s_in, et_ref[0], (((1,), (1,)), ((), ())),
                                                   preferred_element_type=F32)


def _s5_chunk_core(ug, mg, fg, etg, aq):
    g, n_chunks, cw = ug.shape
    n_steps = aq.shape[1]
    assert (1 << n_steps) == n_chunks
    sw = 2 * V7X_LANES
    return pl.pallas_call(
        functools.partial(_s5_chunk_core_kernel, n_chunks=n_chunks, n_steps=n_steps),
        grid=(g,),
        in_specs=[
            pl.BlockSpec((1, n_chunks, cw), lambda i: (i, 0, 0)),
            pl.BlockSpec((1, cw, cw), lambda i: (i, 0, 0)),
            pl.BlockSpec((1, cw, sw), lambda i: (i, 0, 0)),
            pl.BlockSpec((1, cw, sw), lambda i: (i, 0, 0)),
            pl.BlockSpec((1, n_steps, 2, V7X_LANES), lambda i: (i, 0, 0, 0)),
        ],
        out_specs=pl.BlockSpec((1, n_chunks, cw), lambda i: (i, 0, 0)),
        out_shape=jax.ShapeDtypeStruct((g, n_chunks, cw), F32),
        scratch_shapes=[pltpu.VMEM((3 * n_chunks, sw), F32), pltpu.VMEM((3 * n_chunks, sw), F32)],
        compiler_params=_cparams(1, 32),
        name="s5_core",
    )(ug, mg, fg, etg, aq)


def _s5_out_kernel(x_ref, y_ref, u_ref, d_ref, w_ref, o_ref, wb_ref):
    _cast_weight_once(w_ref, wb_ref)
    y = y_ref[...] + d_ref[...] * u_ref[...]
    act = jax.nn.gelu(y).astype(BF16)
    vg = _dot(act, wb_ref[...])
    o_ref[...] = x_ref[...] + vg[:, :D_MODEL] * jax.nn.sigmoid(vg[:, D_MODEL:])


def _s5_out(x, y, u, d_skip, w_out_stack, layer, tm=512):
    t, d = x.shape
    sw = y.shape[1]
    return pl.pallas_call(
        _s5_out_kernel,
        grid=(t // tm,),
        in_specs=[
            pl.BlockSpec((tm, d), lambda i: (i, 0)),
            pl.BlockSpec((tm, sw), lambda i: (i, 0)),
            pl.BlockSpec((tm, sw), lambda i: (i, 0)),
            pl.BlockSpec((1, sw), lambda i: (0, 0)),
            _layer_weight_spec(w_out_stack, layer),
        ],
        out_specs=pl.BlockSpec((tm, d), lambda i: (i, 0)),
        out_shape=jax.ShapeDtypeStruct((t, d), F32),
        scratch_shapes=[pltpu.VMEM((sw, 2 * d), BF16)],
        compiler_params=_cparams(1, 32),
        name="s5_out",
    )(x, y, u, d_skip.reshape(1, sw).astype(F32), w_out_stack)


def _s5_layer(x, norm_g, layer, w_in_stack, lam_re, lam_im, log_dt, b_re, b_im, c_re, c_im, d_skip, w_out_stack):
    t = x.shape[0]
    n_chunks = t // S5_CHUNK
    u = _norm_matmul(x, norm_g, w_in_stack, layer, F32)
    n_steps = n_chunks.bit_length() - 1
    assert (1 << n_steps) == n_chunks
    tk, ta, cq, bq, bm, aq = _s5_param_tables(lam_re, lam_im, log_dt, b_re, b_im, c_re, c_im, n_steps)
    mg, fg, etg = _s5_ops(tk, ta, cq, bq, bm)
    ug = u.astype(BF16).reshape(n_chunks, S5_CHUNK, S5_GROUPS, S5_GROUP)
    ug = ug.transpose(2, 0, 1, 3).reshape(S5_GROUPS, n_chunks, S5_CHUNK_WIDTH)
    yg = _s5_chunk_core(ug, mg, fg, etg, aq)
    y = yg.reshape(S5_GROUPS, n_chunks, S5_CHUNK, S5_GROUP).transpose(1, 2, 0, 3).reshape(t, S5_GROUPS * S5_GROUP)
    return _s5_out(x, y, u, d_skip, w_out_stack, layer)


def _qkv_kernel(x_ref, g_ref, w_ref, qg_ref, kg_ref, q_ref, k_ref, v_ref, wb_ref):
    _cast_weight_once(w_ref, wb_ref)
    h = _rms(x_ref[...], g_ref[...]).astype(BF16)
    qkv = _dot(h, wb_ref[...])
    tm = qkv.shape[0]
    lo = lax.broadcasted_iota(jnp.int32, (tm, V7X_LANES), 1) < HEAD_DIM
    scale = 1.0 / math.sqrt(HEAD_DIM)

    def headnorm(xt, gt):
        sq = xt * xt
        s0 = jnp.sum(jnp.where(lo, sq, 0.0), axis=-1, keepdims=True)
        s1 = jnp.sum(jnp.where(lo, 0.0, sq), axis=-1, keepdims=True)
        rs = jnp.where(lo, lax.rsqrt(s0 / HEAD_DIM + RMS_EPS), lax.rsqrt(s1 / HEAD_DIM + RMS_EPS))
        return xt * rs * gt

    for t in range(D_MODEL // V7X_LANES):
        lanes = slice(t * V7X_LANES, (t + 1) * V7X_LANES)
        q_ref[:, lanes] = (headnorm(qkv[:, lanes], qg_ref[...]) * scale).astype(BF16)
        klanes = slice(D_MODEL + t * V7X_LANES, D_MODEL + (t + 1) * V7X_LANES)
        k_ref[:, lanes] = headnorm(qkv[:, klanes], kg_ref[...]).astype(BF16)
    v_ref[...] = qkv[:, 2 * D_MODEL:].astype(BF16)


def _qkv(x, g, w_qkv_stack, layer, q_norm, k_norm, tm=512):
    t, d = x.shape
    qg = jnp.tile(q_norm.astype(F32), 2).reshape(1, V7X_LANES)
    kg = jnp.tile(k_norm.astype(F32), 2).reshape(1, V7X_LANES)
    out = jax.ShapeDtypeStruct((t, d), BF16)
    ospec = pl.BlockSpec((tm, d), lambda i: (i, 0))
    return pl.pallas_call(
        _qkv_kernel,
        grid=(t // tm,),
        in_specs=[
            pl.BlockSpec((tm, d), lambda i: (i, 0)),
            pl.BlockSpec((1, d), lambda i: (0, 0)),
            _layer_weight_spec(w_qkv_stack, layer),
            pl.BlockSpec((1, V7X_LANES), lambda i: (0, 0)),
            pl.BlockSpec((1, V7X_LANES), lambda i: (0, 0)),
        ],
        out_specs=[ospec, ospec, ospec],
        out_shape=[out, out, out],
        scratch_shapes=[pltpu.VMEM((d, 3 * d), BF16)],
        compiler_params=_cparams(1, 52),
        name="nat_qkv",
    )(x, g.reshape(1, d), w_qkv_stack, qg, kg)


def _nat_bias_table(rpb_ref, b_ref):
    w = V7X_LANES
    c = lax.broadcasted_iota(jnp.int32, (GRID_W, w), 0)
    lane = lax.broadcasted_iota(jnp.int32, (GRID_W, w), 1)
    lo = lane < GRID_W
    kc = jnp.where(lo, lane, lane - GRID_W)
    ws = jnp.clip(c - WIN_COLS // 2, 0, GRID_W - WIN_COLS)
    valid = (kc >= ws) & (kc < ws + WIN_COLS)
    n_ri = 2 * WIN_ROWS - 1
    for h in range(2):
        t_lo, t_hi = [], []
        for ri in range(n_ri):
            vb = jnp.broadcast_to(rpb_ref[h, ri:ri + 1, :], (GRID_W, w))
            t_lo.append(pltpu.roll(vb, w - (WIN_COLS - 1), 1, stride=1, stride_axis=0))
            t_hi.append(pltpu.roll(vb, GRID_W - (WIN_COLS - 1), 1, stride=1, stride_axis=0))
        for o in range(WIN_ROWS):
            for j in range(WIN_ROWS // 2):
                tile = jnp.where(lo, t_lo[o + 2 * j], t_hi[o + 2 * j + 1])
                b_ref[h, o, :, j * w:(j + 1) * w] = jnp.where(valid, tile, MASK_VALUE)


def _nat_attn_kernel(q_ref, k_ref, v_ref, rpb_ref, o_ref, b_ref, *, rows):
    lo = lax.broadcasted_iota(jnp.int32, (GRID_W, V7X_LANES), 1) < HEAD_DIM
    head_mask = (jnp.where(lo, 1.0, 0.0), jnp.where(lo, 0.0, 1.0))
    nkeys = WIN_ROWS * GRID_W
    _nat_bias_table(rpb_ref, b_ref)

    def body(rb, carry):
        chains = []
        for u in range(NAT_ROW_UNROLL):
            r = rb * NAT_ROW_UNROLL + u
            rs = jnp.clip(r - WIN_ROWS // 2, 0, rows - WIN_ROWS)
            off = rs - r + (WIN_ROWS - 1)
            q = q_ref[pl.ds(pl.multiple_of(r * GRID_W, GRID_W), GRID_W), :].astype(F32)
            k = k_ref[pl.ds(pl.multiple_of(rs * GRID_W, GRID_W), nkeys), :]
            for h in range(2):
                chains.append(dict(r=r, rs=rs, off=off, h=h, q=(q * head_mask[h]).astype(BF16), k=k))
        for c in chains:
            c["s"] = lax.dot_general(c["q"], c["k"], (((1,), (1,)), ((), ())), preferred_element_type=F32)
        for c in chains:
            c["s"] = c["s"] + b_ref[c["h"], c["off"]]
        for c in chains:
            c["m"] = jnp.max(c["s"], axis=-1, keepdims=True)
        for c in chains:
            c["p"] = jnp.exp(c["s"] - c["m"])
        for c in chains:
            c["l"] = jnp.sum(c["p"], axis=-1, keepdims=True)
        for c in chains:
            v = v_ref[pl.ds(pl.multiple_of(c["rs"] * GRID_W, GRID_W), nkeys), :]
            c["o"] = _dot(c["p"].astype(BF16), v) / c["l"]
        for u in range(NAT_ROW_UNROLL):
            c0, c1 = chains[2 * u], chains[2 * u + 1]
            o = jnp.where(lo, c0["o"], c1["o"])
            o_ref[pl.ds(pl.multiple_of(c0["r"] * GRID_W, GRID_W), GRID_W), :] = o.astype(BF16)
        return carry

    lax.fori_loop(0, rows // NAT_ROW_UNROLL, body, 0)


def _nat_attn(q, k, v, rpb):
    t, d = q.shape
    rows = t // GRID_W
    assert rows >= WIN_ROWS and rows % NAT_ROW_UNROLL == 0
    n_ri, n_ci = rpb.shape[1], rpb.shape[2]
    rpb_pad = jnp.pad(rpb.astype(F32), ((0, 0), (0, 2 * WIN_ROWS - n_ri), (0, V7X_LANES - n_ci)))
    spec = pl.BlockSpec((t, V7X_LANES), lambda i: (0, i))
    return pl.pallas_call(
        functools.partial(_nat_attn_kernel, rows=rows),
        grid=(d // V7X_LANES,),
        in_specs=[spec, spec, spec,
                  pl.BlockSpec((2, 2 * WIN_ROWS, V7X_LANES), lambda i: (i, 0, 0))],
        out_specs=spec,
        out_shape=jax.ShapeDtypeStruct((t, d), BF16),
        scratch_shapes=[pltpu.VMEM((2, WIN_ROWS, GRID_W, WIN_ROWS * GRID_W), F32)],
        compiler_params=_cparams(1, 48),
        name="nat_attn",
    )(q, k, v, rpb_pad)


def _matmul_residual_kernel(x_ref, a_ref, w_ref, o_ref, wb_ref):
    _cast_weight_once(w_ref, wb_ref)
    o_ref[...] = x_ref[...] + _dot(a_ref[...], wb_ref[...])


def _matmul_residual(x, a, w_stack, layer, tm=512):
    t, d = x.shape
    kdim = a.shape[1]
    return pl.pallas_call(
        _matmul_residual_kernel,
        grid=(t // tm,),
        in_specs=[
            pl.BlockSpec((tm, d), lambda i: (i, 0)),
            pl.BlockSpec((tm, kdim), lambda i: (i, 0)),
            _layer_weight_spec(w_stack, layer),
        ],
        out_specs=pl.BlockSpec((tm, d), lambda i: (i, 0)),
        out_shape=jax.ShapeDtypeStruct((t, d), F32),
        scratch_shapes=[pltpu.VMEM((kdim, d), BF16)],
        compiler_params=_cparams(1, 32),
        name="matmul_residual",
    )(x, a, w_stack)


def _nat_layer(x, norm_g, layer, w_qkv_stack, q_norm, k_norm, rpb, w_o_stack):
    q, k, v = _qkv(x, norm_g, w_qkv_stack, layer, q_norm, k_norm)
    o = _nat_attn(q, k, v, rpb)
    return _matmul_residual(x, o, w_o_stack, layer)


def _router_kernel(x_ref, g_ref, w_ref, b_ref, tri_ref, ids_ref, wts_ref, cnt_ref):
    @pl.when(pl.program_id(0) == 0)
    def _():
        cnt_ref[...] = jnp.zeros(cnt_ref.shape, F32)

    h = _rms(x_ref[...], g_ref[...])
    lg = lax.dot_general(w_ref[...], h, (((1,), (1,)), ((), ())), precision=lax.Precision.HIGHEST,
                         preferred_element_type=F32) + b_ref[...]
    row = lax.broadcasted_iota(jnp.int32, lg.shape, 0)
    big = 4 * V7X_LANES
    is_g = (row >= N_EXPERTS) & (row < N_EXPERTS + N_GROUPS)
    gmax = jnp.max(jnp.where(is_g, lg, MASK_VALUE), axis=0, keepdims=True)
    gsum = jnp.sum(jnp.where(is_g, jnp.exp(lg - gmax), 0.0), axis=0, keepdims=True)
    g_val = 1.0 / gsum
    g_idx = jnp.min(jnp.where(is_g & (lg == gmax), row, big), axis=0, keepdims=True) - N_EXPERTS
    in_grp = (row < N_EXPERTS) & ((row // EXPERTS_PER_GROUP) == g_idx)
    m1 = jnp.max(jnp.where(in_grp, lg, MASK_VALUE), axis=0, keepdims=True)
    i1 = jnp.min(jnp.where(in_grp & (lg == m1), row, big), axis=0, keepdims=True)
    rest = in_grp & (row != i1)
    m2 = jnp.max(jnp.where(rest, lg, MASK_VALUE), axis=0, keepdims=True)
    i2 = jnp.min(jnp.where(rest & (lg == m2), row, big), axis=0, keepdims=True)
    z = jnp.sum(jnp.where(in_grp, jnp.exp(lg - m1), 0.0), axis=0, keepdims=True)
    p1 = 1.0 / z
    p2 = jnp.exp(m2 - m1) / z
    den = p1 + p2
    w1 = g_val * (p1 / den)
    w2 = g_val * (p2 / den)

    sel1 = (row == i1)[:N_EXPERTS]
    sel2 = (row == i2)[:N_EXPERTS]
    onehot = jnp.where(sel1 | sel2, 1.0, 0.0)
    before = _dot(onehot.astype(BF16), tri_ref[...]) + cnt_ref[:, 0:1]
    r1 = jnp.sum(jnp.where(sel1, before, 0.0), axis=0, keepdims=True)
    r2 = jnp.sum(jnp.where(sel2, before, 0.0), axis=0, keepdims=True)
    cnt_ref[...] = cnt_ref[...] + jnp.sum(onehot, axis=1, keepdims=True)

    orow = lax.broadcasted_iota(jnp.int32, ids_ref.shape, 0)
    ids_ref[...] = jnp.where(orow == 0, i1, jnp.where(orow == 1, i2, jnp.where(
        orow == 2, r1.astype(jnp.int32), jnp.where(orow == 3, r2.astype(jnp.int32), 0))))
    wts_ref[...] = jnp.where(orow == 0, w1, jnp.where(orow == 1, w2, 0.0))


def _router(x, g, w_group, b_group, w_expert, b_expert, tm=512):
    t, d = x.shape
    rows = MOE_ROUTER_ROWS
    pad = rows - N_EXPERTS - N_GROUPS
    w = jnp.pad(jnp.concatenate([w_expert, w_group], axis=1).astype(F32).T, ((0, pad), (0, 0)))
    b = jnp.pad(jnp.concatenate([b_expert, b_group]).astype(F32), (0, pad)).reshape(rows, 1)
    tri = (jnp.arange(tm)[:, None] < jnp.arange(tm)[None, :]).astype(BF16)
    return pl.pallas_call(
        _router_kernel,
        grid=(t // tm,),
        in_specs=[
            pl.BlockSpec((tm, d), lambda i: (i, 0)),
            pl.BlockSpec((1, d), lambda i: (0, 0)),
            pl.BlockSpec((rows, d), lambda i: (0, 0)),
            pl.BlockSpec((rows, 1), lambda i: (0, 0)),
            pl.BlockSpec((tm, tm), lambda i: (0, 0)),
        ],
        out_specs=[
            pl.BlockSpec((8, tm), lambda i: (0, i)),
            pl.BlockSpec((8, tm), lambda i: (0, i)),
            pl.BlockSpec((N_EXPERTS, V7X_LANES), lambda i: (0, 0)),
        ],
        out_shape=[
            jax.ShapeDtypeStruct((8, t), jnp.int32),
            jax.ShapeDtypeStruct((8, t), F32),
            jax.ShapeDtypeStruct((N_EXPERTS, V7X_LANES), F32),
        ],
        compiler_params=_cparams(1, 32),
        name="moe_router",
    )(x, g.reshape(1, d), w, b, tri)


def _moe_tables(cnt, t):
    tm = MOE_ROW_TILE
    n_rows = 2 * t
    counts = cnt[:, 0].astype(jnp.int32)
    ends = jnp.cumsum(counts)
    starts = ends - counts
    brk = jnp.concatenate([jnp.arange(n_rows // tm, dtype=jnp.int32) * tm, starts])
    idx = jnp.arange(brk.shape[0])
    before = (brk[None, :] < brk[:, None]) | ((brk[None, :] == brk[:, None]) & (idx[None, :] < idx[:, None]))
    rank = jnp.sum(before.astype(jnp.int32), axis=1)
    lo = jnp.sum(jnp.where(rank[None, :] == idx[:, None], brk[None, :], 0), axis=1)
    hi = jnp.concatenate([lo[1:], jnp.full((1,), n_rows, jnp.int32)])
    anchor = jnp.minimum(lo, n_rows - 1)
    tile = anchor // tm
    expert = jnp.minimum(jnp.sum((ends[None, :] <= anchor[:, None]).astype(jnp.int32), axis=1), N_EXPERTS - 1)
    lo_in = lo - tile * tm
    hi_in = hi - tile * tm
    first = ((hi > lo) & (lo_in == 0)).astype(jnp.int32)
    last = ((hi > lo) & (hi_in == tm)).astype(jnp.int32)
    newexp = jnp.concatenate([jnp.ones((1,), jnp.int32), (expert[1:] != expert[:-1]).astype(jnp.int32)])
    table = jnp.stack([tile, expert, lo_in, hi_in, first, last, newexp]).astype(jnp.int32)
    return starts, table


def _moe_pos_kernel(starts_ref, ids_ref, pos_ref):
    e = ids_ref[0:2, :]
    acc = jnp.zeros(e.shape, jnp.int32)
    for ex in range(N_EXPERTS):
        acc = jnp.where(e == ex, starts_ref[ex], acc)
    pos_ref[0:2, :] = acc + ids_ref[2:4, :]
    pos_ref[2:8, :] = jnp.zeros((6, e.shape[1]), jnp.int32)


def _moe_positions(ids, starts, tm=2048):
    t = ids.shape[1]
    tm = min(tm, t)
    grid_spec = pltpu.PrefetchScalarGridSpec(
        num_scalar_prefetch=1,
        grid=(t // tm,),
        in_specs=[pl.BlockSpec((8, tm), lambda i, st: (0, i))],
        out_specs=pl.BlockSpec((8, tm), lambda i, st: (0, i)),
    )
    return pl.pallas_call(
        _moe_pos_kernel,
        grid_spec=grid_spec,
        out_shape=jax.ShapeDtypeStruct((8, t), jnp.int32),
        compiler_params=_cparams(1, 16),
        name="moe_positions",
    )(starts, ids)


def _tile_positions(pos, tm):
    return pos.reshape(2, -1, tm).transpose(1, 0, 2)


def _to_row_tiles(x, dst_ref):
    rows = x.shape[0]
    for c in range(x.shape[1] // V7X_LANES):
        dst_ref[pl.ds(c, rows, stride=8), :] = x[:, c * V7X_LANES:(c + 1) * V7X_LANES]


def _from_row_tiles(src_ref, rows):
    n = src_ref.shape[0] // rows
    return jnp.concatenate([src_ref[pl.ds(c, rows, stride=8), :] for c in range(n)], axis=1)


def _row_tile(ref, r):
    return ref.at[pl.ds(pl.multiple_of(r * 8, 8), 8), :]


def _moe_dispatch_kernel(pos_ref, x_ref, xs_ref, xt_ref, sem):
    tm = x_ref.shape[0]
    _to_row_tiles(x_ref[...], xt_ref)
    for j in range(tm):
        for k in range(2):
            pltpu.make_async_copy(_row_tile(xt_ref, j), _row_tile(xs_ref, pos_ref[0, k, j]), sem).start(priority=k)
    for k in range(2):
        pltpu.make_async_copy(xt_ref, xs_ref.at[pl.ds(0, 8 * tm), :], sem).wait()


def _moe_dispatch(x, pos, tm=MOE_TOKEN_TILE):
    t, d = x.shape
    assert d == 8 * V7X_LANES
    return pl.pallas_call(
        _moe_dispatch_kernel,
        grid=(t // tm,),
        in_specs=[
            pl.BlockSpec((1, 2, tm), lambda i: (i, 0, 0), memory_space=pltpu.SMEM),
            pl.BlockSpec((tm, d), lambda i: (i, 0)),
        ],
        out_specs=pl.BlockSpec(memory_space=pl.ANY),
        out_shape=jax.ShapeDtypeStruct((2 * t * 8, V7X_LANES), F32),
        scratch_shapes=[pltpu.VMEM((8 * tm, V7X_LANES), F32), pltpu.SemaphoreType.DMA(())],
        compiler_params=_cparams(1, 32),
        name="moe_dispatch",
    )(_tile_positions(pos, tm), x)


def _moe_ffn_kernel(tab_ref, xs_ref, g_ref, wg_ref, wu_ref, wd_ref, ys_ref, wgb_ref, wub_ref, wdb_ref, acc_ref):
    i = pl.program_id(0)
    lo = tab_ref[2, i]
    hi = tab_ref[3, i]
    first = tab_ref[4, i] == 1
    last = tab_ref[5, i] == 1
    tm = acc_ref.shape[0]

    @pl.when(tab_ref[6, i] == 1)
    def _():
        wgb_ref[...] = wg_ref[0, 0, 0].astype(BF16)
        wub_ref[...] = wu_ref[0, 0, 0].astype(BF16)
        wdb_ref[...] = wd_ref[0, 0, 0].astype(BF16)

    @pl.when(hi > lo)
    def _():
        h = _rms(_from_row_tiles(xs_ref, tm), g_ref[...]).astype(BF16)
        a = _dot(h, wgb_ref[...])
        u = _dot(h, wub_ref[...])
        rowid = lax.broadcasted_iota(jnp.int32, a.shape, 0)
        act = jnp.where((rowid >= lo) & (rowid < hi), jax.nn.silu(a) * u, 0.0).astype(BF16)
        y = _dot(act, wdb_ref[...])

        @pl.when(first & last)
        def _():
            _to_row_tiles(y, ys_ref)

        @pl.when(first & jnp.logical_not(last))
        def _():
            acc_ref[...] = y

        @pl.when(jnp.logical_not(first) & jnp.logical_not(last))
        def _():
            acc_ref[...] += y

        @pl.when(jnp.logical_not(first) & last)
        def _():
            _to_row_tiles(acc_ref[...] + y, ys_ref)


def _moe_ffn(xs, g, table, w_gate, w_up, w_down, layer):
    d, ff = w_gate.shape[-2:]
    tm = MOE_ROW_TILE
    n_items = table.shape[1]
    epg = EXPERTS_PER_GROUP
    wmap = lambda i, tab: (layer, tab[1, i] // epg, tab[1, i] % epg, 0, 0)
    grid_spec = pltpu.PrefetchScalarGridSpec(
        num_scalar_prefetch=1,
        grid=(n_items,),
        in_specs=[
            pl.BlockSpec((8 * tm, V7X_LANES), lambda i, tab: (tab[0, i], 0)),
            pl.BlockSpec((1, d), lambda i, tab: (0, 0)),
            pl.BlockSpec((1, 1, 1, d, ff), wmap),
            pl.BlockSpec((1, 1, 1, d, ff), wmap),
            pl.BlockSpec((1, 1, 1, ff, d), wmap),
        ],
        out_specs=pl.BlockSpec((8 * tm, V7X_LANES), lambda i, tab: (tab[0, i], 0)),
        scratch_shapes=[pltpu.VMEM((d, ff), BF16), pltpu.VMEM((d, ff), BF16), pltpu.VMEM((ff, d), BF16),
                        pltpu.VMEM((tm, d), F32)],
    )
    return pl.pallas_call(
        _moe_ffn_kernel,
        grid_spec=grid_spec,
        out_shape=jax.ShapeDtypeStruct(xs.shape, F32),
        compiler_params=_cparams(1, 32),
        name="moe_ffn",
    )(table, xs, g.reshape(1, d), w_gate, w_up, w_down)


def _combine_ple_kernel(pos_ref, x_ref, wc_ref, ys_ref, g_ref, p_ref, wg_ref, wp_ref, o_ref,
                        ya_ref, yb_ref, wgb_ref, wpb_ref, sem):
    tm = x_ref.shape[0]
    for j in range(tm):
        pltpu.make_async_copy(_row_tile(ys_ref, pos_ref[0, 0, j]), _row_tile(ya_ref, j), sem).start(priority=0)
        pltpu.make_async_copy(_row_tile(ys_ref, pos_ref[0, 1, j]), _row_tile(yb_ref, j), sem).start(priority=1)
    _cast_weight_once(wg_ref, wgb_ref)
    _cast_weight_once(wp_ref, wpb_ref)
    proj = _dot(p_ref[0, 0].astype(BF16), wpb_ref[...])
    pltpu.make_async_copy(ys_ref.at[pl.ds(0, 8 * tm), :], ya_ref, sem).wait()
    pltpu.make_async_copy(ys_ref.at[pl.ds(0, 8 * tm), :], yb_ref, sem).wait()
    w = wc_ref[...]
    x2 = x_ref[...] + w[:, 0:1] * _from_row_tiles(ya_ref, tm) + w[:, 1:2] * _from_row_tiles(yb_ref, tm)
    h = _rms(x2, g_ref[...]).astype(BF16)
    gate = jax.nn.sigmoid(_dot(h, wgb_ref[...]))
    o_ref[...] = x2 + gate * proj


def _combine_ple(x, pos, wts, ys, norm_g, p_stack, w_proj_stack, w_gate_stack, layer, tm=MOE_TOKEN_TILE):
    t, d = x.shape
    pd = p_stack.shape[-1]
    wc = wts[0:2].T
    return pl.pallas_call(
        _combine_ple_kernel,
        grid=(t // tm,),
        in_specs=[
            pl.BlockSpec((1, 2, tm), lambda i: (i, 0, 0), memory_space=pltpu.SMEM),
            pl.BlockSpec((tm, d), lambda i: (i, 0)),
            pl.BlockSpec((tm, 2), lambda i: (i, 0)),
            pl.BlockSpec(memory_space=pl.ANY),
            pl.BlockSpec((1, d), lambda i: (0, 0)),
            pl.BlockSpec((1, 1, tm, pd), lambda i: (layer, 0, i, 0)),
            _layer_weight_spec(w_gate_stack, layer),
            _layer_weight_spec(w_proj_stack, layer),
        ],
        out_specs=pl.BlockSpec((tm, d), lambda i: (i, 0)),
        out_shape=jax.ShapeDtypeStruct((t, d), F32),
        scratch_shapes=[pltpu.VMEM((8 * tm, V7X_LANES), F32), pltpu.VMEM((8 * tm, V7X_LANES), F32),
                        pltpu.VMEM((d, d), BF16), pltpu.VMEM((pd, d), BF16), pltpu.SemaphoreType.DMA(())],
        compiler_params=_cparams(1, 40),
        name="moe_combine_ple",
    )(_tile_positions(pos, tm), x, wc, ys, norm_g.reshape(1, d), p_stack, w_gate_stack, w_proj_stack)


def _moe_ple_layer(x, layer, norm_ffn, w_group, b_group, w_expert, b_expert, w_gate, w_up, w_down,
                   norm_ple, p_stack, ple_w_proj, ple_w_gate):
    t = x.shape[0]
    ids, wts, cnt = _router(x, norm_ffn, w_group, b_group, w_expert, b_expert)
    starts, table = _moe_tables(cnt, t)
    pos = _moe_positions(ids, starts)[0:2]
    xs = _moe_dispatch(x, pos)
    ys = _moe_ffn(xs, norm_ffn, table, w_gate, w_up, w_down, layer)
    return _combine_ple(x, pos, wts, ys, norm_ple, p_stack, ple_w_proj, ple_w_gate, layer)


def kernel(x, p, norm_mix, norm_ffn, norm_ple, s5_w_in, s5_lam_re, s5_lam_im, s5_log_dt, s5_b_re, s5_b_im, s5_c_re, s5_c_im, s5_d, s5_w_out, nat_w_qkv, nat_q_norm, nat_k_norm, nat_rpb, nat_w_o, moe_w_group, moe_b_group, moe_w_expert, moe_b_expert, moe_w_gate, moe_w_up, moe_w_down, ple_w_proj, ple_w_gate):
    bsz, seq, d = x.shape
    depth = p.shape[0]
    assert bsz == 1 and d == D_MODEL
    xs = x.reshape(seq, d).astype(F32)
    for i in range(depth):
        j = i // 2
        if i % 2 == 0:
            xs = _s5_layer(xs, norm_mix[i], j, s5_w_in, s5_lam_re[j], s5_lam_im[j], s5_log_dt[j],
                           s5_b_re[j], s5_b_im[j], s5_c_re[j], s5_c_im[j], s5_d[j], s5_w_out)
        else:
            xs = _nat_layer(xs, norm_mix[i], j, nat_w_qkv, nat_q_norm[j], nat_k_norm[j], nat_rpb[j], nat_w_o)
        xs = _moe_ple_layer(xs, i, norm_ffn[i], moe_w_group[i], moe_b_group[i], moe_w_expert[i], moe_b_expert[i],
                            moe_w_gate, moe_w_up, moe_w_down, norm_ple[i], p, ple_w_proj, ple_w_gate)
    return xs.reshape(bsz, seq, d).astype(x.dtype)
```

```python
import functools
import math

import jax
import jax.numpy as jnp
from jax import lax
from jax.experimental import pallas as pl
from jax.experimental.pallas import tpu as pltpu

F32 = jnp.float32
BF16 = jnp.bfloat16

D_MODEL = 1024
GRID_W = 64
S5_GROUP = 16
S5_STATE = 64
S5_GROUPS = 32
N_HEADS = 16
HEAD_DIM = 64
WIN_ROWS = 8
WIN_COLS = 16
N_GROUPS = 4
EXPERTS_PER_GROUP = 8
N_EXPERTS = N_GROUPS * EXPERTS_PER_GROUP
EXPERT_FF = 256
RMS_EPS = 1e-6
MASK_VALUE = -1e30

V7X_LANES = 128
V7X_VMEM_BYTES = 64 * 1024 * 1024

S5_CHUNK = 64
S5_CHUNK_WIDTH = S5_CHUNK * S5_GROUP
S5_LAGS = 2 * S5_CHUNK
S5_LAG_LANES = S5_LAGS * S5_GROUP

NAT_ROW_UNROLL = 8

MOE_ROUTER_ROWS = 40
MOE_ROW_TILE = 512
MOE_TOKEN_TILE = 512


def _cparams(n_axes, vmem_mib):
    return pltpu.CompilerParams(
        dimension_semantics=("arbitrary",) * n_axes,
        vmem_limit_bytes=min(vmem_mib * 1024 * 1024, V7X_VMEM_BYTES - 4 * 1024 * 1024),
    )


def _dot(a, b):
    return jnp.dot(a, b, preferred_element_type=F32)


def _rms(x, g):
    ms = jnp.mean(x * x, axis=-1, keepdims=True)
    return x * lax.rsqrt(ms + RMS_EPS) * g


def _layer_weight_spec(w_stack, layer):
    _, k, n = w_stack.shape
    return pl.BlockSpec((1, k, n), lambda i: (layer, 0, 0), pipeline_mode=pl.Buffered(1))


def _cast_weight_once(w_ref, wb_ref):
    @pl.when(pl.program_id(0) == 0)
    def _():
        wb_ref[...] = w_ref[0].astype(BF16)


def _norm_matmul_kernel(x_ref, g_ref, w_ref, o_ref, wb_ref):
    _cast_weight_once(w_ref, wb_ref)
    h = _rms(x_ref[...], g_ref[...]).astype(BF16)
    o_ref[...] = _dot(h, wb_ref[...]).astype(o_ref.dtype)


def _norm_matmul(x, g, w_stack, layer, out_dtype, tm=512):
    t, d = x.shape
    n = w_stack.shape[2]
    return pl.pallas_call(
        _norm_matmul_kernel,
        grid=(t // tm,),
        in_specs=[
            pl.BlockSpec((tm, d), lambda i: (i, 0)),
            pl.BlockSpec((1, d), lambda i: (0, 0)),
            _layer_weight_spec(w_stack, layer),
        ],
        out_specs=pl.BlockSpec((tm, n), lambda i: (i, 0)),
        out_shape=jax.ShapeDtypeStruct((t, n), out_dtype),
        scratch_shapes=[pltpu.VMEM((d, n), BF16)],
        compiler_params=_cparams(1, 32),
        name="norm_matmul",
    )(x, g.reshape(1, d), w_stack)


def _s5_param_tables(lam_re, lam_im, log_dt, b_re, b_im, c_re, c_im, n_steps):
    q = S5_CHUNK
    g, p = S5_GROUPS, S5_STATE
    dt = jnp.exp(log_dt.astype(F32))[..., None]
    lam_re = lam_re.astype(F32)
    lam_im = lam_im.astype(F32)
    zr = lam_re * dt
    zi = lam_im * dt
    k = jnp.arange(q + 1, dtype=F32)[:, None]
    mag = jnp.exp(zr[:, :, None, :] * k)
    ang = zi[:, :, None, :] * k
    tr = mag * jnp.cos(ang)
    ti = mag * jnp.sin(ang)
    nr = tr[:, :, 1] - 1.0
    ni = ti[:, :, 1]
    den = lam_re * lam_re + lam_im * lam_im
    cr = (nr * lam_re + ni * lam_im) / den
    ci = (ni * lam_re - nr * lam_im) / den
    b_re = b_re.astype(F32)
    b_im = b_im.astype(F32)
    bbr = cr[..., None] * b_re - ci[..., None] * b_im
    bbi = cr[..., None] * b_im + ci[..., None] * b_re

    def both(fwd, rev):
        return jnp.concatenate([fwd, rev], axis=-1)

    def lag_rows(x):
        fwd = jnp.pad(x[0][:, :q], ((0, 0), (q - 1, 1), (0, 0)))
        rev = jnp.pad(x[1][:, :q][:, ::-1], ((0, 0), (0, q), (0, 0)))
        return both(fwd, rev)

    def f_rows(x):
        return both(x[0][:, :q][:, ::-1], x[1][:, :q])

    def e_rows(x):
        return both(x[0][:, 1:q + 1], x[1][:, 1:q + 1][:, ::-1])

    tk = jnp.stack([lag_rows(tr), lag_rows(ti)], axis=1)
    ta = jnp.stack([f_rows(tr), f_rows(ti), e_rows(tr), e_rows(ti)], axis=1)
    cq = jnp.stack([both(c_re[0], c_re[1]), both(c_im[0], c_im[1])], axis=1).astype(F32)
    bbr_t = jnp.swapaxes(bbr, 2, 3)
    bbi_t = jnp.swapaxes(bbi, 2, 3)
    bq = jnp.stack([both(bbr_t[0], bbr_t[1]), both(bbi_t[0], bbi_t[1])], axis=1)
    bm = jnp.concatenate([bq[:, 0], bq[:, 1]], axis=-1)

    steps = (q * 2.0 ** jnp.arange(n_steps, dtype=F32))[:, None, None, None]
    smag = jnp.exp(zr[None] * steps)
    sang = zi[None] * steps
    aq = jnp.stack([both(smag[:, 0] * jnp.cos(sang[:, 0]), smag[:, 1] * jnp.cos(sang[:, 1])),
                    both(smag[:, 0] * jnp.sin(sang[:, 0]), smag[:, 1] * jnp.sin(sang[:, 1]))], axis=2)
    return tk, ta, cq, bq, bm, aq.transpose(1, 0, 2, 3)


def _s5_ops_kernel(tk_ref, ta_ref, cq_ref, bq_ref, bm_ref, m_ref, f_ref, et_ref):
    w = V7X_LANES
    hh = S5_GROUP

    def outer(t, c):
        return (t[:, None, :] * c[None, :, :]).reshape(t.shape[0] * hh, w)

    cre, cim = cq_ref[0, 0], cq_ref[0, 1]
    bre, bim = bq_ref[0, 0], bq_ref[0, 1]
    tre, tim = tk_ref[0, 0], tk_ref[0, 1]
    w_re = outer(tre, cre) - outer(tim, cim)
    w_im = outer(tre, cim) + outer(tim, cre)
    clt = jnp.concatenate([w_re, -w_im], axis=1)
    bk = lax.dot_general(bm_ref[0], clt, (((1,), (1,)), ((), ())), precision=lax.Precision.HIGHEST,
                         preferred_element_type=F32)
    for m in range(S5_CHUNK):
        off = (S5_CHUNK - 1 - m) * S5_GROUP
        blk = bk if off == 0 else pltpu.roll(bk, S5_LAG_LANES - off, axis=1)
        m_ref[0, m * S5_GROUP:(m + 1) * S5_GROUP, :] = blk[:, :S5_CHUNK_WIDTH].astype(BF16)

    fr, fi, er, ei = ta_ref[0, 0], ta_ref[0, 1], ta_ref[0, 2], ta_ref[0, 3]
    f_ref[0, :, 0:w] = (outer(fr, bre) - outer(fi, bim)).astype(BF16)
    f_ref[0, :, w:2 * w] = (outer(fr, bim) + outer(fi, bre)).astype(BF16)
    et_ref[0, :, 0:w] = (outer(er, cre) - outer(ei, cim)).astype(BF16)
    et_ref[0, :, w:2 * w] = (-(outer(er, cim) + outer(ei, cre))).astype(BF16)


def _s5_ops(tk, ta, cq, bq, bm):
    g = tk.shape[0]
    p2 = 2 * S5_STATE
    cw = S5_CHUNK_WIDTH

    def spec(a):
        return pl.BlockSpec((1,) + a.shape[1:], lambda i: (i,) + (0,) * (a.ndim - 1))

    return pl.pallas_call(
        _s5_ops_kernel,
        grid=(g,),
        in_specs=[spec(tk), spec(ta), spec(cq), spec(bq), spec(bm)],
        out_specs=[
            pl.BlockSpec((1, cw, cw), lambda i: (i, 0, 0)),
            pl.BlockSpec((1, cw, 2 * p2), lambda i: (i, 0, 0)),
            pl.BlockSpec((1, cw, 2 * p2), lambda i: (i, 0, 0)),
        ],
        out_shape=[
            jax.ShapeDtypeStruct((g, cw, cw), BF16),
            jax.ShapeDtypeStruct((g, cw, 2 * p2), BF16),
            jax.ShapeDtypeStruct((g, cw, 2 * p2), BF16),
        ],
        compiler_params=_cparams(1, 32),
        name="s5_ops",
    )(tk, ta, cq, bq, bm)


def _s5_chunk_core_kernel(u_ref, m_ref, f_ref, et_ref, a_ref, y_ref, sa_ref, sb_ref, *, n_chunks, n_steps):
    w = V7X_LANES
    n = n_chunks

    @pl.when(pl.program_id(0) == 0)
    def _():
        sa_ref[...] = jnp.zeros(sa_ref.shape, F32)
        sb_ref[...] = jnp.zeros(sb_ref.shape, F32)

    fwd = lax.broadcasted_iota(jnp.int32, (n, w), 1) < S5_STATE
    u = u_ref[0]
    sa_ref[n:2 * n, :] = _dot(u, f_ref[0])

    def neighbours(ref, s, lanes):
        return jnp.where(fwd, ref[n - s:2 * n - s, lanes], ref[n + s:2 * n + s, lanes])

    re, im = slice(0, w), slice(w, 2 * w)
    src, dst = sa_ref, sb_ref
    for k in range(n_steps):
        s = 1 << k
        a = a_ref[0, k]
        ar, ai = a[0:1], a[1:2]
        pr = neighbours(src, s, re)
        pi = neighbours(src, s, im)
        dst[n:2 * n, re] = src[n:2 * n, re] + ar * pr - ai * pi
        dst[n:2 * n, im] = src[n:2 * n, im] + ar * pi + ai * pr
        src, dst = dst, src
    s_in = jnp.concatenate([neighbours(src, 1, re), neighbours(src, 1, im)], axis=1).astype(BF16)
    y_ref[0] = _dot(u, m_ref[0]) + lax.dot_general(s_in, et_ref[0], (((1,), (1,)), ((), ())),
                                                   preferred_element_type=F32)


def _s5_chunk_core(ug, mg, fg, etg, aq):
    g, n_chunks, cw = ug.shape
    n_steps = aq.shape[1]
    assert (1 << n_steps) == n_chunks
    sw = 2 * V7X_LANES
    return pl.pallas_call(
        functools.partial(_s5_chunk_core_kernel, n_chunks=n_chunks, n_steps=n_steps),
        grid=(g,),
        in_specs=[
            pl.BlockSpec((1, n_chunks, cw), lambda i: (i, 0, 0)),
            pl.BlockSpec((1, cw, cw), lambda i: (i, 0, 0)),
            pl.BlockSpec((1, cw, sw), lambda i: (i, 0, 0)),
            pl.BlockSpec((1, cw, sw), lambda i: (i, 0, 0)),
            pl.BlockSpec((1, n_steps, 2, V7X_LANES), lambda i: (i, 0, 0, 0)),
        ],
        out_specs=pl.BlockSpec((1, n_chunks, cw), lambda i: (i, 0, 0)),
        out_shape=jax.ShapeDtypeStruct((g, n_chunks, cw), F32),
        scratch_shapes=[pltpu.VMEM((3 * n_chunks, sw), F32), pltpu.VMEM((3 * n_chunks, sw), F32)],
        compiler_params=_cparams(1, 32),
        name="s5_core",
    )(ug, mg, fg, etg, aq)


def _s5_out_kernel(x_ref, y_ref, u_ref, d_ref, w_ref, o_ref, wb_ref):
    _cast_weight_once(w_ref, wb_ref)
    y = y_ref[...] + d_ref[...] * u_ref[...]
    act = jax.nn.gelu(y).astype(BF16)
    vg = _dot(act, wb_ref[...])
    o_ref[...] = x_ref[...] + vg[:, :D_MODEL] * jax.nn.sigmoid(vg[:, D_MODEL:])


def _s5_out(x, y, u, d_skip, w_out_stack, layer, tm=512):
    t, d = x.shape
    sw = y.shape[1]
    return pl.pallas_call(
        _s5_out_kernel,
        grid=(t // tm,),
        in_specs=[
            pl.BlockSpec((tm, d), lambda i: (i, 0)),
            pl.BlockSpec((tm, sw), lambda i: (i, 0)),
            pl.BlockSpec((tm, sw), lambda i: (i, 0)),
            pl.BlockSpec((1, sw), lambda i: (0, 0)),
            _layer_weight_spec(w_out_stack, layer),
        ],
        out_specs=pl.BlockSpec((tm, d), lambda i: (i, 0)),
        out_shape=jax.ShapeDtypeStruct((t, d), F32),
        scratch_shapes=[pltpu.VMEM((sw, 2 * d), BF16)],
        compiler_params=_cparams(1, 32),
        name="s5_out",
    )(x, y, u, d_skip.reshape(1, sw).astype(F32), w_out_stack)


def _s5_layer(x, norm_g, layer, w_in_stack, lam_re, lam_im, log_dt, b_re, b_im, c_re, c_im, d_skip, w_out_stack):
    t = x.shape[0]
    n_chunks = t // S5_CHUNK
    u = _norm_matmul(x, norm_g, w_in_stack, layer, F32)
    n_steps = n_chunks.bit_length() - 1
    assert (1 << n_steps) == n_chunks
    tk, ta, cq, bq, bm, aq = _s5_param_tables(lam_re, lam_im, log_dt, b_re, b_im, c_re, c_im, n_steps)
    mg, fg, etg = _s5_ops(tk, ta, cq, bq, bm)
    ug = u.astype(BF16).reshape(n_chunks, S5_CHUNK, S5_GROUPS, S5_GROUP)
    ug = ug.transpose(2, 0, 1, 3).reshape(S5_GROUPS, n_chunks, S5_CHUNK_WIDTH)
    yg = _s5_chunk_core(ug, mg, fg, etg, aq)
    y = yg.reshape(S5_GROUPS, n_chunks, S5_CHUNK, S5_GROUP).transpose(1, 2, 0, 3).reshape(t, S5_GROUPS * S5_GROUP)
    return _s5_out(x, y, u, d_skip, w_out_stack, layer)


def _qkv_kernel(x_ref, g_ref, w_ref, qg_ref, kg_ref, q_ref, k_ref, v_ref, wb_ref):
    _cast_weight_once(w_ref, wb_ref)
    h = _rms(x_ref[...], g_ref[...]).astype(BF16)
    qkv = _dot(h, wb_ref[...])
    tm = qkv.shape[0]
    lo = lax.broadcasted_iota(jnp.int32, (tm, V7X_LANES), 1) < HEAD_DIM
    scale = 1.0 / math.sqrt(HEAD_DIM)

    def headnorm(xt, gt):
        sq = xt * xt
        s0 = jnp.sum(jnp.where(lo, sq, 0.0), axis=-1, keepdims=True)
        s1 = jnp.sum(jnp.where(lo, 0.0, sq), axis=-1, keepdims=True)
        rs = jnp.where(lo, lax.rsqrt(s0 / HEAD_DIM + RMS_EPS), lax.rsqrt(s1 / HEAD_DIM + RMS_EPS))
        return xt * rs * gt

    for t in range(D_MODEL // V7X_LANES):
        lanes = slice(t * V7X_LANES, (t + 1) * V7X_LANES)
        q_ref[:, lanes] = (headnorm(qkv[:, lanes], qg_ref[...]) * scale).astype(BF16)
        klanes = slice(D_MODEL + t * V7X_LANES, D_MODEL + (t + 1) * V7X_LANES)
        k_ref[:, lanes] = headnorm(qkv[:, klanes], kg_ref[...]).astype(BF16)
    v_ref[...] = qkv[:, 2 * D_MODEL:].astype(BF16)


def _qkv(x, g, w_qkv_stack, layer, q_norm, k_norm, tm=512):
    t, d = x.shape
    qg = jnp.tile(q_norm.astype(F32), 2).reshape(1, V7X_LANES)
    kg = jnp.tile(k_norm.astype(F32), 2).reshape(1, V7X_LANES)
    out = jax.ShapeDtypeStruct((t, d), BF16)
    ospec = pl.BlockSpec((tm, d), lambda i: (i, 0))
    return pl.pallas_call(
        _qkv_kernel,
        grid=(t // tm,),
        in_specs=[
            pl.BlockSpec((tm, d), lambda i: (i, 0)),
            pl.BlockSpec((1, d), lambda i: (0, 0)),
            _layer_weight_spec(w_qkv_stack, layer),
            pl.BlockSpec((1, V7X_LANES), lambda i: (0, 0)),
            pl.BlockSpec((1, V7X_LANES), lambda i: (0, 0)),
        ],
        out_specs=[ospec, ospec, ospec],
        out_shape=[out, out, out],
        scratch_shapes=[pltpu.VMEM((d, 3 * d), BF16)],
        compiler_params=_cparams(1, 52),
        name="nat_qkv",
    )(x, g.reshape(1, d), w_qkv_stack, qg, kg)


def _nat_bias_table(rpb_ref, b_ref):
    w = V7X_LANES
    c = lax.broadcasted_iota(jnp.int32, (GRID_W, w), 0)
    lane = lax.broadcasted_iota(jnp.int32, (GRID_W, w), 1)
    lo = lane < GRID_W
    kc = jnp.where(lo, lane, lane - GRID_W)
    ws = jnp.clip(c - WIN_COLS // 2, 0, GRID_W - WIN_COLS)
    valid = (kc >= ws) & (kc < ws + WIN_COLS)
    n_ri = 2 * WIN_ROWS - 1
    for h in range(2):
        t_lo, t_hi = [], []
        for ri in range(n_ri):
            vb = jnp.broadcast_to(rpb_ref[h, ri:ri + 1, :], (GRID_W, w))
            t_lo.append(pltpu.roll(vb, w - (WIN_COLS - 1), 1, stride=1, stride_axis=0))
            t_hi.append(pltpu.roll(vb, GRID_W - (WIN_COLS - 1), 1, stride=1, stride_axis=0))
        for o in range(WIN_ROWS):
            for j in range(WIN_ROWS // 2):
                tile = jnp.where(lo, t_lo[o + 2 * j], t_hi[o + 2 * j + 1])
                b_ref[h, o, :, j * w:(j + 1) * w] = jnp.where(valid, tile, MASK_VALUE)


def _nat_attn_kernel(q_ref, k_ref, v_ref, rpb_ref, o_ref, b_ref, *, rows):
    lo = lax.broadcasted_iota(jnp.int32, (GRID_W, V7X_LANES), 1) < HEAD_DIM
    head_mask = (jnp.where(lo, 1.0, 0.0), jnp.where(lo, 0.0, 1.0))
    nkeys = WIN_ROWS * GRID_W
    _nat_bias_table(rpb_ref, b_ref)

    def body(rb, carry):
        chains = []
        for u in range(NAT_ROW_UNROLL):
            r = rb * NAT_ROW_UNROLL + u
            rs = jnp.clip(r - WIN_ROWS // 2, 0, rows - WIN_ROWS)
            off = rs - r + (WIN_ROWS - 1)
            q = q_ref[pl.ds(pl.multiple_of(r * GRID_W, GRID_W), GRID_W), :].astype(F32)
            k = k_ref[pl.ds(pl.multiple_of(rs * GRID_W, GRID_W), nkeys), :]
            for h in range(2):
                chains.append(dict(r=r, rs=rs, off=off, h=h, q=(q * head_mask[h]).astype(BF16), k=k))
        for c in chains:
            c["s"] = lax.dot_general(c["q"], c["k"], (((1,), (1,)), ((), ())), preferred_element_type=F32)
        for c in chains:
            c["s"] = c["s"] + b_ref[c["h"], c["off"]]
        for c in chains:
            c["m"] = jnp.max(c["s"], axis=-1, keepdims=True)
        for c in chains:
            c["p"] = jnp.exp(c["s"] - c["m"])
        for c in chains:
            c["l"] = jnp.sum(c["p"], axis=-1, keepdims=True)
        for c in chains:
            v = v_ref[pl.ds(pl.multiple_of(c["rs"] * GRID_W, GRID_W), nkeys), :]
            c["o"] = _dot(c["p"].astype(BF16), v) / c["l"]
        for u in range(NAT_ROW_UNROLL):
            c0, c1 = chains[2 * u], chains[2 * u + 1]
            o = jnp.where(lo, c0["o"], c1["o"])
            o_ref[pl.ds(pl.multiple_of(c0["r"] * GRID_W, GRID_W), GRID_W), :] = o.astype(BF16)
        return carry

    lax.fori_loop(0, rows // NAT_ROW_UNROLL, body, 0)


def _nat_attn(q, k, v, rpb):
    t, d = q.shape
    rows = t // GRID_W
    assert rows >= WIN_ROWS and rows % NAT_ROW_UNROLL == 0
    n_ri, n_ci = rpb.shape[1], rpb.shape[2]
    rpb_pad = jnp.pad(rpb.astype(F32), ((0, 0), (0, 2 * WIN_ROWS - n_ri), (0, V7X_LANES - n_ci)))
    spec = pl.BlockSpec((t, V7X_LANES), lambda i: (0, i))
    return pl.pallas_call(
        functools.partial(_nat_attn_kernel, rows=rows),
        grid=(d // V7X_LANES,),
        in_specs=[spec, spec, spec,
                  pl.BlockSpec((2, 2 * WIN_ROWS, V7X_LANES), lambda i: (i, 0, 0))],
        out_specs=spec,
        out_shape=jax.ShapeDtypeStruct((t, d), BF16),
        scratch_shapes=[pltpu.VMEM((2, WIN_ROWS, GRID_W, WIN_ROWS * GRID_W), F32)],
        compiler_params=_cparams(1, 48),
        name="nat_attn",
    )(q, k, v, rpb_pad)


def _matmul_residual_kernel(x_ref, a_ref, w_ref, o_ref, wb_ref):
    _cast_weight_once(w_ref, wb_ref)
    o_ref[...] = x_ref[...] + _dot(a_ref[...], wb_ref[...])


def _matmul_residual(x, a, w_stack, layer, tm=512):
    t, d = x.shape
    kdim = a.shape[1]
    return pl.pallas_call(
        _matmul_residual_kernel,
        grid=(t // tm,),
        in_specs=[
            pl.BlockSpec((tm, d), lambda i: (i, 0)),
            pl.BlockSpec((tm, kdim), lambda i: (i, 0)),
            _layer_weight_spec(w_stack, layer),
        ],
        out_specs=pl.BlockSpec((tm, d), lambda i: (i, 0)),
        out_shape=jax.ShapeDtypeStruct((t, d), F32),
        scratch_shapes=[pltpu.VMEM((kdim, d), BF16)],
        compiler_params=_cparams(1, 32),
        name="matmul_residual",
    )(x, a, w_stack)


def _nat_layer(x, norm_g, layer, w_qkv_stack, q_norm, k_norm, rpb, w_o_stack):
    q, k, v = _qkv(x, norm_g, w_qkv_stack, layer, q_norm, k_norm)
    o = _nat_attn(q, k, v, rpb)
    return _matmul_residual(x, o, w_o_stack, layer)


def _router_kernel(x_ref, g_ref, w_ref, b_ref, tri_ref, ids_ref, wts_ref, cnt_ref):
    @pl.when(pl.program_id(0) == 0)
    def _():
        cnt_ref[...] = jnp.zeros(cnt_ref.shape, F32)

    h = _rms(x_ref[...], g_ref[...])
    lg = lax.dot_general(w_ref[...], h, (((1,), (1,)), ((), ())), precision=lax.Precision.HIGHEST,
                         preferred_element_type=F32) + b_ref[...]
    row = lax.broadcasted_iota(jnp.int32, lg.shape, 0)
    big = 4 * V7X_LANES
    is_g = (row >= N_EXPERTS) & (row < N_EXPERTS + N_GROUPS)
    gmax = jnp.max(jnp.where(is_g, lg, MASK_VALUE), axis=0, keepdims=True)
    gsum = jnp.sum(jnp.where(is_g, jnp.exp(lg - gmax), 0.0), axis=0, keepdims=True)
    g_val = 1.0 / gsum
    g_idx = jnp.min(jnp.where(is_g & (lg == gmax), row, big), axis=0, keepdims=True) - N_EXPERTS
    in_grp = (row < N_EXPERTS) & ((row // EXPERTS_PER_GROUP) == g_idx)
    m1 = jnp.max(jnp.where(in_grp, lg, MASK_VALUE), axis=0, keepdims=True)
    i1 = jnp.min(jnp.where(in_grp & (lg == m1), row, big), axis=0, keepdims=True)
    rest = in_grp & (row != i1)
    m2 = jnp.max(jnp.where(rest, lg, MASK_VALUE), axis=0, keepdims=True)
    i2 = jnp.min(jnp.where(rest & (lg == m2), row, big), axis=0, keepdims=True)
    z = jnp.sum(jnp.where(in_grp, jnp.exp(lg - m1), 0.0), axis=0, keepdims=True)
    p1 = 1.0 / z
    p2 = jnp.exp(m2 - m1) / z
    den = p1 + p2
    w1 = g_val * (p1 / den)
    w2 = g_val * (p2 / den)

    sel1 = (row == i1)[:N_EXPERTS]
    sel2 = (row == i2)[:N_EXPERTS]
    onehot = jnp.where(sel1 | sel2, 1.0, 0.0)
    before = _dot(onehot.astype(BF16), tri_ref[...]) + cnt_ref[:, 0:1]
    r1 = jnp.sum(jnp.where(sel1, before, 0.0), axis=0, keepdims=True)
    r2 = jnp.sum(jnp.where(sel2, before, 0.0), axis=0, keepdims=True)
    cnt_ref[...] = cnt_ref[...] + jnp.sum(onehot, axis=1, keepdims=True)

    orow = lax.broadcasted_iota(jnp.int32, ids_ref.shape, 0)
    ids_ref[...] = jnp.where(orow == 0, i1, jnp.where(orow == 1, i2, jnp.where(
        orow == 2, r1.astype(jnp.int32), jnp.where(orow == 3, r2.astype(jnp.int32), 0))))
    wts_ref[...] = jnp.where(orow == 0, w1, jnp.where(orow == 1, w2, 0.0))


def _router(x, g, w_group, b_group, w_expert, b_expert, tm=512):
    t, d = x.shape
    rows = MOE_ROUTER_ROWS
    pad = rows - N_EXPERTS - N_GROUPS
    w = jnp.pad(jnp.concatenate([w_expert, w_group], axis=1).astype(F32).T, ((0, pad), (0, 0)))
    b = jnp.pad(jnp.concatenate([b_expert, b_group]).astype(F32), (0, pad)).reshape(rows, 1)
    tri = (jnp.arange(tm)[:, None] < jnp.arange(tm)[None, :]).astype(BF16)
    return pl.pallas_call(
        _router_kernel,
        grid=(t // tm,),
        in_specs=[
            pl.BlockSpec((tm, d), lambda i: (i, 0)),
            pl.BlockSpec((1, d), lambda i: (0, 0)),
            pl.BlockSpec((rows, d), lambda i: (0, 0)),
            pl.BlockSpec((rows, 1), lambda i: (0, 0)),
            pl.BlockSpec((tm, tm), lambda i: (0, 0)),
        ],
        out_specs=[
            pl.BlockSpec((8, tm), lambda i: (0, i)),
            pl.BlockSpec((8, tm), lambda i: (0, i)),
            pl.BlockSpec((N_EXPERTS, V7X_LANES), lambda i: (0, 0)),
        ],
        out_shape=[
            jax.ShapeDtypeStruct((8, t), jnp.int32),
            jax.ShapeDtypeStruct((8, t), F32),
            jax.ShapeDtypeStruct((N_EXPERTS, V7X_LANES), F32),
        ],
        compiler_params=_cparams(1, 32),
        name="moe_router",
    )(x, g.reshape(1, d), w, b, tri)


def _moe_tables(cnt, t):
    tm = MOE_ROW_TILE
    n_rows = 2 * t
    counts = cnt[:, 0].astype(jnp.int32)
    ends = jnp.cumsum(counts)
    starts = ends - counts
    brk = jnp.concatenate([jnp.arange(n_rows // tm, dtype=jnp.int32) * tm, starts])
    idx = jnp.arange(brk.shape[0])
    before = (brk[None, :] < brk[:, None]) | ((brk[None, :] == brk[:, None]) & (idx[None, :] < idx[:, None]))
    rank = jnp.sum(before.astype(jnp.int32), axis=1)
    lo = jnp.sum(jnp.where(rank[None, :] == idx[:, None], brk[None, :], 0), axis=1)
    hi = jnp.concatenate([lo[1:], jnp.full((1,), n_rows, jnp.int32)])
    anchor = jnp.minimum(lo, n_rows - 1)
    tile = anchor // tm
    expert = jnp.minimum(jnp.sum((ends[None, :] <= anchor[:, None]).astype(jnp.int32), axis=1), N_EXPERTS - 1)
    lo_in = lo - tile * tm
    hi_in = hi - tile * tm
    first = ((hi > lo) & (lo_in == 0)).astype(jnp.int32)
    last = ((hi > lo) & (hi_in == tm)).astype(jnp.int32)
    newexp = jnp.concatenate([jnp.ones((1,), jnp.int32), (expert[1:] != expert[:-1]).astype(jnp.int32)])
    table = jnp.stack([tile, expert, lo_in, hi_in, first, last, newexp]).astype(jnp.int32)
    return starts, table


def _moe_pos_kernel(starts_ref, ids_ref, pos_ref):
    e = ids_ref[0:2, :]
    acc = jnp.zeros(e.shape, jnp.int32)
    for ex in range(N_EXPERTS):
        acc = jnp.where(e == ex, starts_ref[ex], acc)
    pos_ref[0:2, :] = acc + ids_ref[2:4, :]
    pos_ref[2:8, :] = jnp.zeros((6, e.shape[1]), jnp.int32)


def _moe_positions(ids, starts, tm=2048):
    t = ids.shape[1]
    tm = min(tm, t)
    grid_spec = pltpu.PrefetchScalarGridSpec(
        num_scalar_prefetch=1,
        grid=(t // tm,),
        in_specs=[pl.BlockSpec((8, tm), lambda i, st: (0, i))],
        out_specs=pl.BlockSpec((8, tm), lambda i, st: (0, i)),
    )
    return pl.pallas_call(
        _moe_pos_kernel,
        grid_spec=grid_spec,
        out_shape=jax.ShapeDtypeStruct((8, t), jnp.int32),
        compiler_params=_cparams(1, 16),
        name="moe_positions",
    )(starts, ids)


def _tile_positions(pos, tm):
    return pos.reshape(2, -1, tm).transpose(1, 0, 2)


def _to_row_tiles(x, dst_ref):
    rows = x.shape[0]
    for c in range(x.shape[1] // V7X_LANES):
        dst_ref[pl.ds(c, rows, stride=8), :] = x[:, c * V7X_LANES:(c + 1) * V7X_LANES]


def _from_row_tiles(src_ref, rows, base=0):
    return jnp.concatenate([src_ref[pl.ds(base + c, rows, stride=8), :] for c in range(8)], axis=1)


def _row_tile(ref, r):
    return ref.at[pl.ds(pl.multiple_of(r * 8, 8), 8), :]


def _moe_dispatch_kernel(pos_ref, x_ref, g_ref, xs_ref, xt_ref, sem):
    tm = x_ref.shape[0]
    _to_row_tiles(_rms(x_ref[...], g_ref[...]), xt_ref)
    for j in range(tm):
        for k in range(2):
            pltpu.make_async_copy(_row_tile(xt_ref, j), _row_tile(xs_ref, pos_ref[0, k, j]), sem).start(priority=k)
    for k in range(2):
        pltpu.make_async_copy(xt_ref, xs_ref.at[pl.ds(0, 8 * tm), :], sem).wait()


def _moe_dispatch(x, g, pos, tm=MOE_TOKEN_TILE):
    t, d = x.shape
    assert d == 8 * V7X_LANES
    return pl.pallas_call(
        _moe_dispatch_kernel,
        grid=(t // tm,),
        in_specs=[
            pl.BlockSpec((1, 2, tm), lambda i: (i, 0, 0), memory_space=pltpu.SMEM),
            pl.BlockSpec((tm, d), lambda i: (i, 0)),
            pl.BlockSpec((1, d), lambda i: (0, 0)),
        ],
        out_specs=pl.BlockSpec(memory_space=pl.ANY),
        out_shape=jax.ShapeDtypeStruct((2 * t * 8, V7X_LANES), F32),
        scratch_shapes=[pltpu.VMEM((8 * tm, V7X_LANES), F32), pltpu.SemaphoreType.DMA(())],
        compiler_params=_cparams(1, 32),
        name="moe_dispatch",
    )(_tile_positions(pos, tm), x, g.reshape(1, d))


def _moe_ffn_kernel(tab_ref, xs_ref, wg_ref, wu_ref, wd_ref, ys_ref, wgb_ref, wub_ref, wdb_ref, acc_ref):
    i = pl.program_id(0)
    lo = tab_ref[2, i]
    hi = tab_ref[3, i]
    first = tab_ref[4, i] == 1
    last = tab_ref[5, i] == 1
    tm = acc_ref.shape[0]

    @pl.when(tab_ref[6, i] == 1)
    def _():
        wgb_ref[...] = wg_ref[0, 0, 0].astype(BF16)
        wub_ref[...] = wu_ref[0, 0, 0].astype(BF16)
        wdb_ref[...] = wd_ref[0, 0, 0].astype(BF16)

    @pl.when(hi > lo)
    def _():
        h = _from_row_tiles(xs_ref, tm).astype(BF16)
        a = _dot(h, wgb_ref[...])
        u = _dot(h, wub_ref[...])
        rowid = lax.broadcasted_iota(jnp.int32, a.shape, 0)
        act = jnp.where((rowid >= lo) & (rowid < hi), jax.nn.silu(a) * u, 0.0).astype(BF16)
        y = _dot(act, wdb_ref[...])

        @pl.when(first & last)
        def _():
            _to_row_tiles(y, ys_ref)

        @pl.when(first & jnp.logical_not(last))
        def _():
            acc_ref[...] = y

        @pl.when(jnp.logical_not(first) & jnp.logical_not(last))
        def _():
            acc_ref[...] += y

        @pl.when(jnp.logical_not(first) & last)
        def _():
            _to_row_tiles(acc_ref[...] + y, ys_ref)


def _moe_ffn(xs, table, w_gate, w_up, w_down, layer):
    d, ff = w_gate.shape[-2:]
    tm = MOE_ROW_TILE
    n_items = table.shape[1]
    epg = EXPERTS_PER_GROUP
    wmap = lambda i, tab: (layer, tab[1, i] // epg, tab[1, i] % epg, 0, 0)
    grid_spec = pltpu.PrefetchScalarGridSpec(
        num_scalar_prefetch=1,
        grid=(n_items,),
        in_specs=[
            pl.BlockSpec((8 * tm, V7X_LANES), lambda i, tab: (tab[0, i], 0)),
            pl.BlockSpec((1, 1, 1, d, ff), wmap),
            pl.BlockSpec((1, 1, 1, d, ff), wmap),
            pl.BlockSpec((1, 1, 1, ff, d), wmap),
        ],
        out_specs=pl.BlockSpec((8 * tm, V7X_LANES), lambda i, tab: (tab[0, i], 0)),
        scratch_shapes=[pltpu.VMEM((d, ff), BF16), pltpu.VMEM((d, ff), BF16), pltpu.VMEM((ff, d), BF16),
                        pltpu.VMEM((tm, d), F32)],
    )
    return pl.pallas_call(
        _moe_ffn_kernel,
        grid_spec=grid_spec,
        out_shape=jax.ShapeDtypeStruct(xs.shape, F32),
        compiler_params=_cparams(1, 32),
        name="moe_ffn",
    )(table, xs, w_gate, w_up, w_down)


def _combine_ple_kernel(pos_ref, x_ref, wc_ref, ys_ref, g_ref, p_ref, wg_ref, wp_ref, o_ref,
                        ya_ref, yb_ref, wgb_ref, wpb_ref, sems):
    tm = x_ref.shape[0]
    hm = tm // 2
    for j in range(tm):
        sem = sems.at[j // hm]
        pltpu.make_async_copy(_row_tile(ys_ref, pos_ref[0, 0, j]), _row_tile(ya_ref, j), sem).start(priority=0)
        pltpu.make_async_copy(_row_tile(ys_ref, pos_ref[0, 1, j]), _row_tile(yb_ref, j), sem).start(priority=1)
    _cast_weight_once(wg_ref, wgb_ref)
    _cast_weight_once(wp_ref, wpb_ref)
    for half in range(2):
        rows = pl.ds(half * hm, hm)
        tiles = pl.ds(half * 8 * hm, 8 * hm)
        proj = _dot(p_ref[0, 0, rows, :].astype(BF16), wpb_ref[...])
        pltpu.make_async_copy(ys_ref.at[pl.ds(0, 8 * hm), :], ya_ref.at[tiles, :], sems.at[half]).wait()
        pltpu.make_async_copy(ys_ref.at[pl.ds(0, 8 * hm), :], yb_ref.at[tiles, :], sems.at[half]).wait()
        w = wc_ref[rows, :]
        x2 = (x_ref[rows, :] + w[:, 0:1] * _from_row_tiles(ya_ref, hm, half * 8 * hm)
              + w[:, 1:2] * _from_row_tiles(yb_ref, hm, half * 8 * hm))
        h = _rms(x2, g_ref[...]).astype(BF16)
        gate = jax.nn.sigmoid(_dot(h, wgb_ref[...]))
        o_ref[rows, :] = x2 + gate * proj


def _combine_ple(x, pos, wts, ys, norm_g, p_stack, w_proj_stack, w_gate_stack, layer, tm=MOE_TOKEN_TILE):
    t, d = x.shape
    pd = p_stack.shape[-1]
    wc = wts[0:2].T
    return pl.pallas_call(
        _combine_ple_kernel,
        grid=(t // tm,),
        in_specs=[
            pl.BlockSpec((1, 2, tm), lambda i: (i, 0, 0), memory_space=pltpu.SMEM),
            pl.BlockSpec((tm, d), lambda i: (i, 0)),
            pl.BlockSpec((tm, 2), lambda i: (i, 0)),
            pl.BlockSpec(memory_space=pl.ANY),
            pl.BlockSpec((1, d), lambda i: (0, 0)),
            pl.BlockSpec((1, 1, tm, pd), lambda i: (layer, 0, i, 0)),
            _layer_weight_spec(w_gate_stack, layer),
            _layer_weight_spec(w_proj_stack, layer),
        ],
        out_specs=pl.BlockSpec((tm, d), lambda i: (i, 0)),
        out_shape=jax.ShapeDtypeStruct((t, d), F32),
        scratch_shapes=[pltpu.VMEM((8 * tm, V7X_LANES), F32), pltpu.VMEM((8 * tm, V7X_LANES), F32),
                        pltpu.VMEM((d, d), BF16), pltpu.VMEM((pd, d), BF16), pltpu.SemaphoreType.DMA((2,))],
        compiler_params=_cparams(1, 40),
        name="moe_combine_ple",
    )(_tile_positions(pos, tm), x, wc, ys, norm_g.reshape(1, d), p_stack, w_gate_stack, w_proj_stack)


def _moe_ple_layer(x, layer, norm_ffn, w_group, b_group, w_expert, b_expert, w_gate, w_up, w_down,
                   norm_ple, p_stack, ple_w_proj, ple_w_gate):
    t = x.shape[0]
    ids, wts, cnt = _router(x, norm_ffn, w_group, b_group, w_expert, b_expert)
    starts, table = _moe_tables(cnt, t)
    pos = _moe_positions(ids, starts)[0:2]
    xs = _moe_dispatch(x, norm_ffn, pos)
    ys = _moe_ffn(xs, table, w_gate, w_up, w_down, layer)
    return _combine_ple(x, pos, wts, ys, norm_ple, p_stack, ple_w_proj, ple_w_gate, layer)


def kernel(x, p, norm_mix, norm_ffn, norm_ple, s5_w_in, s5_lam_re, s5_lam_im, s5_log_dt, s5_b_re, s5_b_im, s5_c_re, s5_c_im, s5_d, s5_w_out, nat_w_qkv, nat_q_norm, nat_k_norm, nat_rpb, nat_w_o, moe_w_group, moe_b_group, moe_w_expert, moe_b_expert, moe_w_gate, moe_w_up, moe_w_down, ple_w_proj, ple_w_gate):
    bsz, seq, d = x.shape
    depth = p.shape[0]
    assert bsz == 1 and d == D_MODEL
    xs = x.reshape(seq, d).astype(F32)
    for i in range(depth):
        j = i // 2
        if i % 2 == 0:
            xs = _s5_layer(xs, norm_mix[i], j, s5_w_in, s5_lam_re[j], s5_lam_im[j], s5_log_dt[j],
                           s5_b_re[j], s5_b_im[j], s5_c_re[j], s5_c_im[j], s5_d[j], s5_w_out)
        else:
            xs = _nat_layer(xs, norm_mix[i], j, nat_w_qkv, nat_q_norm[j], nat_k_norm[j], nat_rpb[j], nat_w_o)
        xs = _moe_ple_layer(xs, i, norm_ffn[i], moe_w_group[i], moe_b_group[i], moe_w_expert[i], moe_b_expert[i],
                            moe_w_gate, moe_w_up, moe_w_down, norm_ple[i], p, ple_w_proj, ple_w_gate)
    return xs.reshape(bsz, seq, d).astype(x.dtype)
```

```python
import functools
import math

import jax
import jax.numpy as jnp
from jax import lax
from jax.experimental import pallas as pl
from jax.experimental.pallas import tpu as pltpu

F32 = jnp.float32
BF16 = jnp.bfloat16

D_MODEL = 1024
GRID_W = 64
S5_GROUP = 16
S5_STATE = 64
S5_GROUPS = 32
N_HEADS = 16
HEAD_DIM = 64
WIN_ROWS = 8
WIN_COLS = 16
N_GROUPS = 4
EXPERTS_PER_GROUP = 8
N_EXPERTS = N_GROUPS * EXPERTS_PER_GROUP
EXPERT_FF = 256
RMS_EPS = 1e-6
MASK_VALUE = -1e30

V7X_LANES = 128
V7X_VMEM_BYTES = 64 * 1024 * 1024

S5_CHUNK = 64
S5_CHUNK_WIDTH = S5_CHUNK * S5_GROUP
S5_LAGS = 2 * S5_CHUNK
S5_LAG_LANES = S5_LAGS * S5_GROUP

NAT_ROW_UNROLL = 16

MOE_ROUTER_ROWS = 40
MOE_ROW_TILE = 512
MOE_TOKEN_TILE = 512


def _cparams(n_axes, vmem_mib):
    return pltpu.CompilerParams(
        dimension_semantics=("arbitrary",) * n_axes,
        vmem_limit_bytes=min(vmem_mib * 1024 * 1024, V7X_VMEM_BYTES - 4 * 1024 * 1024),
    )


def _dot(a, b):
    return jnp.dot(a, b, preferred_element_type=F32)


def _rms(x, g):
    ms = jnp.mean(x * x, axis=-1, keepdims=True)
    return x * lax.rsqrt(ms + RMS_EPS) * g


def _layer_weight_spec(w_stack, layer):
    _, k, n = w_stack.shape
    return pl.BlockSpec((1, k, n), lambda i: (layer, 0, 0), pipeline_mode=pl.Buffered(1))


def _cast_weight_once(w_ref, wb_ref):
    @pl.when(pl.program_id(0) == 0)
    def _():
        wb_ref[...] = w_ref[0].astype(BF16)


def _norm_matmul_kernel(x_ref, g_ref, w_ref, o_ref, wb_ref):
    _cast_weight_once(w_ref, wb_ref)
    h = _rms(x_ref[...], g_ref[...]).astype(BF16)
    o_ref[...] = _dot(h, wb_ref[...]).astype(o_ref.dtype)


def _norm_matmul(x, g, w_stack, layer, out_dtype, tm=512):
    t, d = x.shape
    n = w_stack.shape[2]
    return pl.pallas_call(
        _norm_matmul_kernel,
        grid=(t // tm,),
        in_specs=[
            pl.BlockSpec((tm, d), lambda i: (i, 0)),
            pl.BlockSpec((1, d), lambda i: (0, 0)),
            _layer_weight_spec(w_stack, layer),
        ],
        out_specs=pl.BlockSpec((tm, n), lambda i: (i, 0)),
        out_shape=jax.ShapeDtypeStruct((t, n), out_dtype),
        scratch_shapes=[pltpu.VMEM((d, n), BF16)],
        compiler_params=_cparams(1, 32),
        name="norm_matmul",
    )(x, g.reshape(1, d), w_stack)


def _s5_param_tables(lam_re, lam_im, log_dt, b_re, b_im, c_re, c_im, n_steps):
    q = S5_CHUNK
    g, p = S5_GROUPS, S5_STATE
    dt = jnp.exp(log_dt.astype(F32))[..., None]
    lam_re = lam_re.astype(F32)
    lam_im = lam_im.astype(F32)
    zr = lam_re * dt
    zi = lam_im * dt
    k = jnp.arange(q + 1, dtype=F32)[:, None]
    mag = jnp.exp(zr[:, :, None, :] * k)
    ang = zi[:, :, None, :] * k
    tr = mag * jnp.cos(ang)
    ti = mag * jnp.sin(ang)
    nr = tr[:, :, 1] - 1.0
    ni = ti[:, :, 1]
    den = lam_re * lam_re + lam_im * lam_im
    cr = (nr * lam_re + ni * lam_im) / den
    ci = (ni * lam_re - nr * lam_im) / den
    b_re = b_re.astype(F32)
    b_im = b_im.astype(F32)
    bbr = cr[..., None] * b_re - ci[..., None] * b_im
    bbi = cr[..., None] * b_im + ci[..., None] * b_re

    def both(fwd, rev):
        return jnp.concatenate([fwd, rev], axis=-1)

    def lag_rows(x):
        fwd = jnp.pad(x[0][:, :q], ((0, 0), (q - 1, 1), (0, 0)))
        rev = jnp.pad(x[1][:, :q][:, ::-1], ((0, 0), (0, q), (0, 0)))
        return both(fwd, rev)

    def f_rows(x):
        return both(x[0][:, :q][:, ::-1], x[1][:, :q])

    def e_rows(x):
        return both(x[0][:, 1:q + 1], x[1][:, 1:q + 1][:, ::-1])

    tk = jnp.stack([lag_rows(tr), lag_rows(ti)], axis=1)
    ta = jnp.stack([f_rows(tr), f_rows(ti), e_rows(tr), e_rows(ti)], axis=1)
    cq = jnp.stack([both(c_re[0], c_re[1]), both(c_im[0], c_im[1])], axis=1).astype(F32)
    bbr_t = jnp.swapaxes(bbr, 2, 3)
    bbi_t = jnp.swapaxes(bbi, 2, 3)
    bq = jnp.stack([both(bbr_t[0], bbr_t[1]), both(bbi_t[0], bbi_t[1])], axis=1)
    bm = jnp.concatenate([bq[:, 0], bq[:, 1]], axis=-1)

    steps = (q * 2.0 ** jnp.arange(n_steps, dtype=F32))[:, None, None, None]
    smag = jnp.exp(zr[None] * steps)
    sang = zi[None] * steps
    aq = jnp.stack([both(smag[:, 0] * jnp.cos(sang[:, 0]), smag[:, 1] * jnp.cos(sang[:, 1])),
                    both(smag[:, 0] * jnp.sin(sang[:, 0]), smag[:, 1] * jnp.sin(sang[:, 1]))], axis=2)
    return tk, ta, cq, bq, bm, aq.transpose(1, 0, 2, 3)


def _s5_ops_kernel(tk_ref, ta_ref, cq_ref, bq_ref, bm_ref, m_ref, f_ref, et_ref):
    w = V7X_LANES
    hh = S5_GROUP
    q = S5_CHUNK

    def outer(t, c):
        return (c[:, None, :] * t[None, :, :]).reshape(hh * t.shape[0], w)

    cre, cim = cq_ref[0, 0], cq_ref[0, 1]
    bre, bim = bq_ref[0, 0], bq_ref[0, 1]
    tre, tim = tk_ref[0, 0], tk_ref[0, 1]
    w_re = outer(tre, cre) - outer(tim, cim)
    w_im = outer(tre, cim) + outer(tim, cre)
    clt = jnp.concatenate([w_re, -w_im], axis=1)
    bk = lax.dot_general(bm_ref[0], clt, (((1,), (1,)), ((), ())), precision=lax.Precision.HIGHEST,
                         preferred_element_type=F32)
    lo = lax.broadcasted_iota(jnp.int32, (q, w), 1) < q
    for hi in range(hh):
        for j in range(hh // 2):
            ka = jnp.broadcast_to(bk[hi:hi + 1, (2 * j) * w:(2 * j + 1) * w], (q, w))
            kb = jnp.broadcast_to(bk[hi:hi + 1, (2 * j + 1) * w:(2 * j + 2) * w], (q, w))
            tile = jnp.where(lo, pltpu.roll(ka, w - (q - 1), 1, stride=1, stride_axis=0),
                             pltpu.roll(kb, 1, 1, stride=1, stride_axis=0))
            m_ref[0, hi * q:(hi + 1) * q, j * w:(j + 1) * w] = tile.astype(BF16)

    fr, fi, er, ei = ta_ref[0, 0], ta_ref[0, 1], ta_ref[0, 2], ta_ref[0, 3]
    f_ref[0, :, 0:w] = (outer(fr, bre) - outer(fi, bim)).astype(BF16)
    f_ref[0, :, w:2 * w] = (outer(fr, bim) + outer(fi, bre)).astype(BF16)
    et_ref[0, :, 0:w] = (outer(er, cre) - outer(ei, cim)).astype(BF16)
    et_ref[0, :, w:2 * w] = (-(outer(er, cim) + outer(ei, cre))).astype(BF16)


def _s5_ops(tk, ta, cq, bq, bm):
    g = tk.shape[0]
    p2 = 2 * S5_STATE
    cw = S5_CHUNK_WIDTH

    def spec(a):
        return pl.BlockSpec((1,) + a.shape[1:], lambda i: (i,) + (0,) * (a.ndim - 1))

    return pl.pallas_call(
        _s5_ops_kernel,
        grid=(g,),
        in_specs=[spec(tk), spec(ta), spec(cq), spec(bq), spec(bm)],
        out_specs=[
            pl.BlockSpec((1, cw, cw), lambda i: (i, 0, 0)),
            pl.BlockSpec((1, cw, 2 * p2), lambda i: (i, 0, 0)),
            pl.BlockSpec((1, cw, 2 * p2), lambda i: (i, 0, 0)),
        ],
        out_shape=[
            jax.ShapeDtypeStruct((g, cw, cw), BF16),
            jax.ShapeDtypeStruct((g, cw, 2 * p2), BF16),
            jax.ShapeDtypeStruct((g, cw, 2 * p2), BF16),
        ],
        compiler_params=_cparams(1, 32),
        name="s5_ops",
    )(tk, ta, cq, bq, bm)


def _s5_chunk_core_kernel(u_ref, m_ref, f_ref, et_ref, a_ref, y_ref, sa_ref, sb_ref, *, n_chunks, n_steps):
    w = V7X_LANES
    n = n_chunks

    @pl.when(pl.program_id(0) == 0)
    def _():
        sa_ref[...] = jnp.zeros(sa_ref.shape, F32)
        sb_ref[...] = jnp.zeros(sb_ref.shape, F32)

    fwd = lax.broadcasted_iota(jnp.int32, (n, w), 1) < S5_STATE
    u = _from_row_tiles(u_ref.at[0], n).astype(BF16)
    sa_ref[n:2 * n, :] = _dot(u, f_ref[0])

    def neighbours(ref, s, lanes):
        return jnp.where(fwd, ref[n - s:2 * n - s, lanes], ref[n + s:2 * n + s, lanes])

    re, im = slice(0, w), slice(w, 2 * w)
    src, dst = sa_ref, sb_ref
    for k in range(n_steps):
        s = 1 << k
        a = a_ref[0, k]
        ar, ai = a[0:1], a[1:2]
        pr = neighbours(src, s, re)
        pi = neighbours(src, s, im)
        dst[n:2 * n, re] = src[n:2 * n, re] + ar * pr - ai * pi
        dst[n:2 * n, im] = src[n:2 * n, im] + ar * pi + ai * pr
        src, dst = dst, src
    s_in = jnp.concatenate([neighbours(src, 1, re), neighbours(src, 1, im)], axis=1).astype(BF16)
    y = _dot(u, m_ref[0]) + lax.dot_general(s_in, et_ref[0], (((1,), (1,)), ((), ())),
                                            preferred_element_type=F32)
    _to_row_tiles(y, y_ref.at[0])


def _s5_chunk_core(ug, mg, fg, etg, aq):
    g = ug.shape[0]
    n_chunks = ug.shape[1] // 8
    cw = S5_CHUNK_WIDTH
    n_steps = aq.shape[1]
    assert (1 << n_steps) == n_chunks
    sw = 2 * V7X_LANES
    return pl.pallas_call(
        functools.partial(_s5_chunk_core_kernel, n_chunks=n_chunks, n_steps=n_steps),
        grid=(g,),
        in_specs=[
            pl.BlockSpec((1, 8 * n_chunks, V7X_LANES), lambda i: (i, 0, 0)),
            pl.BlockSpec((1, cw, cw), lambda i: (i, 0, 0)),
            pl.BlockSpec((1, cw, sw), lambda i: (i, 0, 0)),
            pl.BlockSpec((1, cw, sw), lambda i: (i, 0, 0)),
            pl.BlockSpec((1, n_steps, 2, V7X_LANES), lambda i: (i, 0, 0, 0)),
        ],
        out_specs=pl.BlockSpec((1, 8 * n_chunks, V7X_LANES), lambda i: (i, 0, 0)),
        out_shape=jax.ShapeDtypeStruct((g, 8 * n_chunks, V7X_LANES), F32),
        scratch_shapes=[pltpu.VMEM((3 * n_chunks, sw), F32), pltpu.VMEM((3 * n_chunks, sw), F32)],
        compiler_params=_cparams(1, 32),
        name="s5_core",
    )(ug, mg, fg, etg, aq)


S5_RELAYOUT_TOKENS = 1024


def _s5_to_groups_kernel(u_ref, o_ref, xt_ref):
    w = V7X_LANES
    q = S5_CHUNK
    lo = lax.broadcasted_iota(jnp.int32, (8, w), 1) < q
    n_blk = S5_RELAYOUT_TOKENS // w
    for sg in range(u_ref.shape[1] // w):
        for b in range(n_blk):
            xt_ref[b * w:(b + 1) * w, :] = u_ref[b * w:(b + 1) * w, sg * w:(sg + 1) * w].T
        for gp in range(w // S5_GROUP):
            for hp in range(S5_GROUP // 2):
                a0 = xt_ref[pl.ds(gp * S5_GROUP + 2 * hp, n_blk, stride=w), :]
                a1 = xt_ref[pl.ds(gp * S5_GROUP + 2 * hp + 1, n_blk, stride=w), :]
                even = jnp.where(lo, a0, pltpu.roll(a1, q, 1))
                odd = jnp.where(lo, pltpu.roll(a0, q, 1), a1)
                g = sg * (w // S5_GROUP) + gp
                o_ref[g, pl.ds(hp, n_blk, stride=16), :] = even
                o_ref[g, pl.ds(8 + hp, n_blk, stride=16), :] = odd


def _s5_to_groups(u):
    t, ch = u.shape
    tt = S5_RELAYOUT_TOKENS
    rows = 8 * tt // S5_CHUNK
    return pl.pallas_call(
        _s5_to_groups_kernel,
        grid=(t // tt,),
        in_specs=[pl.BlockSpec((tt, ch), lambda i: (i, 0))],
        out_specs=pl.BlockSpec((S5_GROUPS, rows, V7X_LANES), lambda i: (0, i, 0)),
        out_shape=jax.ShapeDtypeStruct((S5_GROUPS, 8 * t // S5_CHUNK, V7X_LANES), F32),
        scratch_shapes=[pltpu.VMEM((tt, V7X_LANES), F32)],
        compiler_params=_cparams(1, 16),
        name="s5_to_groups",
    )(u)


def _s5_to_tokens_kernel(y_ref, o_ref, yt_ref):
    w = V7X_LANES
    q = S5_CHUNK
    lo = lax.broadcasted_iota(jnp.int32, (8, w), 1) < q
    n_blk = S5_RELAYOUT_TOKENS // w
    for sg in range(o_ref.shape[1] // w):
        for gp in range(w // S5_GROUP):
            for hp in range(S5_GROUP // 2):
                g = sg * (w // S5_GROUP) + gp
                even = y_ref[g, pl.ds(hp, n_blk, stride=16), :]
                odd = y_ref[g, pl.ds(8 + hp, n_blk, stride=16), :]
                yt_ref[pl.ds(gp * S5_GROUP + 2 * hp, n_blk, stride=w), :] = jnp.where(lo, even, pltpu.roll(odd, q, 1))
                yt_ref[pl.ds(gp * S5_GROUP + 2 * hp + 1, n_blk, stride=w), :] = jnp.where(lo, pltpu.roll(even, q, 1), odd)
        for b in range(n_blk):
            o_ref[b * w:(b + 1) * w, sg * w:(sg + 1) * w] = yt_ref[b * w:(b + 1) * w, :].T


def _s5_to_tokens(yg):
    g, rows_all, w = yg.shape
    tt = S5_RELAYOUT_TOKENS
    rows = 8 * tt // S5_CHUNK
    t = rows_all // 8 * S5_CHUNK
    return pl.pallas_call(
        _s5_to_tokens_kernel,
        grid=(t // tt,),
        in_specs=[pl.BlockSpec((g, rows, w), lambda i: (0, i, 0))],
        out_specs=pl.BlockSpec((tt, g * S5_GROUP), lambda i: (i, 0)),
        out_shape=jax.ShapeDtypeStruct((t, g * S5_GROUP), F32),
        scratch_shapes=[pltpu.VMEM((tt, w), F32)],
        compiler_params=_cparams(1, 16),
        name="s5_to_tokens",
    )(yg)


def _s5_out_kernel(x_ref, y_ref, u_ref, d_ref, w_ref, o_ref, wb_ref):
    _cast_weight_once(w_ref, wb_ref)
    y = y_ref[...] + d_ref[...] * u_ref[...]
    act = jax.nn.gelu(y).astype(BF16)
    vg = _dot(act, wb_ref[...])
    o_ref[...] = x_ref[...] + vg[:, :D_MODEL] * jax.nn.sigmoid(vg[:, D_MODEL:])


def _s5_out(x, y, u, d_skip, w_out_stack, layer, tm=512):
    t, d = x.shape
    sw = y.shape[1]
    return pl.pallas_call(
        _s5_out_kernel,
        grid=(t // tm,),
        in_specs=[
            pl.BlockSpec((tm, d), lambda i: (i, 0)),
            pl.BlockSpec((tm, sw), lambda i: (i, 0)),
            pl.BlockSpec((tm, sw), lambda i: (i, 0)),
            pl.BlockSpec((1, sw), lambda i: (0, 0)),
            _layer_weight_spec(w_out_stack, layer),
        ],
        out_specs=pl.BlockSpec((tm, d), lambda i: (i, 0)),
        out_shape=jax.ShapeDtypeStruct((t, d), F32),
        scratch_shapes=[pltpu.VMEM((sw, 2 * d), BF16)],
        compiler_params=_cparams(1, 32),
        name="s5_out",
    )(x, y, u, d_skip.reshape(1, sw).astype(F32), w_out_stack)


def _s5_layer(x, norm_g, layer, w_in_stack, lam_re, lam_im, log_dt, b_re, b_im, c_re, c_im, d_skip, w_out_stack):
    t = x.shape[0]
    n_chunks = t // S5_CHUNK
    u = _norm_matmul(x, norm_g, w_in_stack, layer, F32)
    n_steps = n_chunks.bit_length() - 1
    assert (1 << n_steps) == n_chunks
    tk, ta, cq, bq, bm, aq = _s5_param_tables(lam_re, lam_im, log_dt, b_re, b_im, c_re, c_im, n_steps)
    mg, fg, etg = _s5_ops(tk, ta, cq, bq, bm)
    yg = _s5_chunk_core(_s5_to_groups(u), mg, fg, etg, aq)
    y = _s5_to_tokens(yg)
    return _s5_out(x, y, u, d_skip, w_out_stack, layer)


def _qkv_kernel(x_ref, g_ref, w_ref, qg_ref, kg_ref, q_ref, k_ref, v_ref, wb_ref):
    _cast_weight_once(w_ref, wb_ref)
    h = _rms(x_ref[...], g_ref[...]).astype(BF16)
    qkv = _dot(h, wb_ref[...])
    tm = qkv.shape[0]
    lo = lax.broadcasted_iota(jnp.int32, (tm, V7X_LANES), 1) < HEAD_DIM
    scale = 1.0 / math.sqrt(HEAD_DIM)

    def headnorm(xt, gt):
        sq = xt * xt
        s0 = jnp.sum(jnp.where(lo, sq, 0.0), axis=-1, keepdims=True)
        s1 = jnp.sum(jnp.where(lo, 0.0, sq), axis=-1, keepdims=True)
        rs = jnp.where(lo, lax.rsqrt(s0 / HEAD_DIM + RMS_EPS), lax.rsqrt(s1 / HEAD_DIM + RMS_EPS))
        return xt * rs * gt

    for t in range(D_MODEL // V7X_LANES):
        lanes = slice(t * V7X_LANES, (t + 1) * V7X_LANES)
        q_ref[:, lanes] = (headnorm(qkv[:, lanes], qg_ref[...]) * scale).astype(BF16)
        klanes = slice(D_MODEL + t * V7X_LANES, D_MODEL + (t + 1) * V7X_LANES)
        k_ref[:, lanes] = headnorm(qkv[:, klanes], kg_ref[...]).astype(BF16)
    v_ref[...] = qkv[:, 2 * D_MODEL:].astype(BF16)


def _qkv(x, g, w_qkv_stack, layer, q_norm, k_norm, tm=512):
    t, d = x.shape
    qg = jnp.tile(q_norm.astype(F32), 2).reshape(1, V7X_LANES)
    kg = jnp.tile(k_norm.astype(F32), 2).reshape(1, V7X_LANES)
    out = jax.ShapeDtypeStruct((t, d), BF16)
    ospec = pl.BlockSpec((tm, d), lambda i: (i, 0))
    return pl.pallas_call(
        _qkv_kernel,
        grid=(t // tm,),
        in_specs=[
            pl.BlockSpec((tm, d), lambda i: (i, 0)),
            pl.BlockSpec((1, d), lambda i: (0, 0)),
            _layer_weight_spec(w_qkv_stack, layer),
            pl.BlockSpec((1, V7X_LANES), lambda i: (0, 0)),
            pl.BlockSpec((1, V7X_LANES), lambda i: (0, 0)),
        ],
        out_specs=[ospec, ospec, ospec],
        out_shape=[out, out, out],
        scratch_shapes=[pltpu.VMEM((d, 3 * d), BF16)],
        compiler_params=_cparams(1, 52),
        name="nat_qkv",
    )(x, g.reshape(1, d), w_qkv_stack, qg, kg)


def _nat_bias_table(rpb_ref, b_ref):
    w = V7X_LANES
    c = lax.broadcasted_iota(jnp.int32, (GRID_W, w), 0)
    lane = lax.broadcasted_iota(jnp.int32, (GRID_W, w), 1)
    lo = lane < GRID_W
    kc = jnp.where(lo, lane, lane - GRID_W)
    ws = jnp.clip(c - WIN_COLS // 2, 0, GRID_W - WIN_COLS)
    valid = (kc >= ws) & (kc < ws + WIN_COLS)
    n_ri = 2 * WIN_ROWS - 1
    for h in range(2):
        t_lo, t_hi = [], []
        for ri in range(n_ri):
            vb = jnp.broadcast_to(rpb_ref[h, ri:ri + 1, :], (GRID_W, w))
            t_lo.append(pltpu.roll(vb, w - (WIN_COLS - 1), 1, stride=1, stride_axis=0))
            t_hi.append(pltpu.roll(vb, GRID_W - (WIN_COLS - 1), 1, stride=1, stride_axis=0))
        for o in range(WIN_ROWS):
            for j in range(WIN_ROWS // 2):
                tile = jnp.where(lo, t_lo[o + 2 * j], t_hi[o + 2 * j + 1])
                b_ref[h, o, :, j * w:(j + 1) * w] = jnp.where(valid, tile, MASK_VALUE)


def _nat_attn_kernel(q_ref, k_ref, v_ref, rpb_ref, o_ref, b_ref, *, rows):
    lo = lax.broadcasted_iota(jnp.int32, (GRID_W, V7X_LANES), 1) < HEAD_DIM
    head_mask = (jnp.where(lo, 1.0, 0.0), jnp.where(lo, 0.0, 1.0))
    nkeys = WIN_ROWS * GRID_W
    _nat_bias_table(rpb_ref, b_ref)

    def body(rb, carry):
        chains = []
        for u in range(NAT_ROW_UNROLL):
            r = rb * NAT_ROW_UNROLL + u
            rs = jnp.clip(r - WIN_ROWS // 2, 0, rows - WIN_ROWS)
            off = rs - r + (WIN_ROWS - 1)
            q = q_ref[pl.ds(pl.multiple_of(r * GRID_W, GRID_W), GRID_W), :].astype(F32)
            k = k_ref[pl.ds(pl.multiple_of(rs * GRID_W, GRID_W), nkeys), :]
            for h in range(2):
                chains.append(dict(r=r, rs=rs, off=off, h=h, q=(q * head_mask[h]).astype(BF16), k=k))
        for c in chains:
            c["s"] = lax.dot_general(c["q"], c["k"], (((1,), (1,)), ((), ())), preferred_element_type=F32)
        for c in chains:
            c["s"] = c["s"] + b_ref[c["h"], c["off"]]
        for c in chains:
            c["m"] = jnp.max(c["s"], axis=-1, keepdims=True)
        for c in chains:
            c["p"] = jnp.exp(c["s"] - c["m"])
        for c in chains:
            c["l"] = jnp.sum(c["p"], axis=-1, keepdims=True)
        for c in chains:
            v = v_ref[pl.ds(pl.multiple_of(c["rs"] * GRID_W, GRID_W), nkeys), :]
            c["o"] = _dot(c["p"].astype(BF16), v) / c["l"]
        for u in range(NAT_ROW_UNROLL):
            c0, c1 = chains[2 * u], chains[2 * u + 1]
            o = jnp.where(lo, c0["o"], c1["o"])
            o_ref[pl.ds(pl.multiple_of(c0["r"] * GRID_W, GRID_W), GRID_W), :] = o.astype(BF16)
        return carry

    lax.fori_loop(0, rows // NAT_ROW_UNROLL, body, 0)


def _nat_attn(q, k, v, rpb):
    t, d = q.shape
    rows = t // GRID_W
    assert rows >= WIN_ROWS and rows % NAT_ROW_UNROLL == 0
    n_ri, n_ci = rpb.shape[1], rpb.shape[2]
    rpb_pad = jnp.pad(rpb.astype(F32), ((0, 0), (0, 2 * WIN_ROWS - n_ri), (0, V7X_LANES - n_ci)))
    spec = pl.BlockSpec((t, V7X_LANES), lambda i: (0, i))
    return pl.pallas_call(
        functools.partial(_nat_attn_kernel, rows=rows),
        grid=(d // V7X_LANES,),
        in_specs=[spec, spec, spec,
                  pl.BlockSpec((2, 2 * WIN_ROWS, V7X_LANES), lambda i: (i, 0, 0))],
        out_specs=spec,
        out_shape=jax.ShapeDtypeStruct((t, d), BF16),
        scratch_shapes=[pltpu.VMEM((2, WIN_ROWS, GRID_W, WIN_ROWS * GRID_W), F32)],
        compiler_params=_cparams(1, 48),
        name="nat_attn",
    )(q, k, v, rpb_pad)


def _matmul_residual_kernel(x_ref, a_ref, w_ref, o_ref, wb_ref):
    _cast_weight_once(w_ref, wb_ref)
    o_ref[...] = x_ref[...] + _dot(a_ref[...], wb_ref[...])


def _matmul_residual(x, a, w_stack, layer, tm=512):
    t, d = x.shape
    kdim = a.shape[1]
    return pl.pallas_call(
        _matmul_residual_kernel,
        grid=(t // tm,),
        in_specs=[
            pl.BlockSpec((tm, d), lambda i: (i, 0)),
            pl.BlockSpec((tm, kdim), lambda i: (i, 0)),
            _layer_weight_spec(w_stack, layer),
        ],
        out_specs=pl.BlockSpec((tm, d), lambda i: (i, 0)),
        out_shape=jax.ShapeDtypeStruct((t, d), F32),
        scratch_shapes=[pltpu.VMEM((kdim, d), BF16)],
        compiler_params=_cparams(1, 32),
        name="matmul_residual",
    )(x, a, w_stack)


def _nat_layer(x, norm_g, layer, w_qkv_stack, q_norm, k_norm, rpb, w_o_stack):
    q, k, v = _qkv(x, norm_g, w_qkv_stack, layer, q_norm, k_norm)
    o = _nat_attn(q, k, v, rpb)
    return _matmul_residual(x, o, w_o_stack, layer)


def _router_kernel(x_ref, g_ref, w_ref, b_ref, tri_ref, ids_ref, wts_ref, cnt_ref):
    @pl.when(pl.program_id(0) == 0)
    def _():
        cnt_ref[...] = jnp.zeros(cnt_ref.shape, F32)

    def split(a):
        hi = a.astype(BF16)
        return hi, (a - hi.astype(F32)).astype(BF16)

    def dot_nt(a, b):
        return lax.dot_general(a, b, (((1,), (1,)), ((), ())), preferred_element_type=F32)

    h_hi, h_lo = split(_rms(x_ref[...], g_ref[...]))
    w_hi, w_lo = split(w_ref[...])
    lg = dot_nt(w_hi, h_hi) + (dot_nt(w_hi, h_lo) + dot_nt(w_lo, h_hi)) + b_ref[...]
    row = lax.broadcasted_iota(jnp.int32, lg.shape, 0)
    big = 4 * V7X_LANES
    is_g = (row >= N_EXPERTS) & (row < N_EXPERTS + N_GROUPS)
    gmax = jnp.max(jnp.where(is_g, lg, MASK_VALUE), axis=0, keepdims=True)
    gsum = jnp.sum(jnp.where(is_g, jnp.exp(lg - gmax), 0.0), axis=0, keepdims=True)
    g_val = 1.0 / gsum
    g_idx = jnp.min(jnp.where(is_g & (lg == gmax), row, big), axis=0, keepdims=True) - N_EXPERTS
    in_grp = (row < N_EXPERTS) & ((row // EXPERTS_PER_GROUP) == g_idx)
    m1 = jnp.max(jnp.where(in_grp, lg, MASK_VALUE), axis=0, keepdims=True)
    i1 = jnp.min(jnp.where(in_grp & (lg == m1), row, big), axis=0, keepdims=True)
    rest = in_grp & (row != i1)
    m2 = jnp.max(jnp.where(rest, lg, MASK_VALUE), axis=0, keepdims=True)
    i2 = jnp.min(jnp.where(rest & (lg == m2), row, big), axis=0, keepdims=True)
    z = jnp.sum(jnp.where(in_grp, jnp.exp(lg - m1), 0.0), axis=0, keepdims=True)
    p1 = 1.0 / z
    p2 = jnp.exp(m2 - m1) / z
    den = p1 + p2
    w1 = g_val * (p1 / den)
    w2 = g_val * (p2 / den)

    sel1 = (row == i1)[:N_EXPERTS]
    sel2 = (row == i2)[:N_EXPERTS]
    onehot = jnp.where(sel1 | sel2, 1.0, 0.0)
    before = _dot(onehot.astype(BF16), tri_ref[...]) + cnt_ref[:, 0:1]
    r1 = jnp.sum(jnp.where(sel1, before, 0.0), axis=0, keepdims=True)
    r2 = jnp.sum(jnp.where(sel2, before, 0.0), axis=0, keepdims=True)
    cnt_ref[...] = cnt_ref[...] + jnp.sum(onehot, axis=1, keepdims=True)

    orow = lax.broadcasted_iota(jnp.int32, ids_ref.shape, 0)
    ids_ref[...] = jnp.where(orow == 0, i1, jnp.where(orow == 1, i2, jnp.where(
        orow == 2, r1.astype(jnp.int32), jnp.where(orow == 3, r2.astype(jnp.int32), 0))))
    wts_ref[...] = jnp.where(orow == 0, w1, jnp.where(orow == 1, w2, 0.0))


def _router(x, g, w_group, b_group, w_expert, b_expert, tm=512):
    t, d = x.shape
    rows = MOE_ROUTER_ROWS
    pad = rows - N_EXPERTS - N_GROUPS
    w = jnp.pad(jnp.concatenate([w_expert, w_group], axis=1).astype(F32).T, ((0, pad), (0, 0)))
    b = jnp.pad(jnp.concatenate([b_expert, b_group]).astype(F32), (0, pad)).reshape(rows, 1)
    tri = (jnp.arange(tm)[:, None] < jnp.arange(tm)[None, :]).astype(BF16)
    return pl.pallas_call(
        _router_kernel,
        grid=(t // tm,),
        in_specs=[
            pl.BlockSpec((tm, d), lambda i: (i, 0)),
            pl.BlockSpec((1, d), lambda i: (0, 0)),
            pl.BlockSpec((rows, d), lambda i: (0, 0)),
            pl.BlockSpec((rows, 1), lambda i: (0, 0)),
            pl.BlockSpec((tm, tm), lambda i: (0, 0)),
        ],
        out_specs=[
            pl.BlockSpec((8, tm), lambda i: (0, i)),
            pl.BlockSpec((8, tm), lambda i: (0, i)),
            pl.BlockSpec((N_EXPERTS, V7X_LANES), lambda i: (0, 0)),
        ],
        out_shape=[
            jax.ShapeDtypeStruct((8, t), jnp.int32),
            jax.ShapeDtypeStruct((8, t), F32),
            jax.ShapeDtypeStruct((N_EXPERTS, V7X_LANES), F32),
        ],
        compiler_params=_cparams(1, 32),
        name="moe_router",
    )(x, g.reshape(1, d), w, b, tri)


def _moe_tables(cnt, t):
    tm = MOE_ROW_TILE
    n_rows = 2 * t
    counts = cnt[:, 0].astype(jnp.int32)
    ends = jnp.cumsum(counts)
    starts = ends - counts
    brk = jnp.concatenate([jnp.arange(n_rows // tm, dtype=jnp.int32) * tm, starts])
    idx = jnp.arange(brk.shape[0])
    before = (brk[None, :] < brk[:, None]) | ((brk[None, :] == brk[:, None]) & (idx[None, :] < idx[:, None]))
    rank = jnp.sum(before.astype(jnp.int32), axis=1)
    lo = jnp.sum(jnp.where(rank[None, :] == idx[:, None], brk[None, :], 0), axis=1)
    hi = jnp.concatenate([lo[1:], jnp.full((1,), n_rows, jnp.int32)])
    anchor = jnp.minimum(lo, n_rows - 1)
    tile = anchor // tm
    expert = jnp.minimum(jnp.sum((ends[None, :] <= anchor[:, None]).astype(jnp.int32), axis=1), N_EXPERTS - 1)
    lo_in = lo - tile * tm
    hi_in = hi - tile * tm
    first = ((hi > lo) & (lo_in == 0)).astype(jnp.int32)
    last = ((hi > lo) & (hi_in == tm)).astype(jnp.int32)
    newexp = jnp.concatenate([jnp.ones((1,), jnp.int32), (expert[1:] != expert[:-1]).astype(jnp.int32)])
    table = jnp.stack([tile, expert, lo_in, hi_in, first, last, newexp]).astype(jnp.int32)
    return starts, table


def _moe_pos_kernel(starts_ref, ids_ref, pos_ref):
    e = ids_ref[0:2, :]
    acc = jnp.zeros(e.shape, jnp.int32)
    for ex in range(N_EXPERTS):
        acc = jnp.where(e == ex, starts_ref[ex], acc)
    pos_ref[0:2, :] = acc + ids_ref[2:4, :]
    pos_ref[2:8, :] = jnp.zeros((6, e.shape[1]), jnp.int32)


def _moe_positions(ids, starts, tm=2048):
    t = ids.shape[1]
    tm = min(tm, t)
    grid_spec = pltpu.PrefetchScalarGridSpec(
        num_scalar_prefetch=1,
        grid=(t // tm,),
        in_specs=[pl.BlockSpec((8, tm), lambda i, st: (0, i))],
        out_specs=pl.BlockSpec((8, tm), lambda i, st: (0, i)),
    )
    return pl.pallas_call(
        _moe_pos_kernel,
        grid_spec=grid_spec,
        out_shape=jax.ShapeDtypeStruct((8, t), jnp.int32),
        compiler_params=_cparams(1, 16),
        name="moe_positions",
    )(starts, ids)


def _tile_positions(pos, tm):
    return pos.reshape(2, -1, tm).transpose(1, 0, 2)


def _to_row_tiles(x, dst_ref):
    rows = x.shape[0]
    for c in range(x.shape[1] // V7X_LANES):
        dst_ref[pl.ds(c, rows, stride=8), :] = x[:, c * V7X_LANES:(c + 1) * V7X_LANES]


def _from_row_tiles(src_ref, rows, base=0):
    return jnp.concatenate([src_ref[pl.ds(base + c, rows, stride=8), :] for c in range(8)], axis=1)


def _row_tile(ref, r):
    return ref.at[pl.ds(pl.multiple_of(r * 8, 8), 8), :]


def _moe_dispatch_kernel(pos_ref, x_ref, g_ref, xs_ref, xt_ref, sem):
    tm = x_ref.shape[0]
    _to_row_tiles(_rms(x_ref[...], g_ref[...]), xt_ref)
    for j in range(tm):
        for k in range(2):
            pltpu.make_async_copy(_row_tile(xt_ref, j), _row_tile(xs_ref, pos_ref[0, k, j]), sem).start(priority=k)
    for k in range(2):
        pltpu.make_async_copy(xt_ref, xs_ref.at[pl.ds(0, 8 * tm), :], sem).wait()


def _moe_dispatch(x, g, pos, tm=MOE_TOKEN_TILE):
    t, d = x.shape
    assert d == 8 * V7X_LANES
    return pl.pallas_call(
        _moe_dispatch_kernel,
        grid=(t // tm,),
        in_specs=[
            pl.BlockSpec((1, 2, tm), lambda i: (i, 0, 0), memory_space=pltpu.SMEM),
            pl.BlockSpec((tm, d), lambda i: (i, 0)),
            pl.BlockSpec((1, d), lambda i: (0, 0)),
        ],
        out_specs=pl.BlockSpec(memory_space=pl.ANY),
        out_shape=jax.ShapeDtypeStruct((2 * t * 8, V7X_LANES), F32),
        scratch_shapes=[pltpu.VMEM((8 * tm, V7X_LANES), F32), pltpu.SemaphoreType.DMA(())],
        compiler_params=_cparams(1, 32),
        name="moe_dispatch",
    )(_tile_positions(pos, tm), x, g.reshape(1, d))


def _moe_ffn_kernel(tab_ref, xs_ref, wg_ref, wu_ref, wd_ref, ys_ref, wgb_ref, wub_ref, wdb_ref, acc_ref):
    i = pl.program_id(0)
    lo = tab_ref[2, i]
    hi = tab_ref[3, i]
    first = tab_ref[4, i] == 1
    last = tab_ref[5, i] == 1
    tm = acc_ref.shape[0]

    @pl.when(tab_ref[6, i] == 1)
    def _():
        wgb_ref[...] = wg_ref[0, 0, 0].astype(BF16)
        wub_ref[...] = wu_ref[0, 0, 0].astype(BF16)
        wdb_ref[...] = wd_ref[0, 0, 0].astype(BF16)

    @pl.when(hi > lo)
    def _():
        h = _from_row_tiles(xs_ref, tm).astype(BF16)
        a = _dot(h, wgb_ref[...])
        u = _dot(h, wub_ref[...])
        rowid = lax.broadcasted_iota(jnp.int32, a.shape, 0)
        act = jnp.where((rowid >= lo) & (rowid < hi), jax.nn.silu(a) * u, 0.0).astype(BF16)
        y = _dot(act, wdb_ref[...])

        @pl.when(first & last)
        def _():
            _to_row_tiles(y, ys_ref)

        @pl.when(first & jnp.logical_not(last))
        def _():
            acc_ref[...] = y

        @pl.when(jnp.logical_not(first) & jnp.logical_not(last))
        def _():
            acc_ref[...] += y

        @pl.when(jnp.logical_not(first) & last)
        def _():
            _to_row_tiles(acc_ref[...] + y, ys_ref)


def _moe_ffn(xs, table, w_gate, w_up, w_down, layer):
    d, ff = w_gate.shape[-2:]
    tm = MOE_ROW_TILE
    n_items = table.shape[1]
    epg = EXPERTS_PER_GROUP
    wmap = lambda i, tab: (layer, tab[1, i] // epg, tab[1, i] % epg, 0, 0)
    grid_spec = pltpu.PrefetchScalarGridSpec(
        num_scalar_prefetch=1,
        grid=(n_items,),
        in_specs=[
            pl.BlockSpec((8 * tm, V7X_LANES), lambda i, tab: (tab[0, i], 0)),
            pl.BlockSpec((1, 1, 1, d, ff), wmap),
            pl.BlockSpec((1, 1, 1, d, ff), wmap),
            pl.BlockSpec((1, 1, 1, ff, d), wmap),
        ],
        out_specs=pl.BlockSpec((8 * tm, V7X_LANES), lambda i, tab: (tab[0, i], 0)),
        scratch_shapes=[pltpu.VMEM((d, ff), BF16), pltpu.VMEM((d, ff), BF16), pltpu.VMEM((ff, d), BF16),
                        pltpu.VMEM((tm, d), F32)],
    )
    return pl.pallas_call(
        _moe_ffn_kernel,
        grid_spec=grid_spec,
        out_shape=jax.ShapeDtypeStruct(xs.shape, F32),
        compiler_params=_cparams(1, 32),
        name="moe_ffn",
    )(table, xs, w_gate, w_up, w_down)


def _combine_ple_kernel(pos_ref, x_ref, wc_ref, ys_ref, g_ref, p_ref, wg_ref, wp_ref, o_ref,
                        ya_ref, yb_ref, wgb_ref, wpb_ref, sems):
    tm = x_ref.shape[0]
    hm = tm // 2
    for j in range(tm):
        sem = sems.at[j // hm]
        pltpu.make_async_copy(_row_tile(ys_ref, pos_ref[0, 0, j]), _row_tile(ya_ref, j), sem).start(priority=0)
        pltpu.make_async_copy(_row_tile(ys_ref, pos_ref[0, 1, j]), _row_tile(yb_ref, j), sem).start(priority=1)
    _cast_weight_once(wg_ref, wgb_ref)
    _cast_weight_once(wp_ref, wpb_ref)
    for half in range(2):
        rows = pl.ds(half * hm, hm)
        tiles = pl.ds(half * 8 * hm, 8 * hm)
        proj = _dot(p_ref[0, 0, rows, :].astype(BF16), wpb_ref[...])
        pltpu.make_async_copy(ys_ref.at[pl.ds(0, 8 * hm), :], ya_ref.at[tiles, :], sems.at[half]).wait()
        pltpu.make_async_copy(ys_ref.at[pl.ds(0, 8 * hm), :], yb_ref.at[tiles, :], sems.at[half]).wait()
        w = wc_ref[rows, :]
        x2 = (x_ref[rows, :] + w[:, 0:1] * _from_row_tiles(ya_ref, hm, half * 8 * hm)
              + w[:, 1:2] * _from_row_tiles(yb_ref, hm, half * 8 * hm))
        h = _rms(x2, g_ref[...]).astype(BF16)
        gate = jax.nn.sigmoid(_dot(h, wgb_ref[...]))
        o_ref[rows, :] = x2 + gate * proj


def _combine_ple(x, pos, wts, ys, norm_g, p_stack, w_proj_stack, w_gate_stack, layer, tm=MOE_TOKEN_TILE):
    t, d = x.shape
    pd = p_stack.shape[-1]
    wc = wts[0:2].T
    return pl.pallas_call(
        _combine_ple_kernel,
        grid=(t // tm,),
        in_specs=[
            pl.BlockSpec((1, 2, tm), lambda i: (i, 0, 0), memory_space=pltpu.SMEM),
            pl.BlockSpec((tm, d), lambda i: (i, 0)),
            pl.BlockSpec((tm, 2), lambda i: (i, 0)),
            pl.BlockSpec(memory_space=pl.ANY),
            pl.BlockSpec((1, d), lambda i: (0, 0)),
            pl.BlockSpec((1, 1, tm, pd), lambda i: (layer, 0, i, 0)),
            _layer_weight_spec(w_gate_stack, layer),
            _layer_weight_spec(w_proj_stack, layer),
        ],
        out_specs=pl.BlockSpec((tm, d), lambda i: (i, 0)),
        out_shape=jax.ShapeDtypeStruct((t, d), F32),
        scratch_shapes=[pltpu.VMEM((8 * tm, V7X_LANES), F32), pltpu.VMEM((8 * tm, V7X_LANES), F32),
                        pltpu.VMEM((d, d), BF16), pltpu.VMEM((pd, d), BF16), pltpu.SemaphoreType.DMA((2,))],
        compiler_params=_cparams(1, 40),
        name="moe_combine_ple",
    )(_tile_positions(pos, tm), x, wc, ys, norm_g.reshape(1, d), p_stack, w_gate_stack, w_proj_stack)


def _moe_ple_layer(x, layer, norm_ffn, w_group, b_group, w_expert, b_expert, w_gate, w_up, w_down,
                   norm_ple, p_stack, ple_w_proj, ple_w_gate):
    t = x.shape[0]
    ids, wts, cnt = _router(x, norm_ffn, w_group, b_group, w_expert, b_expert)
    starts, table = _moe_tables(cnt, t)
    pos = _moe_positions(ids, starts)[0:2]
    xs = _moe_dispatch(x, norm_ffn, pos)
    ys = _moe_ffn(xs, table, w_gate, w_up, w_down, layer)
    return _combine_ple(x, pos, wts, ys, norm_ple, p_stack, ple_w_proj, ple_w_gate, layer)


def kernel(x, p, norm_mix, norm_ffn, norm_ple, s5_w_in, s5_lam_re, s5_lam_im, s5_log_dt, s5_b_re, s5_b_im, s5_c_re, s5_c_im, s5_d, s5_w_out, nat_w_qkv, nat_q_norm, nat_k_norm, nat_rpb, nat_w_o, moe_w_group, moe_b_group, moe_w_expert, moe_b_expert, moe_w_gate, moe_w_up, moe_w_down, ple_w_proj, ple_w_gate):
    bsz, seq, d = x.shape
    depth = p.shape[0]
    assert bsz == 1 and d == D_MODEL
    xs = x.reshape(seq, d).astype(F32)
    for i in range(depth):
        j = i // 2
        if i % 2 == 0:
            xs = _s5_layer(xs, norm_mix[i], j, s5_w_in, s5_lam_re[j], s5_lam_im[j], s5_log_dt[j],
                           s5_b_re[j], s5_b_im[j], s5_c_re[j], s5_c_im[j], s5_d[j], s5_w_out)
        else:
            xs = _nat_layer(xs, norm_mix[i], j, nat_w_qkv, nat_q_norm[j], nat_k_norm[j], nat_rpb[j], nat_w_o)
        xs = _moe_ple_layer(xs, i, norm_ffn[i], moe_w_group[i], moe_b_group[i], moe_w_expert[i], moe_b_expert[i],
                            moe_w_gate, moe_w_up, moe_w_down, norm_ple[i], p, ple_w_proj, ple_w_gate)
    return xs.reshape(bsz, seq, d).astype(x.dtype)
```

```python
import functools
import math

import jax
import jax.numpy as jnp
from jax import lax
from jax.experimental import pallas as pl
from jax.experimental.pallas import tpu as pltpu

F32 = jnp.float32
BF16 = jnp.bfloat16

D_MODEL = 1024
GRID_W = 64
S5_GROUP = 16
S5_STATE = 64
S5_GROUPS = 32
N_HEADS = 16
HEAD_DIM = 64
WIN_ROWS = 8
WIN_COLS = 16
N_GROUPS = 4
EXPERTS_PER_GROUP = 8
N_EXPERTS = N_GROUPS * EXPERTS_PER_GROUP
EXPERT_FF = 256
RMS_EPS = 1e-6
MASK_VALUE = -1e30

V7X_LANES = 128
V7X_VMEM_BYTES = 64 * 1024 * 1024

S5_CHUNK = 64
S5_CHUNK_WIDTH = S5_CHUNK * S5_GROUP
S5_LAGS = 2 * S5_CHUNK
S5_LAG_LANES = S5_LAGS * S5_GROUP

NAT_ROW_UNROLL = 32

MOE_ROUTER_ROWS = 40
MOE_ROW_TILE = 512
MOE_TOKEN_TILE = 512


def _cparams(n_axes, vmem_mib):
    return pltpu.CompilerParams(
        dimension_semantics=("arbitrary",) * n_axes,
        vmem_limit_bytes=min(vmem_mib * 1024 * 1024, V7X_VMEM_BYTES - 4 * 1024 * 1024),
    )


def _dot(a, b):
    return jnp.dot(a, b, preferred_element_type=F32)


def _rms(x, g):
    ms = jnp.mean(x * x, axis=-1, keepdims=True)
    return x * lax.rsqrt(ms + RMS_EPS) * g


def _layer_weight_spec(w_stack, layer):
    _, k, n = w_stack.shape
    return pl.BlockSpec((1, k, n), lambda i: (layer, 0, 0), pipeline_mode=pl.Buffered(1))


def _cast_weight_once(w_ref, wb_ref):
    @pl.when(pl.program_id(0) == 0)
    def _():
        wb_ref[...] = w_ref[0].astype(BF16)


def _norm_matmul_kernel(x_ref, g_ref, w_ref, o_ref, wb_ref):
    _cast_weight_once(w_ref, wb_ref)
    h = _rms(x_ref[...], g_ref[...]).astype(BF16)
    o_ref[...] = _dot(h, wb_ref[...]).astype(o_ref.dtype)


def _norm_matmul(x, g, w_stack, layer, out_dtype, tm=512):
    t, d = x.shape
    n = w_stack.shape[2]
    return pl.pallas_call(
        _norm_matmul_kernel,
        grid=(t // tm,),
        in_specs=[
            pl.BlockSpec((tm, d), lambda i: (i, 0)),
            pl.BlockSpec((1, d), lambda i: (0, 0)),
            _layer_weight_spec(w_stack, layer),
        ],
        out_specs=pl.BlockSpec((tm, n), lambda i: (i, 0)),
        out_shape=jax.ShapeDtypeStruct((t, n), out_dtype),
        scratch_shapes=[pltpu.VMEM((d, n), BF16)],
        compiler_params=_cparams(1, 32),
        name="norm_matmul",
    )(x, g.reshape(1, d), w_stack)


def _s5_param_tables(lam_re, lam_im, log_dt, b_re, b_im, c_re, c_im, n_steps):
    q = S5_CHUNK
    g, p = S5_GROUPS, S5_STATE
    dt = jnp.exp(log_dt.astype(F32))[..., None]
    lam_re = lam_re.astype(F32)
    lam_im = lam_im.astype(F32)
    zr = lam_re * dt
    zi = lam_im * dt
    k = jnp.arange(q + 1, dtype=F32)[:, None]
    mag = jnp.exp(zr[:, :, None, :] * k)
    ang = zi[:, :, None, :] * k
    tr = mag * jnp.cos(ang)
    ti = mag * jnp.sin(ang)
    nr = tr[:, :, 1] - 1.0
    ni = ti[:, :, 1]
    den = lam_re * lam_re + lam_im * lam_im
    cr = (nr * lam_re + ni * lam_im) / den
    ci = (ni * lam_re - nr * lam_im) / den
    b_re = b_re.astype(F32)
    b_im = b_im.astype(F32)
    bbr = cr[..., None] * b_re - ci[..., None] * b_im
    bbi = cr[..., None] * b_im + ci[..., None] * b_re

    def both(fwd, rev):
        return jnp.concatenate([fwd, rev], axis=-1)

    def lag_rows(x):
        fwd = jnp.pad(x[0][:, :q], ((0, 0), (q - 1, 1), (0, 0)))
        rev = jnp.pad(x[1][:, :q][:, ::-1], ((0, 0), (0, q), (0, 0)))
        return both(fwd, rev)

    def f_rows(x):
        return both(x[0][:, :q][:, ::-1], x[1][:, :q])

    def e_rows(x):
        return both(x[0][:, 1:q + 1], x[1][:, 1:q + 1][:, ::-1])

    tk = jnp.stack([lag_rows(tr), lag_rows(ti)], axis=1)
    ta = jnp.stack([f_rows(tr), f_rows(ti), e_rows(tr), e_rows(ti)], axis=1)
    cq = jnp.stack([both(c_re[0], c_re[1]), both(c_im[0], c_im[1])], axis=1).astype(F32)
    bbr_t = jnp.swapaxes(bbr, 2, 3)
    bbi_t = jnp.swapaxes(bbi, 2, 3)
    bq = jnp.stack([both(bbr_t[0], bbr_t[1]), both(bbi_t[0], bbi_t[1])], axis=1)
    bm = jnp.concatenate([bq[:, 0], bq[:, 1]], axis=-1)

    steps = (q * 2.0 ** jnp.arange(n_steps, dtype=F32))[:, None, None, None]
    smag = jnp.exp(zr[None] * steps)
    sang = zi[None] * steps
    aq = jnp.stack([both(smag[:, 0] * jnp.cos(sang[:, 0]), smag[:, 1] * jnp.cos(sang[:, 1])),
                    both(smag[:, 0] * jnp.sin(sang[:, 0]), smag[:, 1] * jnp.sin(sang[:, 1]))], axis=2)
    return tk, ta, cq, bq, bm, aq.transpose(1, 0, 2, 3)


def _s5_ops_kernel(tk_ref, ta_ref, cq_ref, bq_ref, bm_ref, m_ref, f_ref, et_ref):
    w = V7X_LANES
    hh = S5_GROUP
    q = S5_CHUNK

    def outer(t, c):
        return (c[:, None, :] * t[None, :, :]).reshape(hh * t.shape[0], w)

    cre, cim = cq_ref[0, 0], cq_ref[0, 1]
    bre, bim = bq_ref[0, 0], bq_ref[0, 1]
    tre, tim = tk_ref[0, 0], tk_ref[0, 1]
    w_re = outer(tre, cre) - outer(tim, cim)
    w_im = outer(tre, cim) + outer(tim, cre)
    clt = jnp.concatenate([w_re, -w_im], axis=1)
    bk = lax.dot_general(bm_ref[0], clt, (((1,), (1,)), ((), ())), precision=lax.Precision.HIGHEST,
                         preferred_element_type=F32)
    lo = lax.broadcasted_iota(jnp.int32, (q, w), 1) < q
    for hi in range(hh):
        for j in range(hh // 2):
            ka = jnp.broadcast_to(bk[hi:hi + 1, (2 * j) * w:(2 * j + 1) * w], (q, w))
            kb = jnp.broadcast_to(bk[hi:hi + 1, (2 * j + 1) * w:(2 * j + 2) * w], (q, w))
            tile = jnp.where(lo, pltpu.roll(ka, w - (q - 1), 1, stride=1, stride_axis=0),
                             pltpu.roll(kb, 1, 1, stride=1, stride_axis=0))
            m_ref[0, hi * q:(hi + 1) * q, j * w:(j + 1) * w] = tile.astype(BF16)

    fr, fi, er, ei = ta_ref[0, 0], ta_ref[0, 1], ta_ref[0, 2], ta_ref[0, 3]
    f_ref[0, :, 0:w] = (outer(fr, bre) - outer(fi, bim)).astype(BF16)
    f_ref[0, :, w:2 * w] = (outer(fr, bim) + outer(fi, bre)).astype(BF16)
    et_ref[0, :, 0:w] = (outer(er, cre) - outer(ei, cim)).astype(BF16)
    et_ref[0, :, w:2 * w] = (-(outer(er, cim) + outer(ei, cre))).astype(BF16)


def _s5_ops(tk, ta, cq, bq, bm):
    g = tk.shape[0]
    p2 = 2 * S5_STATE
    cw = S5_CHUNK_WIDTH

    def spec(a):
        return pl.BlockSpec((1,) + a.shape[1:], lambda i: (i,) + (0,) * (a.ndim - 1))

    return pl.pallas_call(
        _s5_ops_kernel,
        grid=(g,),
        in_specs=[spec(tk), spec(ta), spec(cq), spec(bq), spec(bm)],
        out_specs=[
            pl.BlockSpec((1, cw, cw), lambda i: (i, 0, 0)),
            pl.BlockSpec((1, cw, 2 * p2), lambda i: (i, 0, 0)),
            pl.BlockSpec((1, cw, 2 * p2), lambda i: (i, 0, 0)),
        ],
        out_shape=[
            jax.ShapeDtypeStruct((g, cw, cw), BF16),
            jax.ShapeDtypeStruct((g, cw, 2 * p2), BF16),
            jax.ShapeDtypeStruct((g, cw, 2 * p2), BF16),
        ],
        compiler_params=_cparams(1, 32),
        name="s5_ops",
    )(tk, ta, cq, bq, bm)


def _s5_chunk_core_kernel(u_ref, m_ref, f_ref, et_ref, a_ref, y_ref, sa_ref, sb_ref, *, n_chunks, n_steps):
    w = V7X_LANES
    n = n_chunks

    @pl.when(pl.program_id(0) == 0)
    def _():
        sa_ref[...] = jnp.zeros(sa_ref.shape, F32)
        sb_ref[...] = jnp.zeros(sb_ref.shape, F32)

    fwd = lax.broadcasted_iota(jnp.int32, (n, w), 1) < S5_STATE
    u = _from_row_tiles(u_ref.at[0], n).astype(BF16)
    sa_ref[n:2 * n, :] = _dot(u, f_ref[0])

    def neighbours(ref, s, lanes):
        return jnp.where(fwd, ref[n - s:2 * n - s, lanes], ref[n + s:2 * n + s, lanes])

    re, im = slice(0, w), slice(w, 2 * w)
    src, dst = sa_ref, sb_ref
    for k in range(n_steps):
        s = 1 << k
        a = a_ref[0, k]
        ar, ai = a[0:1], a[1:2]
        pr = neighbours(src, s, re)
        pi = neighbours(src, s, im)
        dst[n:2 * n, re] = src[n:2 * n, re] + ar * pr - ai * pi
        dst[n:2 * n, im] = src[n:2 * n, im] + ar * pi + ai * pr
        src, dst = dst, src
    s_in = jnp.concatenate([neighbours(src, 1, re), neighbours(src, 1, im)], axis=1).astype(BF16)
    y = _dot(u, m_ref[0]) + lax.dot_general(s_in, et_ref[0], (((1,), (1,)), ((), ())),
                                            preferred_element_type=F32)
    _to_row_tiles(y, y_ref.at[0])


def _s5_chunk_core(ug, mg, fg, etg, aq):
    g = ug.shape[0]
    n_chunks = ug.shape[1] // 8
    cw = S5_CHUNK_WIDTH
    n_steps = aq.shape[1]
    assert (1 << n_steps) == n_chunks
    sw = 2 * V7X_LANES
    return pl.pallas_call(
        functools.partial(_s5_chunk_core_kernel, n_chunks=n_chunks, n_steps=n_steps),
        grid=(g,),
        in_specs=[
            pl.BlockSpec((1, 8 * n_chunks, V7X_LANES), lambda i: (i, 0, 0)),
            pl.BlockSpec((1, cw, cw), lambda i: (i, 0, 0)),
            pl.BlockSpec((1, cw, sw), lambda i: (i, 0, 0)),
            pl.BlockSpec((1, cw, sw), lambda i: (i, 0, 0)),
            pl.BlockSpec((1, n_steps, 2, V7X_LANES), lambda i: (i, 0, 0, 0)),
        ],
        out_specs=pl.BlockSpec((1, 8 * n_chunks, V7X_LANES), lambda i: (i, 0, 0)),
        out_shape=jax.ShapeDtypeStruct((g, 8 * n_chunks, V7X_LANES), F32),
        scratch_shapes=[pltpu.VMEM((3 * n_chunks, sw), F32), pltpu.VMEM((3 * n_chunks, sw), F32)],
        compiler_params=_cparams(1, 32),
        name="s5_core",
    )(ug, mg, fg, etg, aq)


S5_RELAYOUT_TOKENS = 1024
S5_RELAYOUT_PITCH = 136


def _s5_to_groups_kernel(u_ref, o_ref, xt_ref):
    w = V7X_LANES
    q = S5_CHUNK
    lo = lax.broadcasted_iota(jnp.int32, (8, w), 1) < q
    n_blk = S5_RELAYOUT_TOKENS // w
    pitch = S5_RELAYOUT_PITCH
    for sg in range(u_ref.shape[1] // w):
        for b in range(n_blk):
            xt_ref[b * pitch:b * pitch + w, :] = u_ref[b * w:(b + 1) * w, sg * w:(sg + 1) * w].T
        for gp in range(w // S5_GROUP):
            for hp in range(S5_GROUP // 2):
                a0 = xt_ref[pl.ds(gp * S5_GROUP + 2 * hp, n_blk, stride=pitch), :]
                a1 = xt_ref[pl.ds(gp * S5_GROUP + 2 * hp + 1, n_blk, stride=pitch), :]
                even = jnp.where(lo, a0, pltpu.roll(a1, q, 1))
                odd = jnp.where(lo, pltpu.roll(a0, q, 1), a1)
                g = sg * (w // S5_GROUP) + gp
                o_ref[g, pl.ds(hp, n_blk, stride=16), :] = even
                o_ref[g, pl.ds(8 + hp, n_blk, stride=16), :] = odd


def _s5_to_groups(u):
    t, ch = u.shape
    tt = S5_RELAYOUT_TOKENS
    rows = 8 * tt // S5_CHUNK
    return pl.pallas_call(
        _s5_to_groups_kernel,
        grid=(t // tt,),
        in_specs=[pl.BlockSpec((tt, ch), lambda i: (i, 0))],
        out_specs=pl.BlockSpec((S5_GROUPS, rows, V7X_LANES), lambda i: (0, i, 0)),
        out_shape=jax.ShapeDtypeStruct((S5_GROUPS, 8 * t // S5_CHUNK, V7X_LANES), F32),
        scratch_shapes=[pltpu.VMEM((tt // V7X_LANES * S5_RELAYOUT_PITCH, V7X_LANES), F32)],
        compiler_params=_cparams(1, 16),
        name="s5_to_groups",
    )(u)


def _s5_to_tokens_kernel(y_ref, o_ref, yt_ref):
    w = V7X_LANES
    q = S5_CHUNK
    lo = lax.broadcasted_iota(jnp.int32, (8, w), 1) < q
    n_blk = S5_RELAYOUT_TOKENS // w
    pitch = S5_RELAYOUT_PITCH
    for sg in range(o_ref.shape[1] // w):
        for gp in range(w // S5_GROUP):
            for hp in range(S5_GROUP // 2):
                g = sg * (w // S5_GROUP) + gp
                even = y_ref[g, pl.ds(hp, n_blk, stride=16), :]
                odd = y_ref[g, pl.ds(8 + hp, n_blk, stride=16), :]
                yt_ref[pl.ds(gp * S5_GROUP + 2 * hp, n_blk, stride=pitch), :] = jnp.where(lo, even, pltpu.roll(odd, q, 1))
                yt_ref[pl.ds(gp * S5_GROUP + 2 * hp + 1, n_blk, stride=pitch), :] = jnp.where(lo, pltpu.roll(even, q, 1), odd)
        for b in range(n_blk):
            o_ref[b * w:(b + 1) * w, sg * w:(sg + 1) * w] = yt_ref[b * pitch:b * pitch + w, :].T


def _s5_to_tokens(yg):
    g, rows_all, w = yg.shape
    tt = S5_RELAYOUT_TOKENS
    rows = 8 * tt // S5_CHUNK
    t = rows_all // 8 * S5_CHUNK
    return pl.pallas_call(
        _s5_to_tokens_kernel,
        grid=(t // tt,),
        in_specs=[pl.BlockSpec((g, rows, w), lambda i: (0, i, 0))],
        out_specs=pl.BlockSpec((tt, g * S5_GROUP), lambda i: (i, 0)),
        out_shape=jax.ShapeDtypeStruct((t, g * S5_GROUP), F32),
        scratch_shapes=[pltpu.VMEM((tt // w * S5_RELAYOUT_PITCH, w), F32)],
        compiler_params=_cparams(1, 16),
        name="s5_to_tokens",
    )(yg)


def _s5_out_kernel(x_ref, y_ref, u_ref, d_ref, w_ref, o_ref, wb_ref):
    _cast_weight_once(w_ref, wb_ref)
    y = y_ref[...] + d_ref[...] * u_ref[...]
    act = jax.nn.gelu(y).astype(BF16)
    vg = _dot(act, wb_ref[...])
    o_ref[...] = x_ref[...] + vg[:, :D_MODEL] * jax.nn.sigmoid(vg[:, D_MODEL:])


def _s5_out(x, y, u, d_skip, w_out_stack, layer, tm=512):
    t, d = x.shape
    sw = y.shape[1]
    return pl.pallas_call(
        _s5_out_kernel,
        grid=(t // tm,),
        in_specs=[
            pl.BlockSpec((tm, d), lambda i: (i, 0)),
            pl.BlockSpec((tm, sw), lambda i: (i, 0)),
            pl.BlockSpec((tm, sw), lambda i: (i, 0)),
            pl.BlockSpec((1, sw), lambda i: (0, 0)),
            _layer_weight_spec(w_out_stack, layer),
        ],
        out_specs=pl.BlockSpec((tm, d), lambda i: (i, 0)),
        out_shape=jax.ShapeDtypeStruct((t, d), F32),
        scratch_shapes=[pltpu.VMEM((sw, 2 * d), BF16)],
        compiler_params=_cparams(1, 32),
        name="s5_out",
    )(x, y, u, d_skip.reshape(1, sw).astype(F32), w_out_stack)


def _s5_layer(x, norm_g, layer, w_in_stack, lam_re, lam_im, log_dt, b_re, b_im, c_re, c_im, d_skip, w_out_stack):
    t = x.shape[0]
    n_chunks = t // S5_CHUNK
    u = _norm_matmul(x, norm_g, w_in_stack, layer, F32)
    n_steps = n_chunks.bit_length() - 1
    assert (1 << n_steps) == n_chunks
    tk, ta, cq, bq, bm, aq = _s5_param_tables(lam_re, lam_im, log_dt, b_re, b_im, c_re, c_im, n_steps)
    mg, fg, etg = _s5_ops(tk, ta, cq, bq, bm)
    yg = _s5_chunk_core(_s5_to_groups(u), mg, fg, etg, aq)
    y = _s5_to_tokens(yg)
    return _s5_out(x, y, u, d_skip, w_out_stack, layer)


def _qkv_kernel(x_ref, g_ref, w_ref, qg_ref, kg_ref, q_ref, k_ref, v_ref, wb_ref):
    _cast_weight_once(w_ref, wb_ref)
    h = _rms(x_ref[...], g_ref[...]).astype(BF16)
    qkv = _dot(h, wb_ref[...])
    tm = qkv.shape[0]
    lo = lax.broadcasted_iota(jnp.int32, (tm, V7X_LANES), 1) < HEAD_DIM
    scale = 1.0 / math.sqrt(HEAD_DIM)

    def headnorm(xt, gt):
        sq = xt * xt
        s0 = jnp.sum(jnp.where(lo, sq, 0.0), axis=-1, keepdims=True)
        s1 = jnp.sum(jnp.where(lo, 0.0, sq), axis=-1, keepdims=True)
        rs = jnp.where(lo, lax.rsqrt(s0 / HEAD_DIM + RMS_EPS), lax.rsqrt(s1 / HEAD_DIM + RMS_EPS))
        return xt * rs * gt

    for t in range(D_MODEL // V7X_LANES):
        lanes = slice(t * V7X_LANES, (t + 1) * V7X_LANES)
        q_ref[:, lanes] = (headnorm(qkv[:, lanes], qg_ref[...]) * scale).astype(BF16)
        klanes = slice(D_MODEL + t * V7X_LANES, D_MODEL + (t + 1) * V7X_LANES)
        k_ref[:, lanes] = headnorm(qkv[:, klanes], kg_ref[...]).astype(BF16)
    v_ref[...] = qkv[:, 2 * D_MODEL:].astype(BF16)


def _qkv(x, g, w_qkv_stack, layer, q_norm, k_norm, tm=512):
    t, d = x.shape
    qg = jnp.tile(q_norm.astype(F32), 2).reshape(1, V7X_LANES)
    kg = jnp.tile(k_norm.astype(F32), 2).reshape(1, V7X_LANES)
    out = jax.ShapeDtypeStruct((t, d), BF16)
    ospec = pl.BlockSpec((tm, d), lambda i: (i, 0))
    return pl.pallas_call(
        _qkv_kernel,
        grid=(t // tm,),
        in_specs=[
            pl.BlockSpec((tm, d), lambda i: (i, 0)),
            pl.BlockSpec((1, d), lambda i: (0, 0)),
            _layer_weight_spec(w_qkv_stack, layer),
            pl.BlockSpec((1, V7X_LANES), lambda i: (0, 0)),
            pl.BlockSpec((1, V7X_LANES), lambda i: (0, 0)),
        ],
        out_specs=[ospec, ospec, ospec],
        out_shape=[out, out, out],
        scratch_shapes=[pltpu.VMEM((d, 3 * d), BF16)],
        compiler_params=_cparams(1, 52),
        name="nat_qkv",
    )(x, g.reshape(1, d), w_qkv_stack, qg, kg)


def _nat_bias_table(rpb_ref, b_ref):
    w = V7X_LANES
    c = lax.broadcasted_iota(jnp.int32, (GRID_W, w), 0)
    lane = lax.broadcasted_iota(jnp.int32, (GRID_W, w), 1)
    lo = lane < GRID_W
    kc = jnp.where(lo, lane, lane - GRID_W)
    ws = jnp.clip(c - WIN_COLS // 2, 0, GRID_W - WIN_COLS)
    valid = (kc >= ws) & (kc < ws + WIN_COLS)
    n_ri = 2 * WIN_ROWS - 1
    for h in range(2):
        t_lo, t_hi = [], []
        for ri in range(n_ri):
            vb = jnp.broadcast_to(rpb_ref[h, ri:ri + 1, :], (GRID_W, w))
            t_lo.append(pltpu.roll(vb, w - (WIN_COLS - 1), 1, stride=1, stride_axis=0))
            t_hi.append(pltpu.roll(vb, GRID_W - (WIN_COLS - 1), 1, stride=1, stride_axis=0))
        for o in range(WIN_ROWS):
            for j in range(WIN_ROWS // 2):
                tile = jnp.where(lo, t_lo[o + 2 * j], t_hi[o + 2 * j + 1])
                b_ref[h, o, :, j * w:(j + 1) * w] = jnp.where(valid, tile, MASK_VALUE)


def _nat_attn_kernel(q_ref, k_ref, v_ref, rpb_ref, o_ref, b_ref, *, rows, unroll):
    lo = lax.broadcasted_iota(jnp.int32, (GRID_W, V7X_LANES), 1) < HEAD_DIM
    head_mask = (jnp.where(lo, 1.0, 0.0), jnp.where(lo, 0.0, 1.0))
    nkeys = WIN_ROWS * GRID_W
    _nat_bias_table(rpb_ref, b_ref)

    def body(rb, carry):
        chains = []
        for u in range(unroll):
            r = rb * unroll + u
            rs = jnp.clip(r - WIN_ROWS // 2, 0, rows - WIN_ROWS)
            off = rs - r + (WIN_ROWS - 1)
            q = q_ref[pl.ds(pl.multiple_of(r * GRID_W, GRID_W), GRID_W), :].astype(F32)
            k = k_ref[pl.ds(pl.multiple_of(rs * GRID_W, GRID_W), nkeys), :]
            for h in range(2):
                chains.append(dict(r=r, rs=rs, off=off, h=h, q=(q * head_mask[h]).astype(BF16), k=k))
        for c in chains:
            c["s"] = lax.dot_general(c["q"], c["k"], (((1,), (1,)), ((), ())), preferred_element_type=F32)
        for c in chains:
            c["s"] = c["s"] + b_ref[c["h"], c["off"]]
        for c in chains:
            c["m"] = jnp.max(c["s"], axis=-1, keepdims=True)
        for c in chains:
            c["p"] = jnp.exp(c["s"] - c["m"])
        for c in chains:
            c["l"] = jnp.sum(c["p"], axis=-1, keepdims=True)
        for c in chains:
            v = v_ref[pl.ds(pl.multiple_of(c["rs"] * GRID_W, GRID_W), nkeys), :]
            c["o"] = _dot(c["p"].astype(BF16), v) / c["l"]
        for u in range(unroll):
            c0, c1 = chains[2 * u], chains[2 * u + 1]
            o = jnp.where(lo, c0["o"], c1["o"])
            o_ref[pl.ds(pl.multiple_of(c0["r"] * GRID_W, GRID_W), GRID_W), :] = o.astype(BF16)
        return carry

    lax.fori_loop(0, rows // unroll, body, 0)


def _nat_attn(q, k, v, rpb):
    t, d = q.shape
    rows = t // GRID_W
    assert rows >= WIN_ROWS
    unroll = math.gcd(rows, NAT_ROW_UNROLL)
    n_ri, n_ci = rpb.shape[1], rpb.shape[2]
    rpb_pad = jnp.pad(rpb.astype(F32), ((0, 0), (0, 2 * WIN_ROWS - n_ri), (0, V7X_LANES - n_ci)))
    spec = pl.BlockSpec((t, V7X_LANES), lambda i: (0, i))
    return pl.pallas_call(
        functools.partial(_nat_attn_kernel, rows=rows, unroll=unroll),
        grid=(d // V7X_LANES,),
        in_specs=[spec, spec, spec,
                  pl.BlockSpec((2, 2 * WIN_ROWS, V7X_LANES), lambda i: (i, 0, 0))],
        out_specs=spec,
        out_shape=jax.ShapeDtypeStruct((t, d), BF16),
        scratch_shapes=[pltpu.VMEM((2, WIN_ROWS, GRID_W, WIN_ROWS * GRID_W), F32)],
        compiler_params=_cparams(1, 48),
        name="nat_attn",
    )(q, k, v, rpb_pad)


def _matmul_residual_kernel(x_ref, a_ref, w_ref, o_ref, wb_ref):
    _cast_weight_once(w_ref, wb_ref)
    o_ref[...] = x_ref[...] + _dot(a_ref[...], wb_ref[...])


def _matmul_residual(x, a, w_stack, layer, tm=512):
    t, d = x.shape
    kdim = a.shape[1]
    return pl.pallas_call(
        _matmul_residual_kernel,
        grid=(t // tm,),
        in_specs=[
            pl.BlockSpec((tm, d), lambda i: (i, 0)),
            pl.BlockSpec((tm, kdim), lambda i: (i, 0)),
            _layer_weight_spec(w_stack, layer),
        ],
        out_specs=pl.BlockSpec((tm, d), lambda i: (i, 0)),
        out_shape=jax.ShapeDtypeStruct((t, d), F32),
        scratch_shapes=[pltpu.VMEM((kdim, d), BF16)],
        compiler_params=_cparams(1, 32),
        name="matmul_residual",
    )(x, a, w_stack)


def _nat_layer(x, norm_g, layer, w_qkv_stack, q_norm, k_norm, rpb, w_o_stack):
    q, k, v = _qkv(x, norm_g, w_qkv_stack, layer, q_norm, k_norm)
    o = _nat_attn(q, k, v, rpb)
    return _matmul_residual(x, o, w_o_stack, layer)


def _router_kernel(x_ref, g_ref, w_ref, b_ref, tri_ref, ids_ref, wts_ref, cnt_ref):
    @pl.when(pl.program_id(0) == 0)
    def _():
        cnt_ref[...] = jnp.zeros(cnt_ref.shape, F32)

    def split(a):
        hi = a.astype(BF16)
        return hi, (a - hi.astype(F32)).astype(BF16)

    def dot_nt(a, b):
        return lax.dot_general(a, b, (((1,), (1,)), ((), ())), preferred_element_type=F32)

    h_hi, h_lo = split(_rms(x_ref[...], g_ref[...]))
    w_hi, w_lo = split(w_ref[...])
    lg = dot_nt(w_hi, h_hi) + (dot_nt(w_hi, h_lo) + dot_nt(w_lo, h_hi)) + b_ref[...]
    row = lax.broadcasted_iota(jnp.int32, lg.shape, 0)
    big = 4 * V7X_LANES
    is_g = (row >= N_EXPERTS) & (row < N_EXPERTS + N_GROUPS)
    gmax = jnp.max(jnp.where(is_g, lg, MASK_VALUE), axis=0, keepdims=True)
    gsum = jnp.sum(jnp.where(is_g, jnp.exp(lg - gmax), 0.0), axis=0, keepdims=True)
    g_val = 1.0 / gsum
    g_idx = jnp.min(jnp.where(is_g & (lg == gmax), row, big), axis=0, keepdims=True) - N_EXPERTS
    in_grp = (row < N_EXPERTS) & ((row // EXPERTS_PER_GROUP) == g_idx)
    m1 = jnp.max(jnp.where(in_grp, lg, MASK_VALUE), axis=0, keepdims=True)
    i1 = jnp.min(jnp.where(in_grp & (lg == m1), row, big), axis=0, keepdims=True)
    rest = in_grp & (row != i1)
    m2 = jnp.max(jnp.where(rest, lg, MASK_VALUE), axis=0, keepdims=True)
    i2 = jnp.min(jnp.where(rest & (lg == m2), row, big), axis=0, keepdims=True)
    z = jnp.sum(jnp.where(in_grp, jnp.exp(lg - m1), 0.0), axis=0, keepdims=True)
    p1 = 1.0 / z
    p2 = jnp.exp(m2 - m1) / z
    den = p1 + p2
    w1 = g_val * (p1 / den)
    w2 = g_val * (p2 / den)

    sel1 = (row == i1)[:N_EXPERTS]
    sel2 = (row == i2)[:N_EXPERTS]
    onehot = jnp.where(sel1 | sel2, 1.0, 0.0)
    before = _dot(onehot.astype(BF16), tri_ref[...]) + cnt_ref[:, 0:1]
    r1 = jnp.sum(jnp.where(sel1, before, 0.0), axis=0, keepdims=True)
    r2 = jnp.sum(jnp.where(sel2, before, 0.0), axis=0, keepdims=True)
    cnt_ref[...] = cnt_ref[...] + jnp.sum(onehot, axis=1, keepdims=True)

    orow = lax.broadcasted_iota(jnp.int32, ids_ref.shape, 0)
    ids_ref[...] = jnp.where(orow == 0, i1, jnp.where(orow == 1, i2, jnp.where(
        orow == 2, r1.astype(jnp.int32), jnp.where(orow == 3, r2.astype(jnp.int32), 0))))
    wts_ref[...] = jnp.where(orow == 0, w1, jnp.where(orow == 1, w2, 0.0))


def _router(x, g, w_group, b_group, w_expert, b_expert, tm=512):
    t, d = x.shape
    rows = MOE_ROUTER_ROWS
    pad = rows - N_EXPERTS - N_GROUPS
    w = jnp.pad(jnp.concatenate([w_expert, w_group], axis=1).astype(F32).T, ((0, pad), (0, 0)))
    b = jnp.pad(jnp.concatenate([b_expert, b_group]).astype(F32), (0, pad)).reshape(rows, 1)
    tri = (jnp.arange(tm)[:, None] < jnp.arange(tm)[None, :]).astype(BF16)
    return pl.pallas_call(
        _router_kernel,
        grid=(t // tm,),
        in_specs=[
            pl.BlockSpec((tm, d), lambda i: (i, 0)),
            pl.BlockSpec((1, d), lambda i: (0, 0)),
            pl.BlockSpec((rows, d), lambda i: (0, 0)),
            pl.BlockSpec((rows, 1), lambda i: (0, 0)),
            pl.BlockSpec((tm, tm), lambda i: (0, 0)),
        ],
        out_specs=[
            pl.BlockSpec((8, tm), lambda i: (0, i)),
            pl.BlockSpec((8, tm), lambda i: (0, i)),
            pl.BlockSpec((N_EXPERTS, V7X_LANES), lambda i: (0, 0)),
        ],
        out_shape=[
            jax.ShapeDtypeStruct((8, t), jnp.int32),
            jax.ShapeDtypeStruct((8, t), F32),
            jax.ShapeDtypeStruct((N_EXPERTS, V7X_LANES), F32),
        ],
        compiler_params=_cparams(1, 32),
        name="moe_router",
    )(x, g.reshape(1, d), w, b, tri)


def _moe_tables(cnt, t):
    tm = MOE_ROW_TILE
    n_rows = 2 * t
    counts = cnt[:, 0].astype(jnp.int32)
    ends = jnp.cumsum(counts)
    starts = ends - counts
    brk = jnp.concatenate([jnp.arange(n_rows // tm, dtype=jnp.int32) * tm, starts])
    idx = jnp.arange(brk.shape[0])
    before = (brk[None, :] < brk[:, None]) | ((brk[None, :] == brk[:, None]) & (idx[None, :] < idx[:, None]))
    rank = jnp.sum(before.astype(jnp.int32), axis=1)
    lo = jnp.sum(jnp.where(rank[None, :] == idx[:, None], brk[None, :], 0), axis=1)
    hi = jnp.concatenate([lo[1:], jnp.full((1,), n_rows, jnp.int32)])
    anchor = jnp.minimum(lo, n_rows - 1)
    tile = anchor // tm
    expert = jnp.minimum(jnp.sum((ends[None, :] <= anchor[:, None]).astype(jnp.int32), axis=1), N_EXPERTS - 1)
    lo_in = lo - tile * tm
    hi_in = hi - tile * tm
    first = ((hi > lo) & (lo_in == 0)).astype(jnp.int32)
    last = ((hi > lo) & (hi_in == tm)).astype(jnp.int32)
    newexp = jnp.concatenate([jnp.ones((1,), jnp.int32), (expert[1:] != expert[:-1]).astype(jnp.int32)])
    table = jnp.stack([tile, expert, lo_in, hi_in, first, last, newexp]).astype(jnp.int32)
    return starts, table


def _moe_pos_kernel(starts_ref, ids_ref, pos_ref):
    e = ids_ref[0:2, :]
    acc = jnp.zeros(e.shape, jnp.int32)
    for ex in range(N_EXPERTS):
        acc = jnp.where(e == ex, starts_ref[ex], acc)
    pos_ref[0:2, :] = acc + ids_ref[2:4, :]
    pos_ref[2:8, :] = jnp.zeros((6, e.shape[1]), jnp.int32)


def _moe_positions(ids, starts, tm=2048):
    t = ids.shape[1]
    tm = min(tm, t)
    grid_spec = pltpu.PrefetchScalarGridSpec(
        num_scalar_prefetch=1,
        grid=(t // tm,),
        in_specs=[pl.BlockSpec((8, tm), lambda i, st: (0, i))],
        out_specs=pl.BlockSpec((8, tm), lambda i, st: (0, i)),
    )
    return pl.pallas_call(
        _moe_pos_kernel,
        grid_spec=grid_spec,
        out_shape=jax.ShapeDtypeStruct((8, t), jnp.int32),
        compiler_params=_cparams(1, 16),
        name="moe_positions",
    )(starts, ids)


def _tile_positions(pos, tm):
    return pos.reshape(2, -1, tm).transpose(1, 0, 2)


def _to_row_tiles(x, dst_ref):
    rows = x.shape[0]
    for c in range(x.shape[1] // V7X_LANES):
        dst_ref[pl.ds(c, rows, stride=8), :] = x[:, c * V7X_LANES:(c + 1) * V7X_LANES]


def _from_row_tiles(src_ref, rows, base=0):
    return jnp.concatenate([src_ref[pl.ds(base + c, rows, stride=8), :] for c in range(8)], axis=1)


def _row_tile(ref, r):
    return ref.at[pl.ds(pl.multiple_of(r * 8, 8), 8), :]


def _moe_dispatch_kernel(pos_ref, x_ref, g_ref, xs_ref, xt_ref, sem):
    tm = x_ref.shape[0]
    _to_row_tiles(_rms(x_ref[...], g_ref[...]), xt_ref)
    for j in range(tm):
        for k in range(2):
            pltpu.make_async_copy(_row_tile(xt_ref, j), _row_tile(xs_ref, pos_ref[0, k, j]), sem).start(priority=k)
    for k in range(2):
        pltpu.make_async_copy(xt_ref, xs_ref.at[pl.ds(0, 8 * tm), :], sem).wait()


def _moe_dispatch(x, g, pos, tm=MOE_TOKEN_TILE):
    t, d = x.shape
    assert d == 8 * V7X_LANES
    return pl.pallas_call(
        _moe_dispatch_kernel,
        grid=(t // tm,),
        in_specs=[
            pl.BlockSpec((1, 2, tm), lambda i: (i, 0, 0), memory_space=pltpu.SMEM),
            pl.BlockSpec((tm, d), lambda i: (i, 0)),
            pl.BlockSpec((1, d), lambda i: (0, 0)),
        ],
        out_specs=pl.BlockSpec(memory_space=pl.ANY),
        out_shape=jax.ShapeDtypeStruct((2 * t * 8, V7X_LANES), F32),
        scratch_shapes=[pltpu.VMEM((8 * tm, V7X_LANES), F32), pltpu.SemaphoreType.DMA(())],
        compiler_params=_cparams(1, 32),
        name="moe_dispatch",
    )(_tile_positions(pos, tm), x, g.reshape(1, d))


def _moe_ffn_kernel(tab_ref, xs_ref, wg_ref, wu_ref, wd_ref, ys_ref, wgb_ref, wub_ref, wdb_ref, acc_ref):
    i = pl.program_id(0)
    lo = tab_ref[2, i]
    hi = tab_ref[3, i]
    first = tab_ref[4, i] == 1
    last = tab_ref[5, i] == 1
    tm = acc_ref.shape[0]

    @pl.when(tab_ref[6, i] == 1)
    def _():
        wgb_ref[...] = wg_ref[0, 0, 0].astype(BF16)
        wub_ref[...] = wu_ref[0, 0, 0].astype(BF16)
        wdb_ref[...] = wd_ref[0, 0, 0].astype(BF16)

    @pl.when(hi > lo)
    def _():
        h = _from_row_tiles(xs_ref, tm).astype(BF16)
        a = _dot(h, wgb_ref[...])
        u = _dot(h, wub_ref[...])
        rowid = lax.broadcasted_iota(jnp.int32, a.shape, 0)
        act = jnp.where((rowid >= lo) & (rowid < hi), jax.nn.silu(a) * u, 0.0).astype(BF16)
        y = _dot(act, wdb_ref[...])

        @pl.when(first & last)
        def _():
            _to_row_tiles(y, ys_ref)

        @pl.when(first & jnp.logical_not(last))
        def _():
            acc_ref[...] = y

        @pl.when(jnp.logical_not(first) & jnp.logical_not(last))
        def _():
            acc_ref[...] += y

        @pl.when(jnp.logical_not(first) & last)
        def _():
            _to_row_tiles(acc_ref[...] + y, ys_ref)


def _moe_ffn(xs, table, w_gate, w_up, w_down, layer):
    d, ff = w_gate.shape[-2:]
    tm = MOE_ROW_TILE
    n_items = table.shape[1]
    epg = EXPERTS_PER_GROUP
    wmap = lambda i, tab: (layer, tab[1, i] // epg, tab[1, i] % epg, 0, 0)
    grid_spec = pltpu.PrefetchScalarGridSpec(
        num_scalar_prefetch=1,
        grid=(n_items,),
        in_specs=[
            pl.BlockSpec((8 * tm, V7X_LANES), lambda i, tab: (tab[0, i], 0)),
            pl.BlockSpec((1, 1, 1, d, ff), wmap),
            pl.BlockSpec((1, 1, 1, d, ff), wmap),
            pl.BlockSpec((1, 1, 1, ff, d), wmap),
        ],
        out_specs=pl.BlockSpec((8 * tm, V7X_LANES), lambda i, tab: (tab[0, i], 0)),
        scratch_shapes=[pltpu.VMEM((d, ff), BF16), pltpu.VMEM((d, ff), BF16), pltpu.VMEM((ff, d), BF16),
                        pltpu.VMEM((tm, d), F32)],
    )
    return pl.pallas_call(
        _moe_ffn_kernel,
        grid_spec=grid_spec,
        out_shape=jax.ShapeDtypeStruct(xs.shape, F32),
        compiler_params=_cparams(1, 32),
        name="moe_ffn",
    )(table, xs, w_gate, w_up, w_down)


def _combine_ple_kernel(pos_ref, x_ref, wc_ref, ys_ref, g_ref, p_ref, wg_ref, wp_ref, o_ref,
                        ya_ref, yb_ref, wgb_ref, wpb_ref, sems):
    tm = x_ref.shape[0]
    hm = tm // 2
    for j in range(tm):
        sem = sems.at[j // hm]
        pltpu.make_async_copy(_row_tile(ys_ref, pos_ref[0, 0, j]), _row_tile(ya_ref, j), sem).start(priority=0)
        pltpu.make_async_copy(_row_tile(ys_ref, pos_ref[0, 1, j]), _row_tile(yb_ref, j), sem).start(priority=1)
    _cast_weight_once(wg_ref, wgb_ref)
    _cast_weight_once(wp_ref, wpb_ref)
    for half in range(2):
        rows = pl.ds(half * hm, hm)
        tiles = pl.ds(half * 8 * hm, 8 * hm)
        proj = _dot(p_ref[0, 0, rows, :].astype(BF16), wpb_ref[...])
        pltpu.make_async_copy(ys_ref.at[pl.ds(0, 8 * hm), :], ya_ref.at[tiles, :], sems.at[half]).wait()
        pltpu.make_async_copy(ys_ref.at[pl.ds(0, 8 * hm), :], yb_ref.at[tiles, :], sems.at[half]).wait()
        w = wc_ref[rows, :]
        x2 = (x_ref[rows, :] + w[:, 0:1] * _from_row_tiles(ya_ref, hm, half * 8 * hm)
              + w[:, 1:2] * _from_row_tiles(yb_ref, hm, half * 8 * hm))
        h = _rms(x2, g_ref[...]).astype(BF16)
        gate = jax.nn.sigmoid(_dot(h, wgb_ref[...]))
        o_ref[rows, :] = x2 + gate * proj


def _combine_ple(x, pos, wts, ys, norm_g, p_stack, w_proj_stack, w_gate_stack, layer, tm=MOE_TOKEN_TILE):
    t, d = x.shape
    pd = p_stack.shape[-1]
    wc = wts[0:2].T
    return pl.pallas_call(
        _combine_ple_kernel,
        grid=(t // tm,),
        in_specs=[
            pl.BlockSpec((1, 2, tm), lambda i: (i, 0, 0), memory_space=pltpu.SMEM),
            pl.BlockSpec((tm, d), lambda i: (i, 0)),
            pl.BlockSpec((tm, 2), lambda i: (i, 0)),
            pl.BlockSpec(memory_space=pl.ANY),
            pl.BlockSpec((1, d), lambda i: (0, 0)),
            pl.BlockSpec((1, 1, tm, pd), lambda i: (layer, 0, i, 0)),
            _layer_weight_spec(w_gate_stack, layer),
            _layer_weight_spec(w_proj_stack, layer),
        ],
        out_specs=pl.BlockSpec((tm, d), lambda i: (i, 0)),
        out_shape=jax.ShapeDtypeStruct((t, d), F32),
        scratch_shapes=[pltpu.VMEM((8 * tm, V7X_LANES), F32), pltpu.VMEM((8 * tm, V7X_LANES), F32),
                        pltpu.VMEM((d, d), BF16), pltpu.VMEM((pd, d), BF16), pltpu.SemaphoreType.DMA((2,))],
        compiler_params=_cparams(1, 40),
        name="moe_combine_ple",
    )(_tile_positions(pos, tm), x, wc, ys, norm_g.reshape(1, d), p_stack, w_gate_stack, w_proj_stack)


def _moe_ple_layer(x, layer, norm_ffn, w_group, b_group, w_expert, b_expert, w_gate, w_up, w_down,
                   norm_ple, p_stack, ple_w_proj, ple_w_gate):
    t = x.shape[0]
    ids, wts, cnt = _router(x, norm_ffn, w_group, b_group, w_expert, b_expert)
    starts, table = _moe_tables(cnt, t)
    pos = _moe_positions(ids, starts)[0:2]
    xs = _moe_dispatch(x, norm_ffn, pos)
    ys = _moe_ffn(xs, table, w_gate, w_up, w_down, layer)
    return _combine_ple(x, pos, wts, ys, norm_ple, p_stack, ple_w_proj, ple_w_gate, layer)


def kernel(x, p, norm_mix, norm_ffn, norm_ple, s5_w_in, s5_lam_re, s5_lam_im, s5_log_dt, s5_b_re, s5_b_im, s5_c_re, s5_c_im, s5_d, s5_w_out, nat_w_qkv, nat_q_norm, nat_k_norm, nat_rpb, nat_w_o, moe_w_group, moe_b_group, moe_w_expert, moe_b_expert, moe_w_gate, moe_w_up, moe_w_down, ple_w_proj, ple_w_gate):
    bsz, seq, d = x.shape
    depth = p.shape[0]
    assert bsz == 1 and d == D_MODEL
    xs = x.reshape(seq, d).astype(F32)
    for i in range(depth):
        j = i // 2
        if i % 2 == 0:
            xs = _s5_layer(xs, norm_mix[i], j, s5_w_in, s5_lam_re[j], s5_lam_im[j], s5_log_dt[j],
                           s5_b_re[j], s5_b_im[j], s5_c_re[j], s5_c_im[j], s5_d[j], s5_w_out)
        else:
            xs = _nat_layer(xs, norm_mix[i], j, nat_w_qkv, nat_q_norm[j], nat_k_norm[j], nat_rpb[j], nat_w_o)
        xs = _moe_ple_layer(xs, i, norm_ffn[i], moe_w_group[i], moe_b_group[i], moe_w_expert[i], moe_b_expert[i],
                            moe_w_gate, moe_w_up, moe_w_down, norm_ple[i], p, ple_w_proj, ple_w_gate)
    return xs.reshape(bsz, seq, d).astype(x.dtype)
```

```python
import functools
import math

import jax
import jax.numpy as jnp
from jax import lax
from jax.experimental import pallas as pl
from jax.experimental.pallas import tpu as pltpu

F32 = jnp.float32
BF16 = jnp.bfloat16

D_MODEL = 1024
GRID_W = 64
S5_GROUP = 16
S5_STATE = 64
S5_GROUPS = 32
HEAD_DIM = 64
WIN_ROWS = 8
WIN_COLS = 16
N_GROUPS = 4
EXPERTS_PER_GROUP = 8
N_EXPERTS = N_GROUPS * EXPERTS_PER_GROUP
RMS_EPS = 1e-6
MASK_VALUE = -1e30

V7X_LANES = 128
V7X_SUBLANES = 8
V7X_VMEM_BYTES = 64 * 1024 * 1024

S5_CHUNK = 64
S5_CHUNK_WIDTH = S5_CHUNK * S5_GROUP

NAT_ROW_UNROLL = 32

MOE_ROUTER_ROWS = 40
MOE_ROW_TILE = 512
MOE_TOKEN_TILE = 512


def _cparams(n_axes, vmem_mib):
    return pltpu.CompilerParams(
        dimension_semantics=("arbitrary",) * n_axes,
        vmem_limit_bytes=min(vmem_mib * 1024 * 1024, V7X_VMEM_BYTES - 4 * 1024 * 1024),
    )


def _dot(a, b):
    return jnp.dot(a, b, preferred_element_type=F32)


def _rms(x, g):
    ms = jnp.mean(x * x, axis=-1, keepdims=True)
    return x * lax.rsqrt(ms + RMS_EPS) * g


def _layer_weight_spec(w_stack, layer):
    _, k, n = w_stack.shape
    return pl.BlockSpec((1, k, n), lambda i: (layer, 0, 0), pipeline_mode=pl.Buffered(1))


def _cast_weight_once(w_ref, wb_ref):
    @pl.when(pl.program_id(0) == 0)
    def _():
        wb_ref[...] = w_ref[0].astype(BF16)


def _norm_matmul_kernel(x_ref, g_ref, w_ref, o_ref, wb_ref):
    _cast_weight_once(w_ref, wb_ref)
    h = _rms(x_ref[...], g_ref[...]).astype(BF16)
    o_ref[...] = _dot(h, wb_ref[...]).astype(o_ref.dtype)


def _norm_matmul(x, g, w_stack, layer, out_dtype, tm=512):
    t, d = x.shape
    n = w_stack.shape[2]
    return pl.pallas_call(
        _norm_matmul_kernel,
        grid=(t // tm,),
        in_specs=[
            pl.BlockSpec((tm, d), lambda i: (i, 0)),
            pl.BlockSpec((1, d), lambda i: (0, 0)),
            _layer_weight_spec(w_stack, layer),
        ],
        out_specs=pl.BlockSpec((tm, n), lambda i: (i, 0)),
        out_shape=jax.ShapeDtypeStruct((t, n), out_dtype),
        scratch_shapes=[pltpu.VMEM((d, n), BF16)],
        compiler_params=_cparams(1, 32),
        name="norm_matmul",
    )(x, g.reshape(1, d), w_stack)


def _s5_param_tables(lam_re, lam_im, log_dt, b_re, b_im, c_re, c_im, n_steps):
    q = S5_CHUNK
    g, p = S5_GROUPS, S5_STATE
    dt = jnp.exp(log_dt.astype(F32))[..., None]
    lam_re = lam_re.astype(F32)
    lam_im = lam_im.astype(F32)
    zr = lam_re * dt
    zi = lam_im * dt
    k = jnp.arange(q + 1, dtype=F32)[:, None]
    mag = jnp.exp(zr[:, :, None, :] * k)
    ang = zi[:, :, None, :] * k
    tr = mag * jnp.cos(ang)
    ti = mag * jnp.sin(ang)
    nr = tr[:, :, 1] - 1.0
    ni = ti[:, :, 1]
    den = lam_re * lam_re + lam_im * lam_im
    cr = (nr * lam_re + ni * lam_im) / den
    ci = (ni * lam_re - nr * lam_im) / den
    b_re = b_re.astype(F32)
    b_im = b_im.astype(F32)
    bbr = cr[..., None] * b_re - ci[..., None] * b_im
    bbi = cr[..., None] * b_im + ci[..., None] * b_re

    def both(fwd, rev):
        return jnp.concatenate([fwd, rev], axis=-1)

    def lag_rows(x):
        fwd = jnp.pad(x[0][:, :q], ((0, 0), (q - 1, 1), (0, 0)))
        rev = jnp.pad(x[1][:, :q][:, ::-1], ((0, 0), (0, q), (0, 0)))
        return both(fwd, rev)

    def f_rows(x):
        return both(x[0][:, :q][:, ::-1], x[1][:, :q])

    def e_rows(x):
        return both(x[0][:, 1:q + 1], x[1][:, 1:q + 1][:, ::-1])

    tk = jnp.stack([lag_rows(tr), lag_rows(ti)], axis=1)
    ta = jnp.stack([f_rows(tr), f_rows(ti), e_rows(tr), e_rows(ti)], axis=1)
    cq = jnp.stack([both(c_re[0], c_re[1]), both(c_im[0], c_im[1])], axis=1).astype(F32)
    bbr_t = jnp.swapaxes(bbr, 2, 3)
    bbi_t = jnp.swapaxes(bbi, 2, 3)
    bq = jnp.stack([both(bbr_t[0], bbr_t[1]), both(bbi_t[0], bbi_t[1])], axis=1)
    bm = jnp.concatenate([bq[:, 0], bq[:, 1]], axis=-1)

    steps = (q * 2.0 ** jnp.arange(n_steps, dtype=F32))[:, None, None, None]
    smag = jnp.exp(zr[None] * steps)
    sang = zi[None] * steps
    aq = jnp.stack([both(smag[:, 0] * jnp.cos(sang[:, 0]), smag[:, 1] * jnp.cos(sang[:, 1])),
                    both(smag[:, 0] * jnp.sin(sang[:, 0]), smag[:, 1] * jnp.sin(sang[:, 1]))], axis=2)
    return tk, ta, cq, bq, bm, aq.transpose(1, 0, 2, 3)


def _s5_ops_kernel(tk_ref, ta_ref, cq_ref, bq_ref, bm_ref, m_ref, f_ref, et_ref):
    w = V7X_LANES
    hh = S5_GROUP
    q = S5_CHUNK

    def outer(t, c):
        return (c[:, None, :] * t[None, :, :]).reshape(hh * t.shape[0], w)

    cre, cim = cq_ref[0, 0], cq_ref[0, 1]
    bre, bim = bq_ref[0, 0], bq_ref[0, 1]
    tre, tim = tk_ref[0, 0], tk_ref[0, 1]
    w_re = outer(tre, cre) - outer(tim, cim)
    w_im = outer(tre, cim) + outer(tim, cre)
    clt = jnp.concatenate([w_re, -w_im], axis=1)
    bk = lax.dot_general(bm_ref[0], clt, (((1,), (1,)), ((), ())), precision=lax.Precision.HIGHEST,
                         preferred_element_type=F32)
    lo = lax.broadcasted_iota(jnp.int32, (q, w), 1) < q
    for hi in range(hh):
        for j in range(hh // 2):
            ka = jnp.broadcast_to(bk[hi:hi + 1, (2 * j) * w:(2 * j + 1) * w], (q, w))
            kb = jnp.broadcast_to(bk[hi:hi + 1, (2 * j + 1) * w:(2 * j + 2) * w], (q, w))
            tile = jnp.where(lo, pltpu.roll(ka, w - (q - 1), 1, stride=1, stride_axis=0),
                             pltpu.roll(kb, 1, 1, stride=1, stride_axis=0))
            m_ref[0, hi * q:(hi + 1) * q, j * w:(j + 1) * w] = tile.astype(BF16)

    fr, fi, er, ei = ta_ref[0, 0], ta_ref[0, 1], ta_ref[0, 2], ta_ref[0, 3]
    f_ref[0, :, 0:w] = (outer(fr, bre) - outer(fi, bim)).astype(BF16)
    f_ref[0, :, w:2 * w] = (outer(fr, bim) + outer(fi, bre)).astype(BF16)
    et_ref[0, :, 0:w] = (outer(er, cre) - outer(ei, cim)).astype(BF16)
    et_ref[0, :, w:2 * w] = (-(outer(er, cim) + outer(ei, cre))).astype(BF16)


def _s5_ops(tk, ta, cq, bq, bm):
    g = tk.shape[0]
    p2 = 2 * S5_STATE
    cw = S5_CHUNK_WIDTH

    def spec(a):
        return pl.BlockSpec((1,) + a.shape[1:], lambda i: (i,) + (0,) * (a.ndim - 1))

    return pl.pallas_call(
        _s5_ops_kernel,
        grid=(g,),
        in_specs=[spec(tk), spec(ta), spec(cq), spec(bq), spec(bm)],
        out_specs=[
            pl.BlockSpec((1, cw, cw), lambda i: (i, 0, 0)),
            pl.BlockSpec((1, cw, 2 * p2), lambda i: (i, 0, 0)),
            pl.BlockSpec((1, cw, 2 * p2), lambda i: (i, 0, 0)),
        ],
        out_shape=[
            jax.ShapeDtypeStruct((g, cw, cw), BF16),
            jax.ShapeDtypeStruct((g, cw, 2 * p2), BF16),
            jax.ShapeDtypeStruct((g, cw, 2 * p2), BF16),
        ],
        compiler_params=_cparams(1, 32),
        name="s5_ops",
    )(tk, ta, cq, bq, bm)


def _s5_chunk_core_kernel(u_ref, m_ref, f_ref, et_ref, a_ref, y_ref, sa_ref, sb_ref, *, n_chunks, n_steps):
    w = V7X_LANES
    n = n_chunks

    @pl.when(pl.program_id(0) == 0)
    def _():
        sa_ref[...] = jnp.zeros(sa_ref.shape, F32)
        sb_ref[...] = jnp.zeros(sb_ref.shape, F32)

    fwd = lax.broadcasted_iota(jnp.int32, (n, w), 1) < S5_STATE
    u = _from_row_tiles(u_ref.at[0], n).astype(BF16)
    sa_ref[n:2 * n, :] = _dot(u, f_ref[0])

    def neighbours(ref, s, lanes):
        return jnp.where(fwd, ref[n - s:2 * n - s, lanes], ref[n + s:2 * n + s, lanes])

    re, im = slice(0, w), slice(w, 2 * w)
    src, dst = sa_ref, sb_ref
    for k in range(n_steps):
        s = 1 << k
        a = a_ref[0, k]
        ar, ai = a[0:1], a[1:2]
        pr = neighbours(src, s, re)
        pi = neighbours(src, s, im)
        dst[n:2 * n, re] = src[n:2 * n, re] + ar * pr - ai * pi
        dst[n:2 * n, im] = src[n:2 * n, im] + ar * pi + ai * pr
        src, dst = dst, src
    s_in = jnp.concatenate([neighbours(src, 1, re), neighbours(src, 1, im)], axis=1).astype(BF16)
    y = _dot(u, m_ref[0]) + lax.dot_general(s_in, et_ref[0], (((1,), (1,)), ((), ())),
                                            preferred_element_type=F32)
    _to_row_tiles(y, y_ref.at[0])


def _s5_chunk_core(ug, mg, fg, etg, aq):
    g = ug.shape[0]
    n_chunks = ug.shape[1] // V7X_SUBLANES
    cw = S5_CHUNK_WIDTH
    n_steps = aq.shape[1]
    assert (1 << n_steps) == n_chunks
    sw = 2 * V7X_LANES
    return pl.pallas_call(
        functools.partial(_s5_chunk_core_kernel, n_chunks=n_chunks, n_steps=n_steps),
        grid=(g,),
        in_specs=[
            pl.BlockSpec((1, V7X_SUBLANES * n_chunks, V7X_LANES), lambda i: (i, 0, 0)),
            pl.BlockSpec((1, cw, cw), lambda i: (i, 0, 0)),
            pl.BlockSpec((1, cw, sw), lambda i: (i, 0, 0)),
            pl.BlockSpec((1, cw, sw), lambda i: (i, 0, 0)),
            pl.BlockSpec((1, n_steps, 2, V7X_LANES), lambda i: (i, 0, 0, 0)),
        ],
        out_specs=pl.BlockSpec((1, V7X_SUBLANES * n_chunks, V7X_LANES), lambda i: (i, 0, 0)),
        out_shape=jax.ShapeDtypeStruct((g, V7X_SUBLANES * n_chunks, V7X_LANES), F32),
        scratch_shapes=[pltpu.VMEM((3 * n_chunks, sw), F32), pltpu.VMEM((3 * n_chunks, sw), F32)],
        compiler_params=_cparams(1, 32),
        name="s5_core",
    )(ug, mg, fg, etg, aq)


S5_RELAYOUT_TOKENS = 1024
S5_RELAYOUT_PITCH = 136


def _s5_to_groups_kernel(u_ref, o_ref, xt_ref):
    w = V7X_LANES
    q = S5_CHUNK
    lo = lax.broadcasted_iota(jnp.int32, (8, w), 1) < q
    n_blk = S5_RELAYOUT_TOKENS // w
    pitch = S5_RELAYOUT_PITCH
    for sg in range(u_ref.shape[1] // w):
        for b in range(n_blk):
            xt_ref[b * pitch:b * pitch + w, :] = u_ref[b * w:(b + 1) * w, sg * w:(sg + 1) * w].T
        for gp in range(w // S5_GROUP):
            for hp in range(S5_GROUP // 2):
                a0 = xt_ref[pl.ds(gp * S5_GROUP + 2 * hp, n_blk, stride=pitch), :]
                a1 = xt_ref[pl.ds(gp * S5_GROUP + 2 * hp + 1, n_blk, stride=pitch), :]
                even = jnp.where(lo, a0, pltpu.roll(a1, q, 1))
                odd = jnp.where(lo, pltpu.roll(a0, q, 1), a1)
                g = sg * (w // S5_GROUP) + gp
                o_ref[g, pl.ds(hp, n_blk, stride=2 * V7X_SUBLANES), :] = even
                o_ref[g, pl.ds(V7X_SUBLANES + hp, n_blk, stride=2 * V7X_SUBLANES), :] = odd


def _s5_to_groups(u):
    t, ch = u.shape
    tt = S5_RELAYOUT_TOKENS
    rows = V7X_SUBLANES * tt // S5_CHUNK
    return pl.pallas_call(
        _s5_to_groups_kernel,
        grid=(t // tt,),
        in_specs=[pl.BlockSpec((tt, ch), lambda i: (i, 0))],
        out_specs=pl.BlockSpec((S5_GROUPS, rows, V7X_LANES), lambda i: (0, i, 0)),
        out_shape=jax.ShapeDtypeStruct((S5_GROUPS, V7X_SUBLANES * t // S5_CHUNK, V7X_LANES), F32),
        scratch_shapes=[pltpu.VMEM((tt // V7X_LANES * S5_RELAYOUT_PITCH, V7X_LANES), F32)],
        compiler_params=_cparams(1, 16),
        name="s5_to_groups",
    )(u)


def _s5_to_tokens_kernel(y_ref, o_ref, yt_ref):
    w = V7X_LANES
    q = S5_CHUNK
    lo = lax.broadcasted_iota(jnp.int32, (8, w), 1) < q
    n_blk = S5_RELAYOUT_TOKENS // w
    pitch = S5_RELAYOUT_PITCH
    for sg in range(o_ref.shape[1] // w):
        for gp in range(w // S5_GROUP):
            for hp in range(S5_GROUP // 2):
                g = sg * (w // S5_GROUP) + gp
                even = y_ref[g, pl.ds(hp, n_blk, stride=2 * V7X_SUBLANES), :]
                odd = y_ref[g, pl.ds(V7X_SUBLANES + hp, n_blk, stride=2 * V7X_SUBLANES), :]
                yt_ref[pl.ds(gp * S5_GROUP + 2 * hp, n_blk, stride=pitch), :] = jnp.where(lo, even, pltpu.roll(odd, q, 1))
                yt_ref[pl.ds(gp * S5_GROUP + 2 * hp + 1, n_blk, stride=pitch), :] = jnp.where(lo, pltpu.roll(even, q, 1), odd)
        for b in range(n_blk):
            o_ref[b * w:(b + 1) * w, sg * w:(sg + 1) * w] = yt_ref[b * pitch:b * pitch + w, :].T


def _s5_to_tokens(yg):
    g, rows_all, w = yg.shape
    tt = S5_RELAYOUT_TOKENS
    rows = V7X_SUBLANES * tt // S5_CHUNK
    t = rows_all // V7X_SUBLANES * S5_CHUNK
    return pl.pallas_call(
        _s5_to_tokens_kernel,
        grid=(t // tt,),
        in_specs=[pl.BlockSpec((g, rows, w), lambda i: (0, i, 0))],
        out_specs=pl.BlockSpec((tt, g * S5_GROUP), lambda i: (i, 0)),
        out_shape=jax.ShapeDtypeStruct((t, g * S5_GROUP), F32),
        scratch_shapes=[pltpu.VMEM((tt // w * S5_RELAYOUT_PITCH, w), F32)],
        compiler_params=_cparams(1, 16),
        name="s5_to_tokens",
    )(yg)


def _s5_out_kernel(x_ref, y_ref, u_ref, d_ref, w_ref, o_ref, wb_ref):
    _cast_weight_once(w_ref, wb_ref)
    y = y_ref[...] + d_ref[...] * u_ref[...]
    act = jax.nn.gelu(y).astype(BF16)
    vg = _dot(act, wb_ref[...])
    o_ref[...] = x_ref[...] + vg[:, :D_MODEL] * jax.nn.sigmoid(vg[:, D_MODEL:])


def _s5_out(x, y, u, d_skip, w_out_stack, layer, tm=512):
    t, d = x.shape
    sw = y.shape[1]
    return pl.pallas_call(
        _s5_out_kernel,
        grid=(t // tm,),
        in_specs=[
            pl.BlockSpec((tm, d), lambda i: (i, 0)),
            pl.BlockSpec((tm, sw), lambda i: (i, 0)),
            pl.BlockSpec((tm, sw), lambda i: (i, 0)),
            pl.BlockSpec((1, sw), lambda i: (0, 0)),
            _layer_weight_spec(w_out_stack, layer),
        ],
        out_specs=pl.BlockSpec((tm, d), lambda i: (i, 0)),
        out_shape=jax.ShapeDtypeStruct((t, d), F32),
        scratch_shapes=[pltpu.VMEM((sw, 2 * d), BF16)],
        compiler_params=_cparams(1, 32),
        name="s5_out",
    )(x, y, u, d_skip.reshape(1, sw).astype(F32), w_out_stack)


def _s5_layer(x, norm_g, layer, w_in_stack, lam_re, lam_im, log_dt, b_re, b_im, c_re, c_im, d_skip, w_out_stack):
    t = x.shape[0]
    n_chunks = t // S5_CHUNK
    u = _norm_matmul(x, norm_g, w_in_stack, layer, F32)
    n_steps = n_chunks.bit_length() - 1
    assert (1 << n_steps) == n_chunks
    tk, ta, cq, bq, bm, aq = _s5_param_tables(lam_re, lam_im, log_dt, b_re, b_im, c_re, c_im, n_steps)
    mg, fg, etg = _s5_ops(tk, ta, cq, bq, bm)
    yg = _s5_chunk_core(_s5_to_groups(u), mg, fg, etg, aq)
    y = _s5_to_tokens(yg)
    return _s5_out(x, y, u, d_skip, w_out_stack, layer)


def _qkv_kernel(x_ref, g_ref, w_ref, qg_ref, kg_ref, q_ref, k_ref, v_ref, wb_ref):
    _cast_weight_once(w_ref, wb_ref)
    h = _rms(x_ref[...], g_ref[...]).astype(BF16)
    qkv = _dot(h, wb_ref[...])
    tm = qkv.shape[0]
    lo = lax.broadcasted_iota(jnp.int32, (tm, V7X_LANES), 1) < HEAD_DIM
    scale = 1.0 / math.sqrt(HEAD_DIM)

    def headnorm(xt, gt):
        sq = xt * xt
        s0 = jnp.sum(jnp.where(lo, sq, 0.0), axis=-1, keepdims=True)
        s1 = jnp.sum(jnp.where(lo, 0.0, sq), axis=-1, keepdims=True)
        rs = jnp.where(lo, lax.rsqrt(s0 / HEAD_DIM + RMS_EPS), lax.rsqrt(s1 / HEAD_DIM + RMS_EPS))
        return xt * rs * gt

    for t in range(D_MODEL // V7X_LANES):
        lanes = slice(t * V7X_LANES, (t + 1) * V7X_LANES)
        q_ref[:, lanes] = (headnorm(qkv[:, lanes], qg_ref[...]) * scale).astype(BF16)
        klanes = slice(D_MODEL + t * V7X_LANES, D_MODEL + (t + 1) * V7X_LANES)
        k_ref[:, lanes] = headnorm(qkv[:, klanes], kg_ref[...]).astype(BF16)
    v_ref[...] = qkv[:, 2 * D_MODEL:].astype(BF16)


def _qkv(x, g, w_qkv_stack, layer, q_norm, k_norm, tm=512):
    t, d = x.shape
    qg = jnp.tile(q_norm.astype(F32), 2).reshape(1, V7X_LANES)
    kg = jnp.tile(k_norm.astype(F32), 2).reshape(1, V7X_LANES)
    out = jax.ShapeDtypeStruct((t, d), BF16)
    ospec = pl.BlockSpec((tm, d), lambda i: (i, 0))
    return pl.pallas_call(
        _qkv_kernel,
        grid=(t // tm,),
        in_specs=[
            pl.BlockSpec((tm, d), lambda i: (i, 0)),
            pl.BlockSpec((1, d), lambda i: (0, 0)),
            _layer_weight_spec(w_qkv_stack, layer),
            pl.BlockSpec((1, V7X_LANES), lambda i: (0, 0)),
            pl.BlockSpec((1, V7X_LANES), lambda i: (0, 0)),
        ],
        out_specs=[ospec, ospec, ospec],
        out_shape=[out, out, out],
        scratch_shapes=[pltpu.VMEM((d, 3 * d), BF16)],
        compiler_params=_cparams(1, 52),
        name="nat_qkv",
    )(x, g.reshape(1, d), w_qkv_stack, qg, kg)


def _nat_bias_table(rpb_ref, b_ref):
    w = V7X_LANES
    c = lax.broadcasted_iota(jnp.int32, (GRID_W, w), 0)
    lane = lax.broadcasted_iota(jnp.int32, (GRID_W, w), 1)
    lo = lane < GRID_W
    kc = jnp.where(lo, lane, lane - GRID_W)
    ws = jnp.clip(c - WIN_COLS // 2, 0, GRID_W - WIN_COLS)
    valid = (kc >= ws) & (kc < ws + WIN_COLS)
    n_ri = 2 * WIN_ROWS - 1
    for h in range(2):
        t_lo, t_hi = [], []
        for ri in range(n_ri):
            vb = jnp.broadcast_to(rpb_ref[h, ri:ri + 1, :], (GRID_W, w))
            t_lo.append(pltpu.roll(vb, w - (WIN_COLS - 1), 1, stride=1, stride_axis=0))
            t_hi.append(pltpu.roll(vb, GRID_W - (WIN_COLS - 1), 1, stride=1, stride_axis=0))
        for o in range(WIN_ROWS):
            for j in range(WIN_ROWS // 2):
                tile = jnp.where(lo, t_lo[o + 2 * j], t_hi[o + 2 * j + 1])
                b_ref[h, o, :, j * w:(j + 1) * w] = jnp.where(valid, tile, MASK_VALUE)


def _nat_attn_kernel(q_ref, k_ref, v_ref, rpb_ref, o_ref, b_ref, *, rows, unroll):
    lo = lax.broadcasted_iota(jnp.int32, (GRID_W, V7X_LANES), 1) < HEAD_DIM
    head_mask = (jnp.where(lo, 1.0, 0.0), jnp.where(lo, 0.0, 1.0))
    nkeys = WIN_ROWS * GRID_W
    _nat_bias_table(rpb_ref, b_ref)

    def body(rb, carry):
        chains = []
        for u in range(unroll):
            r = rb * unroll + u
            rs = jnp.clip(r - WIN_ROWS // 2, 0, rows - WIN_ROWS)
            off = rs - r + (WIN_ROWS - 1)
            q = q_ref[pl.ds(pl.multiple_of(r * GRID_W, GRID_W), GRID_W), :].astype(F32)
            k = k_ref[pl.ds(pl.multiple_of(rs * GRID_W, GRID_W), nkeys), :]
            for h in range(2):
                chains.append(dict(r=r, rs=rs, off=off, h=h, q=(q * head_mask[h]).astype(BF16), k=k))
        for c in chains:
            c["s"] = lax.dot_general(c["q"], c["k"], (((1,), (1,)), ((), ())), preferred_element_type=F32)
        for c in chains:
            c["s"] = c["s"] + b_ref[c["h"], c["off"]]
        for c in chains:
            c["m"] = jnp.max(c["s"], axis=-1, keepdims=True)
        for c in chains:
            c["p"] = jnp.exp(c["s"] - c["m"])
        for c in chains:
            c["l"] = jnp.sum(c["p"], axis=-1, keepdims=True)
        for c in chains:
            v = v_ref[pl.ds(pl.multiple_of(c["rs"] * GRID_W, GRID_W), nkeys), :]
            c["o"] = _dot(c["p"].astype(BF16), v) / c["l"]
        for u in range(unroll):
            c0, c1 = chains[2 * u], chains[2 * u + 1]
            o = jnp.where(lo, c0["o"], c1["o"])
            o_ref[pl.ds(pl.multiple_of(c0["r"] * GRID_W, GRID_W), GRID_W), :] = o.astype(BF16)
        return carry

    lax.fori_loop(0, rows // unroll, body, 0)


def _nat_attn(q, k, v, rpb):
    t, d = q.shape
    rows = t // GRID_W
    assert rows >= WIN_ROWS
    unroll = math.gcd(rows, NAT_ROW_UNROLL)
    n_ri, n_ci = rpb.shape[1], rpb.shape[2]
    rpb_pad = jnp.pad(rpb.astype(F32), ((0, 0), (0, 2 * WIN_ROWS - n_ri), (0, V7X_LANES - n_ci)))
    spec = pl.BlockSpec((t, V7X_LANES), lambda i: (0, i))
    return pl.pallas_call(
        functools.partial(_nat_attn_kernel, rows=rows, unroll=unroll),
        grid=(d // V7X_LANES,),
        in_specs=[spec, spec, spec,
                  pl.BlockSpec((2, 2 * WIN_ROWS, V7X_LANES), lambda i: (i, 0, 0))],
        out_specs=spec,
        out_shape=jax.ShapeDtypeStruct((t, d), BF16),
        scratch_shapes=[pltpu.VMEM((2, WIN_ROWS, GRID_W, WIN_ROWS * GRID_W), F32)],
        compiler_params=_cparams(1, 48),
        name="nat_attn",
    )(q, k, v, rpb_pad)


def _matmul_residual_kernel(x_ref, a_ref, w_ref, o_ref, wb_ref):
    _cast_weight_once(w_ref, wb_ref)
    o_ref[...] = x_ref[...] + _dot(a_ref[...], wb_ref[...])


def _matmul_residual(x, a, w_stack, layer, tm=512):
    t, d = x.shape
    kdim = a.shape[1]
    return pl.pallas_call(
        _matmul_residual_kernel,
        grid=(t // tm,),
        in_specs=[
            pl.BlockSpec((tm, d), lambda i: (i, 0)),
            pl.BlockSpec((tm, kdim), lambda i: (i, 0)),
            _layer_weight_spec(w_stack, layer),
        ],
        out_specs=pl.BlockSpec((tm, d), lambda i: (i, 0)),
        out_shape=jax.ShapeDtypeStruct((t, d), F32),
        scratch_shapes=[pltpu.VMEM((kdim, d), BF16)],
        compiler_params=_cparams(1, 32),
        name="matmul_residual",
    )(x, a, w_stack)


def _nat_layer(x, norm_g, layer, w_qkv_stack, q_norm, k_norm, rpb, w_o_stack):
    q, k, v = _qkv(x, norm_g, w_qkv_stack, layer, q_norm, k_norm)
    o = _nat_attn(q, k, v, rpb)
    return _matmul_residual(x, o, w_o_stack, layer)


def _router_kernel(x_ref, g_ref, w_ref, b_ref, tri_ref, ids_ref, wts_ref, cnt_ref):
    @pl.when(pl.program_id(0) == 0)
    def _():
        cnt_ref[...] = jnp.zeros(cnt_ref.shape, F32)

    def split(a):
        hi = a.astype(BF16)
        return hi, (a - hi.astype(F32)).astype(BF16)

    def dot_nt(a, b):
        return lax.dot_general(a, b, (((1,), (1,)), ((), ())), preferred_element_type=F32)

    h_hi, h_lo = split(_rms(x_ref[...], g_ref[...]))
    w_hi, w_lo = split(w_ref[...])
    lg = dot_nt(w_hi, h_hi) + (dot_nt(w_hi, h_lo) + dot_nt(w_lo, h_hi)) + b_ref[...]
    row = lax.broadcasted_iota(jnp.int32, lg.shape, 0)
    big = 4 * V7X_LANES
    is_g = (row >= N_EXPERTS) & (row < N_EXPERTS + N_GROUPS)
    gmax = jnp.max(jnp.where(is_g, lg, MASK_VALUE), axis=0, keepdims=True)
    gsum = jnp.sum(jnp.where(is_g, jnp.exp(lg - gmax), 0.0), axis=0, keepdims=True)
    g_val = 1.0 / gsum
    g_idx = jnp.min(jnp.where(is_g & (lg == gmax), row, big), axis=0, keepdims=True) - N_EXPERTS
    in_grp = (row < N_EXPERTS) & ((row // EXPERTS_PER_GROUP) == g_idx)
    m1 = jnp.max(jnp.where(in_grp, lg, MASK_VALUE), axis=0, keepdims=True)
    i1 = jnp.min(jnp.where(in_grp & (lg == m1), row, big), axis=0, keepdims=True)
    rest = in_grp & (row != i1)
    m2 = jnp.max(jnp.where(rest, lg, MASK_VALUE), axis=0, keepdims=True)
    i2 = jnp.min(jnp.where(rest & (lg == m2), row, big), axis=0, keepdims=True)
    z = jnp.sum(jnp.where(in_grp, jnp.exp(lg - m1), 0.0), axis=0, keepdims=True)
    p1 = 1.0 / z
    p2 = jnp.exp(m2 - m1) / z
    den = p1 + p2
    w1 = g_val * (p1 / den)
    w2 = g_val * (p2 / den)

    sel1 = (row == i1)[:N_EXPERTS]
    sel2 = (row == i2)[:N_EXPERTS]
    onehot = jnp.where(sel1 | sel2, 1.0, 0.0)
    before = _dot(onehot.astype(BF16), tri_ref[...]) + cnt_ref[:, 0:1]
    r1 = jnp.sum(jnp.where(sel1, before, 0.0), axis=0, keepdims=True)
    r2 = jnp.sum(jnp.where(sel2, before, 0.0), axis=0, keepdims=True)
    cnt_ref[...] = cnt_ref[...] + jnp.sum(onehot, axis=1, keepdims=True)

    orow = lax.broadcasted_iota(jnp.int32, ids_ref.shape, 0)
    ids_ref[...] = jnp.where(orow == 0, i1, jnp.where(orow == 1, i2, jnp.where(
        orow == 2, r1.astype(jnp.int32), jnp.where(orow == 3, r2.astype(jnp.int32), 0))))
    wts_ref[...] = jnp.where(orow == 0, w1, jnp.where(orow == 1, w2, 0.0))


def _router(x, g, w_group, b_group, w_expert, b_expert, tm=512):
    t, d = x.shape
    rows = MOE_ROUTER_ROWS
    pad = rows - N_EXPERTS - N_GROUPS
    w = jnp.pad(jnp.concatenate([w_expert, w_group], axis=1).astype(F32).T, ((0, pad), (0, 0)))
    b = jnp.pad(jnp.concatenate([b_expert, b_group]).astype(F32), (0, pad)).reshape(rows, 1)
    tri = (jnp.arange(tm)[:, None] < jnp.arange(tm)[None, :]).astype(BF16)
    return pl.pallas_call(
        _router_kernel,
        grid=(t // tm,),
        in_specs=[
            pl.BlockSpec((tm, d), lambda i: (i, 0)),
            pl.BlockSpec((1, d), lambda i: (0, 0)),
            pl.BlockSpec((rows, d), lambda i: (0, 0)),
            pl.BlockSpec((rows, 1), lambda i: (0, 0)),
            pl.BlockSpec((tm, tm), lambda i: (0, 0)),
        ],
        out_specs=[
            pl.BlockSpec((8, tm), lambda i: (0, i)),
            pl.BlockSpec((8, tm), lambda i: (0, i)),
            pl.BlockSpec((N_EXPERTS, V7X_LANES), lambda i: (0, 0)),
        ],
        out_shape=[
            jax.ShapeDtypeStruct((8, t), jnp.int32),
            jax.ShapeDtypeStruct((8, t), F32),
            jax.ShapeDtypeStruct((N_EXPERTS, V7X_LANES), F32),
        ],
        compiler_params=_cparams(1, 32),
        name="moe_router",
    )(x, g.reshape(1, d), w, b, tri)


def _moe_tables(cnt, t):
    tm = MOE_ROW_TILE
    n_rows = 2 * t
    counts = cnt[:, 0].astype(jnp.int32)
    ends = jnp.cumsum(counts)
    starts = ends - counts
    brk = jnp.concatenate([jnp.arange(n_rows // tm, dtype=jnp.int32) * tm, starts])
    idx = jnp.arange(brk.shape[0])
    before = (brk[None, :] < brk[:, None]) | ((brk[None, :] == brk[:, None]) & (idx[None, :] < idx[:, None]))
    rank = jnp.sum(before.astype(jnp.int32), axis=1)
    lo = jnp.sum(jnp.where(rank[None, :] == idx[:, None], brk[None, :], 0), axis=1)
    hi = jnp.concatenate([lo[1:], jnp.full((1,), n_rows, jnp.int32)])
    anchor = jnp.minimum(lo, n_rows - 1)
    tile = anchor // tm
    expert = jnp.minimum(jnp.sum((ends[None, :] <= anchor[:, None]).astype(jnp.int32), axis=1), N_EXPERTS - 1)
    lo_in = lo - tile * tm
    hi_in = hi - tile * tm
    first = ((hi > lo) & (lo_in == 0)).astype(jnp.int32)
    last = ((hi > lo) & (hi_in == tm)).astype(jnp.int32)
    newexp = jnp.concatenate([jnp.ones((1,), jnp.int32), (expert[1:] != expert[:-1]).astype(jnp.int32)])
    table = jnp.stack([tile, expert, lo_in, hi_in, first, last, newexp]).astype(jnp.int32)
    return starts, table


def _moe_pos_kernel(starts_ref, ids_ref, pos_ref):
    e = ids_ref[0:2, :]
    acc = jnp.zeros(e.shape, jnp.int32)
    for ex in range(N_EXPERTS):
        acc = jnp.where(e == ex, starts_ref[ex], acc)
    pos_ref[0:2, :] = acc + ids_ref[2:4, :]
    pos_ref[2:8, :] = jnp.zeros((6, e.shape[1]), jnp.int32)


def _moe_positions(ids, starts, tm=2048):
    t = ids.shape[1]
    tm = min(tm, t)
    grid_spec = pltpu.PrefetchScalarGridSpec(
        num_scalar_prefetch=1,
        grid=(t // tm,),
        in_specs=[pl.BlockSpec((8, tm), lambda i, st: (0, i))],
        out_specs=pl.BlockSpec((8, tm), lambda i, st: (0, i)),
    )
    return pl.pallas_call(
        _moe_pos_kernel,
        grid_spec=grid_spec,
        out_shape=jax.ShapeDtypeStruct((8, t), jnp.int32),
        compiler_params=_cparams(1, 16),
        name="moe_positions",
    )(starts, ids)


def _tile_positions(pos, tm):
    return pos.reshape(2, -1, tm).transpose(1, 0, 2)


def _to_row_tiles(x, dst_ref):
    rows = x.shape[0]
    for c in range(x.shape[1] // V7X_LANES):
        dst_ref[pl.ds(c, rows, stride=V7X_SUBLANES), :] = x[:, c * V7X_LANES:(c + 1) * V7X_LANES]


def _from_row_tiles(src_ref, rows, base=0):
    return jnp.concatenate([src_ref[pl.ds(base + c, rows, stride=V7X_SUBLANES), :] for c in range(V7X_SUBLANES)], axis=1)


def _row_tile(ref, r):
    return ref.at[pl.ds(pl.multiple_of(r * V7X_SUBLANES, V7X_SUBLANES), V7X_SUBLANES), :]


def _moe_dispatch_kernel(pos_ref, x_ref, g_ref, xs_ref, xt_ref, sem):
    tm = x_ref.shape[0]
    _to_row_tiles(_rms(x_ref[...], g_ref[...]), xt_ref)
    for j in range(tm):
        for k in range(2):
            pltpu.make_async_copy(_row_tile(xt_ref, j), _row_tile(xs_ref, pos_ref[0, k, j]), sem).start(priority=k)
    for k in range(2):
        pltpu.make_async_copy(xt_ref, xs_ref.at[pl.ds(0, V7X_SUBLANES * tm), :], sem).wait()


def _moe_dispatch(x, g, pos, tm=MOE_TOKEN_TILE):
    t, d = x.shape
    assert d == V7X_SUBLANES * V7X_LANES
    return pl.pallas_call(
        _moe_dispatch_kernel,
        grid=(t // tm,),
        in_specs=[
            pl.BlockSpec((1, 2, tm), lambda i: (i, 0, 0), memory_space=pltpu.SMEM),
            pl.BlockSpec((tm, d), lambda i: (i, 0)),
            pl.BlockSpec((1, d), lambda i: (0, 0)),
        ],
        out_specs=pl.BlockSpec(memory_space=pl.ANY),
        out_shape=jax.ShapeDtypeStruct((2 * t * V7X_SUBLANES, V7X_LANES), F32),
        scratch_shapes=[pltpu.VMEM((V7X_SUBLANES * tm, V7X_LANES), F32), pltpu.SemaphoreType.DMA(())],
        compiler_params=_cparams(1, 32),
        name="moe_dispatch",
    )(_tile_positions(pos, tm), x, g.reshape(1, d))


def _moe_ffn_kernel(tab_ref, xs_ref, wg_ref, wu_ref, wd_ref, ys_ref, wgb_ref, wub_ref, wdb_ref, acc_ref):
    i = pl.program_id(0)
    lo = tab_ref[2, i]
    hi = tab_ref[3, i]
    first = tab_ref[4, i] == 1
    last = tab_ref[5, i] == 1
    tm = acc_ref.shape[0]

    @pl.when(tab_ref[6, i] == 1)
    def _():
        wgb_ref[...] = wg_ref[0, 0, 0].astype(BF16)
        wub_ref[...] = wu_ref[0, 0, 0].astype(BF16)
        wdb_ref[...] = wd_ref[0, 0, 0].astype(BF16)

    @pl.when(hi > lo)
    def _():
        h = _from_row_tiles(xs_ref, tm).astype(BF16)
        a = _dot(h, wgb_ref[...])
        u = _dot(h, wub_ref[...])
        rowid = lax.broadcasted_iota(jnp.int32, a.shape, 0)
        act = jnp.where((rowid >= lo) & (rowid < hi), jax.nn.silu(a) * u, 0.0).astype(BF16)
        y = _dot(act, wdb_ref[...])

        @pl.when(first & last)
        def _():
            _to_row_tiles(y, ys_ref)

        @pl.when(first & jnp.logical_not(last))
        def _():
            acc_ref[...] = y

        @pl.when(jnp.logical_not(first) & jnp.logical_not(last))
        def _():
            acc_ref[...] += y

        @pl.when(jnp.logical_not(first) & last)
        def _():
            _to_row_tiles(acc_ref[...] + y, ys_ref)


def _moe_ffn(xs, table, w_gate, w_up, w_down, layer):
    d, ff = w_gate.shape[-2:]
    tm = MOE_ROW_TILE
    n_items = table.shape[1]
    epg = EXPERTS_PER_GROUP
    wmap = lambda i, tab: (layer, tab[1, i] // epg, tab[1, i] % epg, 0, 0)
    grid_spec = pltpu.PrefetchScalarGridSpec(
        num_scalar_prefetch=1,
        grid=(n_items,),
        in_specs=[
            pl.BlockSpec((V7X_SUBLANES * tm, V7X_LANES), lambda i, tab: (tab[0, i], 0)),
            pl.BlockSpec((1, 1, 1, d, ff), wmap),
            pl.BlockSpec((1, 1, 1, d, ff), wmap),
            pl.BlockSpec((1, 1, 1, ff, d), wmap),
        ],
        out_specs=pl.BlockSpec((V7X_SUBLANES * tm, V7X_LANES), lambda i, tab: (tab[0, i], 0)),
        scratch_shapes=[pltpu.VMEM((d, ff), BF16), pltpu.VMEM((d, ff), BF16), pltpu.VMEM((ff, d), BF16),
                        pltpu.VMEM((tm, d), F32)],
    )
    return pl.pallas_call(
        _moe_ffn_kernel,
        grid_spec=grid_spec,
        out_shape=jax.ShapeDtypeStruct(xs.shape, F32),
        compiler_params=_cparams(1, 32),
        name="moe_ffn",
    )(table, xs, w_gate, w_up, w_down)


def _combine_ple_kernel(pos_ref, x_ref, wc_ref, ys_ref, g_ref, p_ref, wg_ref, wp_ref, o_ref,
                        ya_ref, yb_ref, wgb_ref, wpb_ref, sems):
    tm = x_ref.shape[0]
    hm = tm // 2
    for j in range(tm):
        sem = sems.at[j // hm]
        pltpu.make_async_copy(_row_tile(ys_ref, pos_ref[0, 0, j]), _row_tile(ya_ref, j), sem).start(priority=0)
        pltpu.make_async_copy(_row_tile(ys_ref, pos_ref[0, 1, j]), _row_tile(yb_ref, j), sem).start(priority=1)
    _cast_weight_once(wg_ref, wgb_ref)
    _cast_weight_once(wp_ref, wpb_ref)
    for half in range(2):
        rows = pl.ds(half * hm, hm)
        tiles = pl.ds(half * V7X_SUBLANES * hm, V7X_SUBLANES * hm)
        proj = _dot(p_ref[0, 0, rows, :].astype(BF16), wpb_ref[...])
        pltpu.make_async_copy(ys_ref.at[pl.ds(0, V7X_SUBLANES * hm), :], ya_ref.at[tiles, :], sems.at[half]).wait()
        pltpu.make_async_copy(ys_ref.at[pl.ds(0, V7X_SUBLANES * hm), :], yb_ref.at[tiles, :], sems.at[half]).wait()
        w = wc_ref[rows, :]
        x2 = (x_ref[rows, :] + w[:, 0:1] * _from_row_tiles(ya_ref, hm, half * V7X_SUBLANES * hm)
              + w[:, 1:2] * _from_row_tiles(yb_ref, hm, half * V7X_SUBLANES * hm))
        h = _rms(x2, g_ref[...]).astype(BF16)
        gate = jax.nn.sigmoid(_dot(h, wgb_ref[...]))
        o_ref[rows, :] = x2 + gate * proj


def _combine_ple(x, pos, wts, ys, norm_g, p_stack, w_proj_stack, w_gate_stack, layer, tm=MOE_TOKEN_TILE):
    t, d = x.shape
    pd = p_stack.shape[-1]
    wc = wts[0:2].T
    return pl.pallas_call(
        _combine_ple_kernel,
        grid=(t // tm,),
        in_specs=[
            pl.BlockSpec((1, 2, tm), lambda i: (i, 0, 0), memory_space=pltpu.SMEM),
            pl.BlockSpec((tm, d), lambda i: (i, 0)),
            pl.BlockSpec((tm, 2), lambda i: (i, 0)),
            pl.BlockSpec(memory_space=pl.ANY),
            pl.BlockSpec((1, d), lambda i: (0, 0)),
            pl.BlockSpec((1, 1, tm, pd), lambda i: (layer, 0, i, 0)),
            _layer_weight_spec(w_gate_stack, layer),
            _layer_weight_spec(w_proj_stack, layer),
        ],
        out_specs=pl.BlockSpec((tm, d), lambda i: (i, 0)),
        out_shape=jax.ShapeDtypeStruct((t, d), F32),
        scratch_shapes=[pltpu.VMEM((V7X_SUBLANES * tm, V7X_LANES), F32), pltpu.VMEM((V7X_SUBLANES * tm, V7X_LANES), F32),
                        pltpu.VMEM((d, d), BF16), pltpu.VMEM((pd, d), BF16), pltpu.SemaphoreType.DMA((2,))],
        compiler_params=_cparams(1, 40),
        name="moe_combine_ple",
    )(_tile_positions(pos, tm), x, wc, ys, norm_g.reshape(1, d), p_stack, w_gate_stack, w_proj_stack)


def _moe_ple_layer(x, layer, norm_ffn, w_group, b_group, w_expert, b_expert, w_gate, w_up, w_down,
                   norm_ple, p_stack, ple_w_proj, ple_w_gate):
    t = x.shape[0]
    ids, wts, cnt = _router(x, norm_ffn, w_group, b_group, w_expert, b_expert)
    starts, table = _moe_tables(cnt, t)
    pos = _moe_positions(ids, starts)[0:2]
    xs = _moe_dispatch(x, norm_ffn, pos)
    ys = _moe_ffn(xs, table, w_gate, w_up, w_down, layer)
    return _combine_ple(x, pos, wts, ys, norm_ple, p_stack, ple_w_proj, ple_w_gate, layer)


def kernel(x, p, norm_mix, norm_ffn, norm_ple, s5_w_in, s5_lam_re, s5_lam_im, s5_log_dt, s5_b_re, s5_b_im, s5_c_re, s5_c_im, s5_d, s5_w_out, nat_w_qkv, nat_q_norm, nat_k_norm, nat_rpb, nat_w_o, moe_w_group, moe_b_group, moe_w_expert, moe_b_expert, moe_w_gate, moe_w_up, moe_w_down, ple_w_proj, ple_w_gate):
    bsz, seq, d = x.shape
    depth = p.shape[0]
    assert bsz == 1 and d == D_MODEL
    xs = x.reshape(seq, d).astype(F32)
    for i in range(depth):
        j = i // 2
        if i % 2 == 0:
            xs = _s5_layer(xs, norm_mix[i], j, s5_w_in, s5_lam_re[j], s5_lam_im[j], s5_log_dt[j],
                           s5_b_re[j], s5_b_im[j], s5_c_re[j], s5_c_im[j], s5_d[j], s5_w_out)
        else:
            xs = _nat_layer(xs, norm_mix[i], j, nat_w_qkv, nat_q_norm[j], nat_k_norm[j], nat_rpb[j], nat_w_o)
        xs = _moe_ple_layer(xs, i, norm_ffn[i], moe_w_group[i], moe_b_group[i], moe_w_expert[i], moe_b_expert[i],
                            moe_w_gate, moe_w_up, moe_w_down, norm_ple[i], p, ple_w_proj, ple_w_gate)
    return xs.reshape(bsz, seq, d).astype(x.dtype)
```

```python
import functools
import math

import jax
import jax.numpy as jnp
from jax import lax
from jax.experimental import pallas as pl
from jax.experimental.pallas import tpu as pltpu

F32 = jnp.float32
BF16 = jnp.bfloat16

D_MODEL = 1024
GRID_W = 64
S5_GROUP = 16
S5_STATE = 64
S5_GROUPS = 32
HEAD_DIM = 64
WIN_ROWS = 8
WIN_COLS = 16
N_GROUPS = 4
EXPERTS_PER_GROUP = 8
N_EXPERTS = N_GROUPS * EXPERTS_PER_GROUP
RMS_EPS = 1e-6
MASK_VALUE = -1e30

V7X_LANES = 128
V7X_SUBLANES = 8
V7X_VMEM_BYTES = 64 * 1024 * 1024

S5_CHUNK = 64
S5_CHUNK_WIDTH = S5_CHUNK * S5_GROUP

NAT_ROW_UNROLL = 32

MOE_ROUTER_ROWS = 40
MOE_ROW_TILE = 512
MOE_TOKEN_TILE = 512


def _cparams(n_axes, vmem_mib):
    return pltpu.CompilerParams(
        dimension_semantics=("arbitrary",) * n_axes,
        vmem_limit_bytes=min(vmem_mib * 1024 * 1024, V7X_VMEM_BYTES - 4 * 1024 * 1024),
    )


def _dot(a, b):
    return jnp.dot(a, b, preferred_element_type=F32)


def _rms(x, g):
    ms = jnp.mean(x * x, axis=-1, keepdims=True)
    return x * lax.rsqrt(ms + RMS_EPS) * g


def _layer_weight_spec(w_stack, layer):
    _, k, n = w_stack.shape
    return pl.BlockSpec((1, k, n), lambda i: (layer, 0, 0), pipeline_mode=pl.Buffered(1))


def _cast_weight_once(w_ref, wb_ref):
    @pl.when(pl.program_id(0) == 0)
    def _():
        wb_ref[...] = w_ref[0].astype(BF16)


def _s5_param_tables(lam_re, lam_im, log_dt, b_re, b_im, c_re, c_im, n_steps):
    q = S5_CHUNK
    g, p = S5_GROUPS, S5_STATE
    dt = jnp.exp(log_dt.astype(F32))[..., None]
    lam_re = lam_re.astype(F32)
    lam_im = lam_im.astype(F32)
    zr = lam_re * dt
    zi = lam_im * dt
    k = jnp.arange(q + 1, dtype=F32)[:, None]
    mag = jnp.exp(zr[:, :, None, :] * k)
    ang = zi[:, :, None, :] * k
    tr = mag * jnp.cos(ang)
    ti = mag * jnp.sin(ang)
    nr = tr[:, :, 1] - 1.0
    ni = ti[:, :, 1]
    den = lam_re * lam_re + lam_im * lam_im
    cr = (nr * lam_re + ni * lam_im) / den
    ci = (ni * lam_re - nr * lam_im) / den
    b_re = b_re.astype(F32)
    b_im = b_im.astype(F32)
    bbr = cr[..., None] * b_re - ci[..., None] * b_im
    bbi = cr[..., None] * b_im + ci[..., None] * b_re

    def both(fwd, rev):
        return jnp.concatenate([fwd, rev], axis=-1)

    def lag_rows(x):
        fwd = jnp.pad(x[0][:, :q], ((0, 0), (q - 1, 1), (0, 0)))
        rev = jnp.pad(x[1][:, :q][:, ::-1], ((0, 0), (0, q), (0, 0)))
        return both(fwd, rev)

    def f_rows(x):
        return both(x[0][:, :q][:, ::-1], x[1][:, :q])

    def e_rows(x):
        return both(x[0][:, 1:q + 1], x[1][:, 1:q + 1][:, ::-1])

    tk = jnp.stack([lag_rows(tr), lag_rows(ti)], axis=1)
    ta = jnp.stack([f_rows(tr), f_rows(ti), e_rows(tr), e_rows(ti)], axis=1)
    cq = jnp.stack([both(c_re[0], c_re[1]), both(c_im[0], c_im[1])], axis=1).astype(F32)
    bbr_t = jnp.swapaxes(bbr, 2, 3)
    bbi_t = jnp.swapaxes(bbi, 2, 3)
    bq = jnp.stack([both(bbr_t[0], bbr_t[1]), both(bbi_t[0], bbi_t[1])], axis=1)
    bm = jnp.concatenate([bq[:, 0], bq[:, 1]], axis=-1)

    steps = (q * 2.0 ** jnp.arange(n_steps, dtype=F32))[:, None, None, None]
    smag = jnp.exp(zr[None] * steps)
    sang = zi[None] * steps
    aq = jnp.stack([both(smag[:, 0] * jnp.cos(sang[:, 0]), smag[:, 1] * jnp.cos(sang[:, 1])),
                    both(smag[:, 0] * jnp.sin(sang[:, 0]), smag[:, 1] * jnp.sin(sang[:, 1]))], axis=2)
    return tk, ta, cq, bq, bm, aq.transpose(1, 0, 2, 3)


def _s5_ops_kernel(tk_ref, ta_ref, cq_ref, bq_ref, bm_ref, m_ref, f_ref, et_ref):
    w = V7X_LANES
    hh = S5_GROUP
    q = S5_CHUNK

    def outer(t, c):
        return (c[:, None, :] * t[None, :, :]).reshape(hh * t.shape[0], w)

    cre, cim = cq_ref[0, 0], cq_ref[0, 1]
    bre, bim = bq_ref[0, 0], bq_ref[0, 1]
    tre, tim = tk_ref[0, 0], tk_ref[0, 1]
    w_re = outer(tre, cre) - outer(tim, cim)
    w_im = outer(tre, cim) + outer(tim, cre)
    clt = jnp.concatenate([w_re, -w_im], axis=1)
    bk = lax.dot_general(bm_ref[0], clt, (((1,), (1,)), ((), ())), precision=lax.Precision.HIGHEST,
                         preferred_element_type=F32)
    lo = lax.broadcasted_iota(jnp.int32, (q, w), 1) < q
    for hi in range(hh):
        for j in range(hh // 2):
            ka = jnp.broadcast_to(bk[hi:hi + 1, (2 * j) * w:(2 * j + 1) * w], (q, w))
            kb = jnp.broadcast_to(bk[hi:hi + 1, (2 * j + 1) * w:(2 * j + 2) * w], (q, w))
            tile = jnp.where(lo, pltpu.roll(ka, w - (q - 1), 1, stride=1, stride_axis=0),
                             pltpu.roll(kb, 1, 1, stride=1, stride_axis=0))
            m_ref[0, hi * q:(hi + 1) * q, j * w:(j + 1) * w] = tile.astype(BF16)

    fr, fi, er, ei = ta_ref[0, 0], ta_ref[0, 1], ta_ref[0, 2], ta_ref[0, 3]
    f_ref[0, :, 0:w] = (outer(fr, bre) - outer(fi, bim)).astype(BF16)
    f_ref[0, :, w:2 * w] = (outer(fr, bim) + outer(fi, bre)).astype(BF16)
    et_ref[0, :, 0:w] = (outer(er, cre) - outer(ei, cim)).astype(BF16)
    et_ref[0, :, w:2 * w] = (-(outer(er, cim) + outer(ei, cre))).astype(BF16)


def _s5_ops(tk, ta, cq, bq, bm):
    g = tk.shape[0]
    p2 = 2 * S5_STATE
    cw = S5_CHUNK_WIDTH

    def spec(a):
        return pl.BlockSpec((1,) + a.shape[1:], lambda i: (i,) + (0,) * (a.ndim - 1))

    return pl.pallas_call(
        _s5_ops_kernel,
        grid=(g,),
        in_specs=[spec(tk), spec(ta), spec(cq), spec(bq), spec(bm)],
        out_specs=[
            pl.BlockSpec((1, cw, cw), lambda i: (i, 0, 0)),
            pl.BlockSpec((1, cw, 2 * p2), lambda i: (i, 0, 0)),
            pl.BlockSpec((1, cw, 2 * p2), lambda i: (i, 0, 0)),
        ],
        out_shape=[
            jax.ShapeDtypeStruct((g, cw, cw), BF16),
            jax.ShapeDtypeStruct((g, cw, 2 * p2), BF16),
            jax.ShapeDtypeStruct((g, cw, 2 * p2), BF16),
        ],
        compiler_params=_cparams(1, 32),
        name="s5_ops",
    )(tk, ta, cq, bq, bm)


def _s5_chunk_core_kernel(u_ref, m_ref, f_ref, et_ref, a_ref, y_ref, sa_ref, sb_ref, *, n_chunks, n_steps):
    w = V7X_LANES
    n = n_chunks

    @pl.when(pl.program_id(0) == 0)
    def _():
        sa_ref[...] = jnp.zeros(sa_ref.shape, F32)
        sb_ref[...] = jnp.zeros(sb_ref.shape, F32)

    fwd = lax.broadcasted_iota(jnp.int32, (n, w), 1) < S5_STATE
    u = _from_row_tiles(u_ref.at[0], n).astype(BF16)
    sa_ref[n:2 * n, :] = _dot(u, f_ref[0])

    def neighbours(ref, s, lanes):
        return jnp.where(fwd, ref[n - s:2 * n - s, lanes], ref[n + s:2 * n + s, lanes])

    re, im = slice(0, w), slice(w, 2 * w)
    src, dst = sa_ref, sb_ref
    for k in range(n_steps):
        s = 1 << k
        a = a_ref[0, k]
        ar, ai = a[0:1], a[1:2]
        pr = neighbours(src, s, re)
        pi = neighbours(src, s, im)
        dst[n:2 * n, re] = src[n:2 * n, re] + ar * pr - ai * pi
        dst[n:2 * n, im] = src[n:2 * n, im] + ar * pi + ai * pr
        src, dst = dst, src
    s_in = jnp.concatenate([neighbours(src, 1, re), neighbours(src, 1, im)], axis=1).astype(BF16)
    y = _dot(u, m_ref[0]) + lax.dot_general(s_in, et_ref[0], (((1,), (1,)), ((), ())),
                                            preferred_element_type=F32)
    _to_row_tiles(y, y_ref.at[0])


def _s5_chunk_core(ug, mg, fg, etg, aq):
    g = ug.shape[0]
    n_chunks = ug.shape[1] // V7X_SUBLANES
    cw = S5_CHUNK_WIDTH
    n_steps = aq.shape[1]
    assert (1 << n_steps) == n_chunks
    sw = 2 * V7X_LANES
    return pl.pallas_call(
        functools.partial(_s5_chunk_core_kernel, n_chunks=n_chunks, n_steps=n_steps),
        grid=(g,),
        in_specs=[
            pl.BlockSpec((1, V7X_SUBLANES * n_chunks, V7X_LANES), lambda i: (i, 0, 0)),
            pl.BlockSpec((1, cw, cw), lambda i: (i, 0, 0)),
            pl.BlockSpec((1, cw, sw), lambda i: (i, 0, 0)),
            pl.BlockSpec((1, cw, sw), lambda i: (i, 0, 0)),
            pl.BlockSpec((1, n_steps, 2, V7X_LANES), lambda i: (i, 0, 0, 0)),
        ],
        out_specs=pl.BlockSpec((1, V7X_SUBLANES * n_chunks, V7X_LANES), lambda i: (i, 0, 0)),
        out_shape=jax.ShapeDtypeStruct((g, V7X_SUBLANES * n_chunks, V7X_LANES), F32),
        scratch_shapes=[pltpu.VMEM((3 * n_chunks, sw), F32), pltpu.VMEM((3 * n_chunks, sw), F32)],
        compiler_params=_cparams(1, 32),
        name="s5_core",
    )(ug, mg, fg, etg, aq)


S5_RELAYOUT_TOKENS = 1024
S5_RELAYOUT_PITCH = 136


def _s5_to_groups_kernel(u_ref, o_ref, xt_ref):
    w = V7X_LANES
    q = S5_CHUNK
    lo = lax.broadcasted_iota(jnp.int32, (8, w), 1) < q
    n_blk = S5_RELAYOUT_TOKENS // w
    pitch = S5_RELAYOUT_PITCH
    for sg in range(u_ref.shape[1] // w):
        for b in range(n_blk):
            xt_ref[b * pitch:b * pitch + w, :] = u_ref[b * w:(b + 1) * w, sg * w:(sg + 1) * w].T
        for gp in range(w // S5_GROUP):
            for hp in range(S5_GROUP // 2):
                a0 = xt_ref[pl.ds(gp * S5_GROUP + 2 * hp, n_blk, stride=pitch), :]
                a1 = xt_ref[pl.ds(gp * S5_GROUP + 2 * hp + 1, n_blk, stride=pitch), :]
                even = jnp.where(lo, a0, pltpu.roll(a1, q, 1))
                odd = jnp.where(lo, pltpu.roll(a0, q, 1), a1)
                g = sg * (w // S5_GROUP) + gp
                o_ref[g, pl.ds(hp, n_blk, stride=2 * V7X_SUBLANES), :] = even
                o_ref[g, pl.ds(V7X_SUBLANES + hp, n_blk, stride=2 * V7X_SUBLANES), :] = odd


def _s5_to_tokens_kernel(y_ref, o_ref, yt_ref):
    w = V7X_LANES
    q = S5_CHUNK
    lo = lax.broadcasted_iota(jnp.int32, (8, w), 1) < q
    n_blk = S5_RELAYOUT_TOKENS // w
    pitch = S5_RELAYOUT_PITCH
    for sg in range(o_ref.shape[1] // w):
        for gp in range(w // S5_GROUP):
            for hp in range(S5_GROUP // 2):
                g = sg * (w // S5_GROUP) + gp
                even = y_ref[g, pl.ds(hp, n_blk, stride=2 * V7X_SUBLANES), :]
                odd = y_ref[g, pl.ds(V7X_SUBLANES + hp, n_blk, stride=2 * V7X_SUBLANES), :]
                yt_ref[pl.ds(gp * S5_GROUP + 2 * hp, n_blk, stride=pitch), :] = jnp.where(lo, even, pltpu.roll(odd, q, 1))
                yt_ref[pl.ds(gp * S5_GROUP + 2 * hp + 1, n_blk, stride=pitch), :] = jnp.where(lo, pltpu.roll(even, q, 1), odd)
        for b in range(n_blk):
            o_ref[b * w:(b + 1) * w, sg * w:(sg + 1) * w] = yt_ref[b * pitch:b * pitch + w, :].T


def _s5_in_kernel(x_ref, g_ref, w_ref, u_ref, ug_ref, wb_ref, xt_ref):
    _cast_weight_once(w_ref, wb_ref)
    h = _rms(x_ref[...], g_ref[...]).astype(BF16)
    u_ref[...] = _dot(h, wb_ref[...])
    _s5_to_groups_kernel(u_ref, ug_ref, xt_ref)


def _s5_in(x, g, w_stack, layer):
    t, d = x.shape
    n = w_stack.shape[2]
    tt = S5_RELAYOUT_TOKENS
    rows = V7X_SUBLANES * tt // S5_CHUNK
    return pl.pallas_call(
        _s5_in_kernel,
        grid=(t // tt,),
        in_specs=[
            pl.BlockSpec((tt, d), lambda i: (i, 0)),
            pl.BlockSpec((1, d), lambda i: (0, 0)),
            _layer_weight_spec(w_stack, layer),
        ],
        out_specs=[
            pl.BlockSpec((tt, n), lambda i: (i, 0)),
            pl.BlockSpec((S5_GROUPS, rows, V7X_LANES), lambda i: (0, i, 0)),
        ],
        out_shape=[
            jax.ShapeDtypeStruct((t, n), F32),
            jax.ShapeDtypeStruct((S5_GROUPS, V7X_SUBLANES * t // S5_CHUNK, V7X_LANES), F32),
        ],
        scratch_shapes=[pltpu.VMEM((d, n), BF16),
                        pltpu.VMEM((tt // V7X_LANES * S5_RELAYOUT_PITCH, V7X_LANES), F32)],
        compiler_params=_cparams(1, 40),
        name="s5_in",
    )(x, g.reshape(1, d), w_stack)


def _s5_out_kernel(x_ref, yg_ref, u_ref, d_ref, w_ref, o_ref, wb_ref, y_ref, yt_ref):
    _cast_weight_once(w_ref, wb_ref)
    _s5_to_tokens_kernel(yg_ref, y_ref, yt_ref)
    y = y_ref[...] + d_ref[...] * u_ref[...]
    act = jax.nn.gelu(y).astype(BF16)
    vg = _dot(act, wb_ref[...])
    o_ref[...] = x_ref[...] + vg[:, :D_MODEL] * jax.nn.sigmoid(vg[:, D_MODEL:])


def _s5_out(x, yg, u, d_skip, w_out_stack, layer):
    t, d = x.shape
    sw = u.shape[1]
    tt = S5_RELAYOUT_TOKENS
    rows = V7X_SUBLANES * tt // S5_CHUNK
    return pl.pallas_call(
        _s5_out_kernel,
        grid=(t // tt,),
        in_specs=[
            pl.BlockSpec((tt, d), lambda i: (i, 0)),
            pl.BlockSpec((S5_GROUPS, rows, V7X_LANES), lambda i: (0, i, 0)),
            pl.BlockSpec((tt, sw), lambda i: (i, 0)),
            pl.BlockSpec((1, sw), lambda i: (0, 0)),
            _layer_weight_spec(w_out_stack, layer),
        ],
        out_specs=pl.BlockSpec((tt, d), lambda i: (i, 0)),
        out_shape=jax.ShapeDtypeStruct((t, d), F32),
        scratch_shapes=[pltpu.VMEM((sw, 2 * d), BF16), pltpu.VMEM((tt, sw), F32),
                        pltpu.VMEM((tt // V7X_LANES * S5_RELAYOUT_PITCH, V7X_LANES), F32)],
        compiler_params=_cparams(1, 52),
        name="s5_out",
    )(x, yg, u, d_skip.reshape(1, sw).astype(F32), w_out_stack)


def _s5_layer(x, norm_g, layer, w_in_stack, lam_re, lam_im, log_dt, b_re, b_im, c_re, c_im, d_skip, w_out_stack):
    t = x.shape[0]
    n_chunks = t // S5_CHUNK
    u, ug = _s5_in(x, norm_g, w_in_stack, layer)
    n_steps = n_chunks.bit_length() - 1
    assert (1 << n_steps) == n_chunks
    tk, ta, cq, bq, bm, aq = _s5_param_tables(lam_re, lam_im, log_dt, b_re, b_im, c_re, c_im, n_steps)
    mg, fg, etg = _s5_ops(tk, ta, cq, bq, bm)
    yg = _s5_chunk_core(ug, mg, fg, etg, aq)
    return _s5_out(x, yg, u, d_skip, w_out_stack, layer)


def _qkv_kernel(x_ref, g_ref, w_ref, qg_ref, kg_ref, q_ref, k_ref, v_ref, wb_ref):
    _cast_weight_once(w_ref, wb_ref)
    h = _rms(x_ref[...], g_ref[...]).astype(BF16)
    qkv = _dot(h, wb_ref[...])
    tm = qkv.shape[0]
    lo = lax.broadcasted_iota(jnp.int32, (tm, V7X_LANES), 1) < HEAD_DIM
    scale = 1.0 / math.sqrt(HEAD_DIM)

    def headnorm(xt, gt):
        sq = xt * xt
        s0 = jnp.sum(jnp.where(lo, sq, 0.0), axis=-1, keepdims=True)
        s1 = jnp.sum(jnp.where(lo, 0.0, sq), axis=-1, keepdims=True)
        rs = jnp.where(lo, lax.rsqrt(s0 / HEAD_DIM + RMS_EPS), lax.rsqrt(s1 / HEAD_DIM + RMS_EPS))
        return xt * rs * gt

    for t in range(D_MODEL // V7X_LANES):
        lanes = slice(t * V7X_LANES, (t + 1) * V7X_LANES)
        q_ref[:, lanes] = (headnorm(qkv[:, lanes], qg_ref[...]) * scale).astype(BF16)
        klanes = slice(D_MODEL + t * V7X_LANES, D_MODEL + (t + 1) * V7X_LANES)
        k_ref[:, lanes] = headnorm(qkv[:, klanes], kg_ref[...]).astype(BF16)
    v_ref[...] = qkv[:, 2 * D_MODEL:].astype(BF16)


def _qkv(x, g, w_qkv_stack, layer, q_norm, k_norm, tm=512):
    t, d = x.shape
    qg = jnp.tile(q_norm.astype(F32), 2).reshape(1, V7X_LANES)
    kg = jnp.tile(k_norm.astype(F32), 2).reshape(1, V7X_LANES)
    out = jax.ShapeDtypeStruct((t, d), BF16)
    ospec = pl.BlockSpec((tm, d), lambda i: (i, 0))
    return pl.pallas_call(
        _qkv_kernel,
        grid=(t // tm,),
        in_specs=[
            pl.BlockSpec((tm, d), lambda i: (i, 0)),
            pl.BlockSpec((1, d), lambda i: (0, 0)),
            _layer_weight_spec(w_qkv_stack, layer),
            pl.BlockSpec((1, V7X_LANES), lambda i: (0, 0)),
            pl.BlockSpec((1, V7X_LANES), lambda i: (0, 0)),
        ],
        out_specs=[ospec, ospec, ospec],
        out_shape=[out, out, out],
        scratch_shapes=[pltpu.VMEM((d, 3 * d), BF16)],
        compiler_params=_cparams(1, 52),
        name="nat_qkv",
    )(x, g.reshape(1, d), w_qkv_stack, qg, kg)


def _nat_bias_table(rpb_ref, b_ref):
    w = V7X_LANES
    c = lax.broadcasted_iota(jnp.int32, (GRID_W, w), 0)
    lane = lax.broadcasted_iota(jnp.int32, (GRID_W, w), 1)
    lo = lane < GRID_W
    kc = jnp.where(lo, lane, lane - GRID_W)
    ws = jnp.clip(c - WIN_COLS // 2, 0, GRID_W - WIN_COLS)
    valid = (kc >= ws) & (kc < ws + WIN_COLS)
    n_ri = 2 * WIN_ROWS - 1
    for h in range(2):
        t_lo, t_hi = [], []
        for ri in range(n_ri):
            vb = jnp.broadcast_to(rpb_ref[h, ri:ri + 1, :], (GRID_W, w))
            t_lo.append(pltpu.roll(vb, w - (WIN_COLS - 1), 1, stride=1, stride_axis=0))
            t_hi.append(pltpu.roll(vb, GRID_W - (WIN_COLS - 1), 1, stride=1, stride_axis=0))
        for o in range(WIN_ROWS):
            for j in range(WIN_ROWS // 2):
                tile = jnp.where(lo, t_lo[o + 2 * j], t_hi[o + 2 * j + 1])
                b_ref[h, o, :, j * w:(j + 1) * w] = jnp.where(valid, tile, MASK_VALUE)


def _nat_attn_kernel(q_ref, k_ref, v_ref, rpb_ref, o_ref, b_ref, *, rows, unroll):
    lo = lax.broadcasted_iota(jnp.int32, (GRID_W, V7X_LANES), 1) < HEAD_DIM
    head_mask = (jnp.where(lo, 1.0, 0.0), jnp.where(lo, 0.0, 1.0))
    nkeys = WIN_ROWS * GRID_W
    _nat_bias_table(rpb_ref, b_ref)

    def body(rb, carry):
        chains = []
        for u in range(unroll):
            r = rb * unroll + u
            rs = jnp.clip(r - WIN_ROWS // 2, 0, rows - WIN_ROWS)
            off = rs - r + (WIN_ROWS - 1)
            q = q_ref[pl.ds(pl.multiple_of(r * GRID_W, GRID_W), GRID_W), :].astype(F32)
            k = k_ref[pl.ds(pl.multiple_of(rs * GRID_W, GRID_W), nkeys), :]
            for h in range(2):
                chains.append(dict(r=r, rs=rs, off=off, h=h, q=(q * head_mask[h]).astype(BF16), k=k))
        for c in chains:
            c["s"] = lax.dot_general(c["q"], c["k"], (((1,), (1,)), ((), ())), preferred_element_type=F32)
        for c in chains:
            c["s"] = c["s"] + b_ref[c["h"], c["off"]]
        for c in chains:
            c["m"] = jnp.max(c["s"], axis=-1, keepdims=True)
        for c in chains:
            c["p"] = jnp.exp(c["s"] - c["m"])
        for c in chains:
            c["l"] = jnp.sum(c["p"], axis=-1, keepdims=True)
        for c in chains:
            v = v_ref[pl.ds(pl.multiple_of(c["rs"] * GRID_W, GRID_W), nkeys), :]
            c["o"] = _dot(c["p"].astype(BF16), v) / c["l"]
        for u in range(unroll):
            c0, c1 = chains[2 * u], chains[2 * u + 1]
            o = jnp.where(lo, c0["o"], c1["o"])
            o_ref[pl.ds(pl.multiple_of(c0["r"] * GRID_W, GRID_W), GRID_W), :] = o.astype(BF16)
        return carry

    lax.fori_loop(0, rows // unroll, body, 0)


def _nat_attn(q, k, v, rpb):
    t, d = q.shape
    rows = t // GRID_W
    assert rows >= WIN_ROWS
    unroll = math.gcd(rows, NAT_ROW_UNROLL)
    n_ri, n_ci = rpb.shape[1], rpb.shape[2]
    rpb_pad = jnp.pad(rpb.astype(F32), ((0, 0), (0, 2 * WIN_ROWS - n_ri), (0, V7X_LANES - n_ci)))
    spec = pl.BlockSpec((t, V7X_LANES), lambda i: (0, i))
    return pl.pallas_call(
        functools.partial(_nat_attn_kernel, rows=rows, unroll=unroll),
        grid=(d // V7X_LANES,),
        in_specs=[spec, spec, spec,
                  pl.BlockSpec((2, 2 * WIN_ROWS, V7X_LANES), lambda i: (i, 0, 0))],
        out_specs=spec,
        out_shape=jax.ShapeDtypeStruct((t, d), BF16),
        scratch_shapes=[pltpu.VMEM((2, WIN_ROWS, GRID_W, WIN_ROWS * GRID_W), F32)],
        compiler_params=_cparams(1, 48),
        name="nat_attn",
    )(q, k, v, rpb_pad)


def _matmul_residual_kernel(x_ref, a_ref, w_ref, o_ref, wb_ref):
    _cast_weight_once(w_ref, wb_ref)
    o_ref[...] = x_ref[...] + _dot(a_ref[...], wb_ref[...])


def _matmul_residual(x, a, w_stack, layer, tm=512):
    t, d = x.shape
    kdim = a.shape[1]
    return pl.pallas_call(
        _matmul_residual_kernel,
        grid=(t // tm,),
        in_specs=[
            pl.BlockSpec((tm, d), lambda i: (i, 0)),
            pl.BlockSpec((tm, kdim), lambda i: (i, 0)),
            _layer_weight_spec(w_stack, layer),
        ],
        out_specs=pl.BlockSpec((tm, d), lambda i: (i, 0)),
        out_shape=jax.ShapeDtypeStruct((t, d), F32),
        scratch_shapes=[pltpu.VMEM((kdim, d), BF16)],
        compiler_params=_cparams(1, 32),
        name="matmul_residual",
    )(x, a, w_stack)


def _nat_layer(x, norm_g, layer, w_qkv_stack, q_norm, k_norm, rpb, w_o_stack):
    q, k, v = _qkv(x, norm_g, w_qkv_stack, layer, q_norm, k_norm)
    o = _nat_attn(q, k, v, rpb)
    return _matmul_residual(x, o, w_o_stack, layer)


def _router_kernel(x_ref, g_ref, w_ref, b_ref, tri_ref, ids_ref, wts_ref, cnt_ref):
    @pl.when(pl.program_id(0) == 0)
    def _():
        cnt_ref[...] = jnp.zeros(cnt_ref.shape, F32)

    def split(a):
        hi = a.astype(BF16)
        return hi, (a - hi.astype(F32)).astype(BF16)

    def dot_nt(a, b):
        return lax.dot_general(a, b, (((1,), (1,)), ((), ())), preferred_element_type=F32)

    h_hi, h_lo = split(_rms(x_ref[...], g_ref[...]))
    w_hi, w_lo = split(w_ref[...])
    lg = dot_nt(w_hi, h_hi) + (dot_nt(w_hi, h_lo) + dot_nt(w_lo, h_hi)) + b_ref[...]
    row = lax.broadcasted_iota(jnp.int32, lg.shape, 0)
    big = 4 * V7X_LANES
    is_g = (row >= N_EXPERTS) & (row < N_EXPERTS + N_GROUPS)
    gmax = jnp.max(jnp.where(is_g, lg, MASK_VALUE), axis=0, keepdims=True)
    gsum = jnp.sum(jnp.where(is_g, jnp.exp(lg - gmax), 0.0), axis=0, keepdims=True)
    g_val = 1.0 / gsum
    g_idx = jnp.min(jnp.where(is_g & (lg == gmax), row, big), axis=0, keepdims=True) - N_EXPERTS
    in_grp = (row < N_EXPERTS) & ((row // EXPERTS_PER_GROUP) == g_idx)
    m1 = jnp.max(jnp.where(in_grp, lg, MASK_VALUE), axis=0, keepdims=True)
    i1 = jnp.min(jnp.where(in_grp & (lg == m1), row, big), axis=0, keepdims=True)
    rest = in_grp & (row != i1)
    m2 = jnp.max(jnp.where(rest, lg, MASK_VALUE), axis=0, keepdims=True)
    i2 = jnp.min(jnp.where(rest & (lg == m2), row, big), axis=0, keepdims=True)
    z = jnp.sum(jnp.where(in_grp, jnp.exp(lg - m1), 0.0), axis=0, keepdims=True)
    p1 = 1.0 / z
    p2 = jnp.exp(m2 - m1) / z
    den = p1 + p2
    w1 = g_val * (p1 / den)
    w2 = g_val * (p2 / den)

    sel1 = (row == i1)[:N_EXPERTS]
    sel2 = (row == i2)[:N_EXPERTS]
    onehot = jnp.where(sel1 | sel2, 1.0, 0.0)
    before = _dot(onehot.astype(BF16), tri_ref[...]) + cnt_ref[:, 0:1]
    r1 = jnp.sum(jnp.where(sel1, before, 0.0), axis=0, keepdims=True)
    r2 = jnp.sum(jnp.where(sel2, before, 0.0), axis=0, keepdims=True)
    cnt_ref[...] = cnt_ref[...] + jnp.sum(onehot, axis=1, keepdims=True)

    orow = lax.broadcasted_iota(jnp.int32, ids_ref.shape, 0)
    ids_ref[...] = jnp.where(orow == 0, i1, jnp.where(orow == 1, i2, jnp.where(
        orow == 2, r1.astype(jnp.int32), jnp.where(orow == 3, r2.astype(jnp.int32), 0))))
    wts_ref[...] = jnp.where(orow == 0, w1, jnp.where(orow == 1, w2, 0.0))


def _router(x, g, w_group, b_group, w_expert, b_expert, tm=512):
    t, d = x.shape
    rows = MOE_ROUTER_ROWS
    pad = rows - N_EXPERTS - N_GROUPS
    w = jnp.pad(jnp.concatenate([w_expert, w_group], axis=1).astype(F32).T, ((0, pad), (0, 0)))
    b = jnp.pad(jnp.concatenate([b_expert, b_group]).astype(F32), (0, pad)).reshape(rows, 1)
    tri = (jnp.arange(tm)[:, None] < jnp.arange(tm)[None, :]).astype(BF16)
    return pl.pallas_call(
        _router_kernel,
        grid=(t // tm,),
        in_specs=[
            pl.BlockSpec((tm, d), lambda i: (i, 0)),
            pl.BlockSpec((1, d), lambda i: (0, 0)),
            pl.BlockSpec((rows, d), lambda i: (0, 0)),
            pl.BlockSpec((rows, 1), lambda i: (0, 0)),
            pl.BlockSpec((tm, tm), lambda i: (0, 0)),
        ],
        out_specs=[
            pl.BlockSpec((8, tm), lambda i: (0, i)),
            pl.BlockSpec((8, tm), lambda i: (0, i)),
            pl.BlockSpec((N_EXPERTS, V7X_LANES), lambda i: (0, 0)),
        ],
        out_shape=[
            jax.ShapeDtypeStruct((8, t), jnp.int32),
            jax.ShapeDtypeStruct((8, t), F32),
            jax.ShapeDtypeStruct((N_EXPERTS, V7X_LANES), F32),
        ],
        compiler_params=_cparams(1, 32),
        name="moe_router",
    )(x, g.reshape(1, d), w, b, tri)


def _moe_tables(cnt, t):
    tm = MOE_ROW_TILE
    n_rows = 2 * t
    counts = cnt[:, 0].astype(jnp.int32)
    ends = jnp.cumsum(counts)
    starts = ends - counts
    brk = jnp.concatenate([jnp.arange(n_rows // tm, dtype=jnp.int32) * tm, starts])
    idx = jnp.arange(brk.shape[0])
    before = (brk[None, :] < brk[:, None]) | ((brk[None, :] == brk[:, None]) & (idx[None, :] < idx[:, None]))
    rank = jnp.sum(before.astype(jnp.int32), axis=1)
    lo = jnp.sum(jnp.where(rank[None, :] == idx[:, None], brk[None, :], 0), axis=1)
    hi = jnp.concatenate([lo[1:], jnp.full((1,), n_rows, jnp.int32)])
    anchor = jnp.minimum(lo, n_rows - 1)
    tile = anchor // tm
    expert = jnp.minimum(jnp.sum((ends[None, :] <= anchor[:, None]).astype(jnp.int32), axis=1), N_EXPERTS - 1)
    lo_in = lo - tile * tm
    hi_in = hi - tile * tm
    first = ((hi > lo) & (lo_in == 0)).astype(jnp.int32)
    last = ((hi > lo) & (hi_in == tm)).astype(jnp.int32)
    newexp = jnp.concatenate([jnp.ones((1,), jnp.int32), (expert[1:] != expert[:-1]).astype(jnp.int32)])
    table = jnp.stack([tile, expert, lo_in, hi_in, first, last, newexp]).astype(jnp.int32)
    return starts, table


def _moe_pos_kernel(starts_ref, ids_ref, pos_ref):
    e = ids_ref[0:2, :]
    acc = jnp.zeros(e.shape, jnp.int32)
    for ex in range(N_EXPERTS):
        acc = jnp.where(e == ex, starts_ref[ex], acc)
    pos_ref[0:2, :] = acc + ids_ref[2:4, :]
    pos_ref[2:8, :] = jnp.zeros((6, e.shape[1]), jnp.int32)


def _moe_positions(ids, starts, tm=2048):
    t = ids.shape[1]
    tm = min(tm, t)
    grid_spec = pltpu.PrefetchScalarGridSpec(
        num_scalar_prefetch=1,
        grid=(t // tm,),
        in_specs=[pl.BlockSpec((8, tm), lambda i, st: (0, i))],
        out_specs=pl.BlockSpec((8, tm), lambda i, st: (0, i)),
    )
    return pl.pallas_call(
        _moe_pos_kernel,
        grid_spec=grid_spec,
        out_shape=jax.ShapeDtypeStruct((8, t), jnp.int32),
        compiler_params=_cparams(1, 16),
        name="moe_positions",
    )(starts, ids)


def _tile_positions(pos, tm):
    return pos.reshape(2, -1, tm).transpose(1, 0, 2)


def _to_row_tiles(x, dst_ref):
    rows = x.shape[0]
    for c in range(x.shape[1] // V7X_LANES):
        dst_ref[pl.ds(c, rows, stride=V7X_SUBLANES), :] = x[:, c * V7X_LANES:(c + 1) * V7X_LANES]


def _from_row_tiles(src_ref, rows, base=0):
    return jnp.concatenate([src_ref[pl.ds(base + c, rows, stride=V7X_SUBLANES), :] for c in range(V7X_SUBLANES)], axis=1)


def _row_tile(ref, r):
    return ref.at[pl.ds(pl.multiple_of(r * V7X_SUBLANES, V7X_SUBLANES), V7X_SUBLANES), :]


def _moe_dispatch_kernel(pos_ref, x_ref, g_ref, xs_ref, xt_ref, sem):
    tm = x_ref.shape[0]
    _to_row_tiles(_rms(x_ref[...], g_ref[...]), xt_ref)
    for j in range(tm):
        for k in range(2):
            pltpu.make_async_copy(_row_tile(xt_ref, j), _row_tile(xs_ref, pos_ref[0, k, j]), sem).start(priority=k)
    for k in range(2):
        pltpu.make_async_copy(xt_ref, xs_ref.at[pl.ds(0, V7X_SUBLANES * tm), :], sem).wait()


def _moe_dispatch(x, g, pos, tm=MOE_TOKEN_TILE):
    t, d = x.shape
    assert d == V7X_SUBLANES * V7X_LANES
    return pl.pallas_call(
        _moe_dispatch_kernel,
        grid=(t // tm,),
        in_specs=[
            pl.BlockSpec((1, 2, tm), lambda i: (i, 0, 0), memory_space=pltpu.SMEM),
            pl.BlockSpec((tm, d), lambda i: (i, 0)),
            pl.BlockSpec((1, d), lambda i: (0, 0)),
        ],
        out_specs=pl.BlockSpec(memory_space=pl.ANY),
        out_shape=jax.ShapeDtypeStruct((2 * t * V7X_SUBLANES, V7X_LANES), F32),
        scratch_shapes=[pltpu.VMEM((V7X_SUBLANES * tm, V7X_LANES), F32), pltpu.SemaphoreType.DMA(())],
        compiler_params=_cparams(1, 32),
        name="moe_dispatch",
    )(_tile_positions(pos, tm), x, g.reshape(1, d))


def _moe_ffn_kernel(tab_ref, xs_ref, wg_ref, wu_ref, wd_ref, ys_ref, wgb_ref, wub_ref, wdb_ref, acc_ref):
    i = pl.program_id(0)
    lo = tab_ref[2, i]
    hi = tab_ref[3, i]
    first = tab_ref[4, i] == 1
    last = tab_ref[5, i] == 1
    tm = acc_ref.shape[0]

    @pl.when(tab_ref[6, i] == 1)
    def _():
        wgb_ref[...] = wg_ref[0, 0, 0].astype(BF16)
        wub_ref[...] = wu_ref[0, 0, 0].astype(BF16)
        wdb_ref[...] = wd_ref[0, 0, 0].astype(BF16)

    @pl.when(hi > lo)
    def _():
        h = _from_row_tiles(xs_ref, tm).astype(BF16)
        a = _dot(h, wgb_ref[...])
        u = _dot(h, wub_ref[...])
        rowid = lax.broadcasted_iota(jnp.int32, a.shape, 0)
        act = jnp.where((rowid >= lo) & (rowid < hi), jax.nn.silu(a) * u, 0.0).astype(BF16)
        y = _dot(act, wdb_ref[...])

        @pl.when(first & last)
        def _():
            _to_row_tiles(y, ys_ref)

        @pl.when(first & jnp.logical_not(last))
        def _():
            acc_ref[...] = y

        @pl.when(jnp.logical_not(first) & jnp.logical_not(last))
        def _():
            acc_ref[...] += y

        @pl.when(jnp.logical_not(first) & last)
        def _():
            _to_row_tiles(acc_ref[...] + y, ys_ref)


def _moe_ffn(xs, table, w_gate, w_up, w_down, layer):
    d, ff = w_gate.shape[-2:]
    tm = MOE_ROW_TILE
    n_items = table.shape[1]
    epg = EXPERTS_PER_GROUP
    wmap = lambda i, tab: (layer, tab[1, i] // epg, tab[1, i] % epg, 0, 0)
    grid_spec = pltpu.PrefetchScalarGridSpec(
        num_scalar_prefetch=1,
        grid=(n_items,),
        in_specs=[
            pl.BlockSpec((V7X_SUBLANES * tm, V7X_LANES), lambda i, tab: (tab[0, i], 0)),
            pl.BlockSpec((1, 1, 1, d, ff), wmap),
            pl.BlockSpec((1, 1, 1, d, ff), wmap),
            pl.BlockSpec((1, 1, 1, ff, d), wmap),
        ],
        out_specs=pl.BlockSpec((V7X_SUBLANES * tm, V7X_LANES), lambda i, tab: (tab[0, i], 0)),
        scratch_shapes=[pltpu.VMEM((d, ff), BF16), pltpu.VMEM((d, ff), BF16), pltpu.VMEM((ff, d), BF16),
                        pltpu.VMEM((tm, d), F32)],
    )
    return pl.pallas_call(
        _moe_ffn_kernel,
        grid_spec=grid_spec,
        out_shape=jax.ShapeDtypeStruct(xs.shape, F32),
        compiler_params=_cparams(1, 32),
        name="moe_ffn",
    )(table, xs, w_gate, w_up, w_down)


def _combine_ple_kernel(pos_ref, x_ref, wc_ref, ys_ref, g_ref, p_ref, wg_ref, wp_ref, o_ref,
                        ya_ref, yb_ref, wgb_ref, wpb_ref, sems):
    tm = x_ref.shape[0]
    hm = tm // 2
    for j in range(tm):
        sem = sems.at[j // hm]
        pltpu.make_async_copy(_row_tile(ys_ref, pos_ref[0, 0, j]), _row_tile(ya_ref, j), sem).start(priority=0)
        pltpu.make_async_copy(_row_tile(ys_ref, pos_ref[0, 1, j]), _row_tile(yb_ref, j), sem).start(priority=1)
    _cast_weight_once(wg_ref, wgb_ref)
    _cast_weight_once(wp_ref, wpb_ref)
    for half in range(2):
        rows = pl.ds(half * hm, hm)
        tiles = pl.ds(half * V7X_SUBLANES * hm, V7X_SUBLANES * hm)
        proj = _dot(p_ref[0, 0, rows, :].astype(BF16), wpb_ref[...])
        pltpu.make_async_copy(ys_ref.at[pl.ds(0, V7X_SUBLANES * hm), :], ya_ref.at[tiles, :], sems.at[half]).wait()
        pltpu.make_async_copy(ys_ref.at[pl.ds(0, V7X_SUBLANES * hm), :], yb_ref.at[tiles, :], sems.at[half]).wait()
        w = wc_ref[rows, :]
        x2 = (x_ref[rows, :] + w[:, 0:1] * _from_row_tiles(ya_ref, hm, half * V7X_SUBLANES * hm)
              + w[:, 1:2] * _from_row_tiles(yb_ref, hm, half * V7X_SUBLANES * hm))
        h = _rms(x2, g_ref[...]).astype(BF16)
        gate = jax.nn.sigmoid(_dot(h, wgb_ref[...]))
        o_ref[rows, :] = x2 + gate * proj


def _combine_ple(x, pos, wts, ys, norm_g, p_stack, w_proj_stack, w_gate_stack, layer, tm=MOE_TOKEN_TILE):
    t, d = x.shape
    pd = p_stack.shape[-1]
    wc = wts[0:2].T
    return pl.pallas_call(
        _combine_ple_kernel,
        grid=(t // tm,),
        in_specs=[
            pl.BlockSpec((1, 2, tm), lambda i: (i, 0, 0), memory_space=pltpu.SMEM),
            pl.BlockSpec((tm, d), lambda i: (i, 0)),
            pl.BlockSpec((tm, 2), lambda i: (i, 0)),
            pl.BlockSpec(memory_space=pl.ANY),
            pl.BlockSpec((1, d), lambda i: (0, 0)),
            pl.BlockSpec((1, 1, tm, pd), lambda i: (layer, 0, i, 0)),
            _layer_weight_spec(w_gate_stack, layer),
            _layer_weight_spec(w_proj_stack, layer),
        ],
        out_specs=pl.BlockSpec((tm, d), lambda i: (i, 0)),
        out_shape=jax.ShapeDtypeStruct((t, d), F32),
        scratch_shapes=[pltpu.VMEM((V7X_SUBLANES * tm, V7X_LANES), F32), pltpu.VMEM((V7X_SUBLANES * tm, V7X_LANES), F32),
                        pltpu.VMEM((d, d), BF16), pltpu.VMEM((pd, d), BF16), pltpu.SemaphoreType.DMA((2,))],
        compiler_params=_cparams(1, 40),
        name="moe_combine_ple",
    )(_tile_positions(pos, tm), x, wc, ys, norm_g.reshape(1, d), p_stack, w_gate_stack, w_proj_stack)


def _moe_ple_layer(x, layer, norm_ffn, w_group, b_group, w_expert, b_expert, w_gate, w_up, w_down,
                   norm_ple, p_stack, ple_w_proj, ple_w_gate):
    t = x.shape[0]
    ids, wts, cnt = _router(x, norm_ffn, w_group, b_group, w_expert, b_expert)
    starts, table = _moe_tables(cnt, t)
    pos = _moe_positions(ids, starts)[0:2]
    xs = _moe_dispatch(x, norm_ffn, pos)
    ys = _moe_ffn(xs, table, w_gate, w_up, w_down, layer)
    return _combine_ple(x, pos, wts, ys, norm_ple, p_stack, ple_w_proj, ple_w_gate, layer)


def kernel(x, p, norm_mix, norm_ffn, norm_ple, s5_w_in, s5_lam_re, s5_lam_im, s5_log_dt, s5_b_re, s5_b_im, s5_c_re, s5_c_im, s5_d, s5_w_out, nat_w_qkv, nat_q_norm, nat_k_norm, nat_rpb, nat_w_o, moe_w_group, moe_b_group, moe_w_expert, moe_b_expert, moe_w_gate, moe_w_up, moe_w_down, ple_w_proj, ple_w_gate):
    bsz, seq, d = x.shape
    depth = p.shape[0]
    assert bsz == 1 and d == D_MODEL
    xs = x.reshape(seq, d).astype(F32)
    for i in range(depth):
        j = i // 2
        if i % 2 == 0:
            xs = _s5_layer(xs, norm_mix[i], j, s5_w_in, s5_lam_re[j], s5_lam_im[j], s5_log_dt[j],
                           s5_b_re[j], s5_b_im[j], s5_c_re[j], s5_c_im[j], s5_d[j], s5_w_out)
        else:
            xs = _nat_layer(xs, norm_mix[i], j, nat_w_qkv, nat_q_norm[j], nat_k_norm[j], nat_rpb[j], nat_w_o)
        xs = _moe_ple_layer(xs, i, norm_ffn[i], moe_w_group[i], moe_b_group[i], moe_w_expert[i], moe_b_expert[i],
                            moe_w_gate, moe_w_up, moe_w_down, norm_ple[i], p, ple_w_proj, ple_w_gate)
    return xs.reshape(bsz, seq, d).astype(x.dtype)
```

```python
import functools
import math

import jax
import jax.numpy as jnp
from jax import lax
from jax.experimental import pallas as pl
from jax.experimental.pallas import tpu as pltpu

F32 = jnp.float32
BF16 = jnp.bfloat16

D_MODEL = 1024
GRID_W = 64
S5_GROUP = 16
S5_STATE = 64
S5_GROUPS = 32
HEAD_DIM = 64
WIN_ROWS = 8
WIN_COLS = 16
N_GROUPS = 4
EXPERTS_PER_GROUP = 8
N_EXPERTS = N_GROUPS * EXPERTS_PER_GROUP
RMS_EPS = 1e-6
MASK_VALUE = -1e30

V7X_LANES = 128
V7X_SUBLANES = 8
V7X_VMEM_BYTES = 64 * 1024 * 1024

S5_CHUNK = 64
S5_CHUNK_WIDTH = S5_CHUNK * S5_GROUP

NAT_ROW_UNROLL = 32

MOE_ROUTER_ROWS = 40
MOE_ROW_TILE = 512
MOE_TOKEN_TILE = 512


def _cparams(n_axes, vmem_mib):
    return pltpu.CompilerParams(
        dimension_semantics=("arbitrary",) * n_axes,
        vmem_limit_bytes=min(vmem_mib * 1024 * 1024, V7X_VMEM_BYTES - 4 * 1024 * 1024),
    )


def _dot(a, b):
    return jnp.dot(a, b, preferred_element_type=F32)


def _rms(x, g):
    ms = jnp.mean(x * x, axis=-1, keepdims=True)
    return x * lax.rsqrt(ms + RMS_EPS) * g


def _layer_weight_spec(w_stack, layer):
    _, k, n = w_stack.shape
    return pl.BlockSpec((1, k, n), lambda i: (layer, 0, 0), pipeline_mode=pl.Buffered(1))


def _cast_weight_once(w_ref, wb_ref):
    @pl.when(pl.program_id(0) == 0)
    def _():
        wb_ref[...] = w_ref[0].astype(BF16)


def _s5_param_tables(lam_re, lam_im, log_dt, b_re, b_im, c_re, c_im, n_steps):
    q = S5_CHUNK
    g, p = S5_GROUPS, S5_STATE
    dt = jnp.exp(log_dt.astype(F32))[..., None]
    lam_re = lam_re.astype(F32)
    lam_im = lam_im.astype(F32)
    zr = lam_re * dt
    zi = lam_im * dt
    k = jnp.arange(q + 1, dtype=F32)[:, None]
    mag = jnp.exp(zr[:, :, None, :] * k)
    ang = zi[:, :, None, :] * k
    tr = mag * jnp.cos(ang)
    ti = mag * jnp.sin(ang)
    nr = tr[:, :, 1] - 1.0
    ni = ti[:, :, 1]
    den = lam_re * lam_re + lam_im * lam_im
    cr = (nr * lam_re + ni * lam_im) / den
    ci = (ni * lam_re - nr * lam_im) / den
    b_re = b_re.astype(F32)
    b_im = b_im.astype(F32)
    bbr = cr[..., None] * b_re - ci[..., None] * b_im
    bbi = cr[..., None] * b_im + ci[..., None] * b_re

    def both(fwd, rev):
        return jnp.concatenate([fwd, rev], axis=-1)

    def lag_rows(x):
        fwd = jnp.pad(x[0][:, :q], ((0, 0), (q - 1, 1), (0, 0)))
        rev = jnp.pad(x[1][:, :q][:, ::-1], ((0, 0), (0, q), (0, 0)))
        return both(fwd, rev)

    def f_rows(x):
        return both(x[0][:, :q][:, ::-1], x[1][:, :q])

    def e_rows(x):
        return both(x[0][:, 1:q + 1], x[1][:, 1:q + 1][:, ::-1])

    tk = jnp.stack([lag_rows(tr), lag_rows(ti)], axis=1)
    ta = jnp.stack([f_rows(tr), f_rows(ti), e_rows(tr), e_rows(ti)], axis=1)
    cq = jnp.stack([both(c_re[0], c_re[1]), both(c_im[0], c_im[1])], axis=1).astype(F32)
    bbr_t = jnp.swapaxes(bbr, 2, 3)
    bbi_t = jnp.swapaxes(bbi, 2, 3)
    bq = jnp.stack([both(bbr_t[0], bbr_t[1]), both(bbi_t[0], bbi_t[1])], axis=1)
    bm = jnp.concatenate([bq[:, 0], bq[:, 1]], axis=-1)

    steps = (q * 2.0 ** jnp.arange(n_steps, dtype=F32))[:, None, None, None]
    smag = jnp.exp(zr[None] * steps)
    sang = zi[None] * steps
    aq = jnp.stack([both(smag[:, 0] * jnp.cos(sang[:, 0]), smag[:, 1] * jnp.cos(sang[:, 1])),
                    both(smag[:, 0] * jnp.sin(sang[:, 0]), smag[:, 1] * jnp.sin(sang[:, 1]))], axis=2)
    return tk, ta, cq, bq, bm, aq.transpose(1, 0, 2, 3)


def _s5_ops_kernel(tk_ref, ta_ref, cq_ref, bq_ref, bm_ref, m_ref, f_ref, et_ref):
    w = V7X_LANES
    hh = S5_GROUP
    q = S5_CHUNK

    def outer(t, c):
        return (c[:, None, :] * t[None, :, :]).reshape(hh * t.shape[0], w)

    cre, cim = cq_ref[0, 0], cq_ref[0, 1]
    bre, bim = bq_ref[0, 0], bq_ref[0, 1]
    tre, tim = tk_ref[0, 0], tk_ref[0, 1]
    w_re = outer(tre, cre) - outer(tim, cim)
    w_im = outer(tre, cim) + outer(tim, cre)
    clt = jnp.concatenate([w_re, -w_im], axis=1)
    bk = lax.dot_general(bm_ref[0], clt, (((1,), (1,)), ((), ())), precision=lax.Precision.HIGHEST,
                         preferred_element_type=F32)
    lo = lax.broadcasted_iota(jnp.int32, (q, w), 1) < q
    for hi in range(hh):
        for j in range(hh // 2):
            ka = jnp.broadcast_to(bk[hi:hi + 1, (2 * j) * w:(2 * j + 1) * w], (q, w))
            kb = jnp.broadcast_to(bk[hi:hi + 1, (2 * j + 1) * w:(2 * j + 2) * w], (q, w))
            tile = jnp.where(lo, pltpu.roll(ka, w - (q - 1), 1, stride=1, stride_axis=0),
                             pltpu.roll(kb, 1, 1, stride=1, stride_axis=0))
            m_ref[0, hi * q:(hi + 1) * q, j * w:(j + 1) * w] = tile.astype(BF16)

    fr, fi, er, ei = ta_ref[0, 0], ta_ref[0, 1], ta_ref[0, 2], ta_ref[0, 3]
    f_ref[0, :, 0:w] = (outer(fr, bre) - outer(fi, bim)).astype(BF16)
    f_ref[0, :, w:2 * w] = (outer(fr, bim) + outer(fi, bre)).astype(BF16)
    et_ref[0, :, 0:w] = (outer(er, cre) - outer(ei, cim)).astype(BF16)
    et_ref[0, :, w:2 * w] = (-(outer(er, cim) + outer(ei, cre))).astype(BF16)


def _s5_ops(tk, ta, cq, bq, bm):
    g = tk.shape[0]
    p2 = 2 * S5_STATE
    cw = S5_CHUNK_WIDTH

    def spec(a):
        return pl.BlockSpec((1,) + a.shape[1:], lambda i: (i,) + (0,) * (a.ndim - 1))

    return pl.pallas_call(
        _s5_ops_kernel,
        grid=(g,),
        in_specs=[spec(tk), spec(ta), spec(cq), spec(bq), spec(bm)],
        out_specs=[
            pl.BlockSpec((1, cw, cw), lambda i: (i, 0, 0)),
            pl.BlockSpec((1, cw, 2 * p2), lambda i: (i, 0, 0)),
            pl.BlockSpec((1, cw, 2 * p2), lambda i: (i, 0, 0)),
        ],
        out_shape=[
            jax.ShapeDtypeStruct((g, cw, cw), BF16),
            jax.ShapeDtypeStruct((g, cw, 2 * p2), BF16),
            jax.ShapeDtypeStruct((g, cw, 2 * p2), BF16),
        ],
        compiler_params=_cparams(1, 32),
        name="s5_ops",
    )(tk, ta, cq, bq, bm)


def _s5_chunk_core_kernel(u_ref, m_ref, f_ref, et_ref, a_ref, y_ref, sa_ref, sb_ref, *, n_chunks, n_steps):
    w = V7X_LANES
    n = n_chunks

    @pl.when(pl.program_id(0) == 0)
    def _():
        sa_ref[...] = jnp.zeros(sa_ref.shape, F32)
        sb_ref[...] = jnp.zeros(sb_ref.shape, F32)

    fwd = lax.broadcasted_iota(jnp.int32, (n, w), 1) < S5_STATE
    u = _from_row_tiles(u_ref.at[0], n).astype(BF16)
    sa_ref[n:2 * n, :] = _dot(u, f_ref[0])

    def neighbours(ref, s, lanes):
        return jnp.where(fwd, ref[n - s:2 * n - s, lanes], ref[n + s:2 * n + s, lanes])

    re, im = slice(0, w), slice(w, 2 * w)
    src, dst = sa_ref, sb_ref
    for k in range(n_steps):
        s = 1 << k
        a = a_ref[0, k]
        ar, ai = a[0:1], a[1:2]
        pr = neighbours(src, s, re)
        pi = neighbours(src, s, im)
        dst[n:2 * n, re] = src[n:2 * n, re] + ar * pr - ai * pi
        dst[n:2 * n, im] = src[n:2 * n, im] + ar * pi + ai * pr
        src, dst = dst, src
    s_in = jnp.concatenate([neighbours(src, 1, re), neighbours(src, 1, im)], axis=1).astype(BF16)
    y = _dot(u, m_ref[0]) + lax.dot_general(s_in, et_ref[0], (((1,), (1,)), ((), ())),
                                            preferred_element_type=F32)
    _to_row_tiles(y, y_ref.at[0])


def _s5_chunk_core(ug, mg, fg, etg, aq):
    g = ug.shape[0]
    n_chunks = ug.shape[1] // V7X_SUBLANES
    cw = S5_CHUNK_WIDTH
    n_steps = aq.shape[1]
    assert (1 << n_steps) == n_chunks
    sw = 2 * V7X_LANES
    return pl.pallas_call(
        functools.partial(_s5_chunk_core_kernel, n_chunks=n_chunks, n_steps=n_steps),
        grid=(g,),
        in_specs=[
            pl.BlockSpec((1, V7X_SUBLANES * n_chunks, V7X_LANES), lambda i: (i, 0, 0)),
            pl.BlockSpec((1, cw, cw), lambda i: (i, 0, 0)),
            pl.BlockSpec((1, cw, sw), lambda i: (i, 0, 0)),
            pl.BlockSpec((1, cw, sw), lambda i: (i, 0, 0)),
            pl.BlockSpec((1, n_steps, 2, V7X_LANES), lambda i: (i, 0, 0, 0)),
        ],
        out_specs=pl.BlockSpec((1, V7X_SUBLANES * n_chunks, V7X_LANES), lambda i: (i, 0, 0)),
        out_shape=jax.ShapeDtypeStruct((g, V7X_SUBLANES * n_chunks, V7X_LANES), F32),
        scratch_shapes=[pltpu.VMEM((3 * n_chunks, sw), F32), pltpu.VMEM((3 * n_chunks, sw), F32)],
        compiler_params=_cparams(1, 32),
        name="s5_core",
    )(ug, mg, fg, etg, aq)


S5_RELAYOUT_TOKENS = 1024
S5_RELAYOUT_PITCH = 136


def _s5_to_groups_kernel(u_ref, o_ref, xt_ref):
    w = V7X_LANES
    q = S5_CHUNK
    lo = lax.broadcasted_iota(jnp.int32, (8, w), 1) < q
    n_blk = S5_RELAYOUT_TOKENS // w
    pitch = S5_RELAYOUT_PITCH
    for sg in range(u_ref.shape[1] // w):
        for b in range(n_blk):
            xt_ref[b * pitch:b * pitch + w, :] = u_ref[b * w:(b + 1) * w, sg * w:(sg + 1) * w].T
        for gp in range(w // S5_GROUP):
            for hp in range(S5_GROUP // 2):
                a0 = xt_ref[pl.ds(gp * S5_GROUP + 2 * hp, n_blk, stride=pitch), :]
                a1 = xt_ref[pl.ds(gp * S5_GROUP + 2 * hp + 1, n_blk, stride=pitch), :]
                even = jnp.where(lo, a0, pltpu.roll(a1, q, 1))
                odd = jnp.where(lo, pltpu.roll(a0, q, 1), a1)
                g = sg * (w // S5_GROUP) + gp
                o_ref[g, pl.ds(hp, n_blk, stride=2 * V7X_SUBLANES), :] = even
                o_ref[g, pl.ds(V7X_SUBLANES + hp, n_blk, stride=2 * V7X_SUBLANES), :] = odd


def _s5_to_tokens_kernel(y_ref, o_ref, yt_ref):
    w = V7X_LANES
    q = S5_CHUNK
    lo = lax.broadcasted_iota(jnp.int32, (8, w), 1) < q
    n_blk = S5_RELAYOUT_TOKENS // w
    pitch = S5_RELAYOUT_PITCH
    for sg in range(o_ref.shape[1] // w):
        for gp in range(w // S5_GROUP):
            for hp in range(S5_GROUP // 2):
                g = sg * (w // S5_GROUP) + gp
                even = y_ref[g, pl.ds(hp, n_blk, stride=2 * V7X_SUBLANES), :]
                odd = y_ref[g, pl.ds(V7X_SUBLANES + hp, n_blk, stride=2 * V7X_SUBLANES), :]
                yt_ref[pl.ds(gp * S5_GROUP + 2 * hp, n_blk, stride=pitch), :] = jnp.where(lo, even, pltpu.roll(odd, q, 1))
                yt_ref[pl.ds(gp * S5_GROUP + 2 * hp + 1, n_blk, stride=pitch), :] = jnp.where(lo, pltpu.roll(even, q, 1), odd)
        for b in range(n_blk):
            o_ref[b * w:(b + 1) * w, sg * w:(sg + 1) * w] = yt_ref[b * pitch:b * pitch + w, :].T


def _s5_in_kernel(x_ref, g_ref, w_ref, u_ref, ug_ref, wb_ref, xt_ref):
    _cast_weight_once(w_ref, wb_ref)
    h = _rms(x_ref[...], g_ref[...]).astype(BF16)
    u_ref[...] = _dot(h, wb_ref[...])
    _s5_to_groups_kernel(u_ref, ug_ref, xt_ref)


def _s5_in(x, g, w_stack, layer):
    t, d = x.shape
    n = w_stack.shape[2]
    tt = S5_RELAYOUT_TOKENS
    rows = V7X_SUBLANES * tt // S5_CHUNK
    return pl.pallas_call(
        _s5_in_kernel,
        grid=(t // tt,),
        in_specs=[
            pl.BlockSpec((tt, d), lambda i: (i, 0)),
            pl.BlockSpec((1, d), lambda i: (0, 0)),
            _layer_weight_spec(w_stack, layer),
        ],
        out_specs=[
            pl.BlockSpec((tt, n), lambda i: (i, 0)),
            pl.BlockSpec((S5_GROUPS, rows, V7X_LANES), lambda i: (0, i, 0)),
        ],
        out_shape=[
            jax.ShapeDtypeStruct((t, n), F32),
            jax.ShapeDtypeStruct((S5_GROUPS, V7X_SUBLANES * t // S5_CHUNK, V7X_LANES), F32),
        ],
        scratch_shapes=[pltpu.VMEM((d, n), BF16),
                        pltpu.VMEM((tt // V7X_LANES * S5_RELAYOUT_PITCH, V7X_LANES), F32)],
        compiler_params=_cparams(1, 40),
        name="s5_in",
    )(x, g.reshape(1, d), w_stack)


def _s5_out_kernel(x_ref, yg_ref, u_ref, d_ref, w_ref, rg_ref, rw_ref, rb_ref, tri_ref,
                   o_ref, ids_ref, wts_ref, cnt_ref, wb_ref, y_ref, yt_ref):
    _cast_weight_once(w_ref, wb_ref)
    _s5_to_tokens_kernel(yg_ref, y_ref, yt_ref)
    y = y_ref[...] + d_ref[...] * u_ref[...]
    act = jax.nn.gelu(y).astype(BF16)
    vg = _dot(act, wb_ref[...])
    x1 = x_ref[...] + vg[:, :D_MODEL] * jax.nn.sigmoid(vg[:, D_MODEL:])
    o_ref[...] = x1
    _route_tile(x1, rg_ref, rw_ref, rb_ref, tri_ref, ids_ref, wts_ref, cnt_ref)


def _s5_out(x, yg, u, d_skip, w_out_stack, layer, router):
    t, d = x.shape
    sw = u.shape[1]
    tt = S5_RELAYOUT_TOKENS
    rows = V7X_SUBLANES * tt // S5_CHUNK
    r_ops, r_specs = _router_operands(*router, tt)
    ro_specs, ro_shapes = _router_outputs(t, tt)
    return pl.pallas_call(
        _s5_out_kernel,
        grid=(t // tt,),
        in_specs=[
            pl.BlockSpec((tt, d), lambda i: (i, 0)),
            pl.BlockSpec((S5_GROUPS, rows, V7X_LANES), lambda i: (0, i, 0)),
            pl.BlockSpec((tt, sw), lambda i: (i, 0)),
            pl.BlockSpec((1, sw), lambda i: (0, 0)),
            _layer_weight_spec(w_out_stack, layer),
        ] + r_specs,
        out_specs=[pl.BlockSpec((tt, d), lambda i: (i, 0))] + ro_specs,
        out_shape=[jax.ShapeDtypeStruct((t, d), F32)] + ro_shapes,
        scratch_shapes=[pltpu.VMEM((sw, 2 * d), BF16), pltpu.VMEM((tt, sw), F32),
                        pltpu.VMEM((tt // V7X_LANES * S5_RELAYOUT_PITCH, V7X_LANES), F32)],
        compiler_params=_cparams(1, 56),
        name="s5_out",
    )(x, yg, u, d_skip.reshape(1, sw).astype(F32), w_out_stack, *r_ops)


def _s5_layer(x, norm_g, layer, w_in_stack, lam_re, lam_im, log_dt, b_re, b_im, c_re, c_im, d_skip, w_out_stack,
              router):
    t = x.shape[0]
    n_chunks = t // S5_CHUNK
    u, ug = _s5_in(x, norm_g, w_in_stack, layer)
    n_steps = n_chunks.bit_length() - 1
    assert (1 << n_steps) == n_chunks
    tk, ta, cq, bq, bm, aq = _s5_param_tables(lam_re, lam_im, log_dt, b_re, b_im, c_re, c_im, n_steps)
    mg, fg, etg = _s5_ops(tk, ta, cq, bq, bm)
    yg = _s5_chunk_core(ug, mg, fg, etg, aq)
    return _s5_out(x, yg, u, d_skip, w_out_stack, layer, router)


def _qkv_kernel(x_ref, g_ref, w_ref, qg_ref, kg_ref, q_ref, k_ref, v_ref, wb_ref):
    _cast_weight_once(w_ref, wb_ref)
    h = _rms(x_ref[...], g_ref[...]).astype(BF16)
    qkv = _dot(h, wb_ref[...])
    tm = qkv.shape[0]
    lo = lax.broadcasted_iota(jnp.int32, (tm, V7X_LANES), 1) < HEAD_DIM
    scale = 1.0 / math.sqrt(HEAD_DIM)

    def headnorm(xt, gt):
        sq = xt * xt
        s0 = jnp.sum(jnp.where(lo, sq, 0.0), axis=-1, keepdims=True)
        s1 = jnp.sum(jnp.where(lo, 0.0, sq), axis=-1, keepdims=True)
        rs = jnp.where(lo, lax.rsqrt(s0 / HEAD_DIM + RMS_EPS), lax.rsqrt(s1 / HEAD_DIM + RMS_EPS))
        return xt * rs * gt

    for t in range(D_MODEL // V7X_LANES):
        lanes = slice(t * V7X_LANES, (t + 1) * V7X_LANES)
        q_ref[:, lanes] = (headnorm(qkv[:, lanes], qg_ref[...]) * scale).astype(BF16)
        klanes = slice(D_MODEL + t * V7X_LANES, D_MODEL + (t + 1) * V7X_LANES)
        k_ref[:, lanes] = headnorm(qkv[:, klanes], kg_ref[...]).astype(BF16)
    v_ref[...] = qkv[:, 2 * D_MODEL:].astype(BF16)


def _qkv(x, g, w_qkv_stack, layer, q_norm, k_norm, tm=512):
    t, d = x.shape
    qg = jnp.tile(q_norm.astype(F32), 2).reshape(1, V7X_LANES)
    kg = jnp.tile(k_norm.astype(F32), 2).reshape(1, V7X_LANES)
    out = jax.ShapeDtypeStruct((t, d), BF16)
    ospec = pl.BlockSpec((tm, d), lambda i: (i, 0))
    return pl.pallas_call(
        _qkv_kernel,
        grid=(t // tm,),
        in_specs=[
            pl.BlockSpec((tm, d), lambda i: (i, 0)),
            pl.BlockSpec((1, d), lambda i: (0, 0)),
            _layer_weight_spec(w_qkv_stack, layer),
            pl.BlockSpec((1, V7X_LANES), lambda i: (0, 0)),
            pl.BlockSpec((1, V7X_LANES), lambda i: (0, 0)),
        ],
        out_specs=[ospec, ospec, ospec],
        out_shape=[out, out, out],
        scratch_shapes=[pltpu.VMEM((d, 3 * d), BF16)],
        compiler_params=_cparams(1, 52),
        name="nat_qkv",
    )(x, g.reshape(1, d), w_qkv_stack, qg, kg)


def _nat_bias_table(rpb_ref, b_ref):
    w = V7X_LANES
    c = lax.broadcasted_iota(jnp.int32, (GRID_W, w), 0)
    lane = lax.broadcasted_iota(jnp.int32, (GRID_W, w), 1)
    lo = lane < GRID_W
    kc = jnp.where(lo, lane, lane - GRID_W)
    ws = jnp.clip(c - WIN_COLS // 2, 0, GRID_W - WIN_COLS)
    valid = (kc >= ws) & (kc < ws + WIN_COLS)
    n_ri = 2 * WIN_ROWS - 1
    for h in range(2):
        t_lo, t_hi = [], []
        for ri in range(n_ri):
            vb = jnp.broadcast_to(rpb_ref[h, ri:ri + 1, :], (GRID_W, w))
            t_lo.append(pltpu.roll(vb, w - (WIN_COLS - 1), 1, stride=1, stride_axis=0))
            t_hi.append(pltpu.roll(vb, GRID_W - (WIN_COLS - 1), 1, stride=1, stride_axis=0))
        for o in range(WIN_ROWS):
            for j in range(WIN_ROWS // 2):
                tile = jnp.where(lo, t_lo[o + 2 * j], t_hi[o + 2 * j + 1])
                b_ref[h, o, :, j * w:(j + 1) * w] = jnp.where(valid, tile, MASK_VALUE)


def _nat_attn_kernel(q_ref, k_ref, v_ref, rpb_ref, o_ref, b_ref, *, rows, unroll):
    lo = lax.broadcasted_iota(jnp.int32, (GRID_W, V7X_LANES), 1) < HEAD_DIM
    head_mask = (jnp.where(lo, 1.0, 0.0), jnp.where(lo, 0.0, 1.0))
    nkeys = WIN_ROWS * GRID_W
    _nat_bias_table(rpb_ref, b_ref)

    def body(rb, carry):
        chains = []
        for u in range(unroll):
            r = rb * unroll + u
            rs = jnp.clip(r - WIN_ROWS // 2, 0, rows - WIN_ROWS)
            off = rs - r + (WIN_ROWS - 1)
            q = q_ref[pl.ds(pl.multiple_of(r * GRID_W, GRID_W), GRID_W), :].astype(F32)
            k = k_ref[pl.ds(pl.multiple_of(rs * GRID_W, GRID_W), nkeys), :]
            for h in range(2):
                chains.append(dict(r=r, rs=rs, off=off, h=h, q=(q * head_mask[h]).astype(BF16), k=k))
        for c in chains:
            c["s"] = lax.dot_general(c["q"], c["k"], (((1,), (1,)), ((), ())), preferred_element_type=F32)
        for c in chains:
            c["s"] = c["s"] + b_ref[c["h"], c["off"]]
        for c in chains:
            c["m"] = jnp.max(c["s"], axis=-1, keepdims=True)
        for c in chains:
            c["p"] = jnp.exp(c["s"] - c["m"])
        for c in chains:
            c["l"] = jnp.sum(c["p"], axis=-1, keepdims=True)
        for c in chains:
            v = v_ref[pl.ds(pl.multiple_of(c["rs"] * GRID_W, GRID_W), nkeys), :]
            c["o"] = _dot(c["p"].astype(BF16), v) / c["l"]
        for u in range(unroll):
            c0, c1 = chains[2 * u], chains[2 * u + 1]
            o = jnp.where(lo, c0["o"], c1["o"])
            o_ref[pl.ds(pl.multiple_of(c0["r"] * GRID_W, GRID_W), GRID_W), :] = o.astype(BF16)
        return carry

    lax.fori_loop(0, rows // unroll, body, 0)


def _nat_attn(q, k, v, rpb):
    t, d = q.shape
    rows = t // GRID_W
    assert rows >= WIN_ROWS
    unroll = math.gcd(rows, NAT_ROW_UNROLL)
    n_ri, n_ci = rpb.shape[1], rpb.shape[2]
    rpb_pad = jnp.pad(rpb.astype(F32), ((0, 0), (0, 2 * WIN_ROWS - n_ri), (0, V7X_LANES - n_ci)))
    spec = pl.BlockSpec((t, V7X_LANES), lambda i: (0, i))
    return pl.pallas_call(
        functools.partial(_nat_attn_kernel, rows=rows, unroll=unroll),
        grid=(d // V7X_LANES,),
        in_specs=[spec, spec, spec,
                  pl.BlockSpec((2, 2 * WIN_ROWS, V7X_LANES), lambda i: (i, 0, 0))],
        out_specs=spec,
        out_shape=jax.ShapeDtypeStruct((t, d), BF16),
        scratch_shapes=[pltpu.VMEM((2, WIN_ROWS, GRID_W, WIN_ROWS * GRID_W), F32)],
        compiler_params=_cparams(1, 48),
        name="nat_attn",
    )(q, k, v, rpb_pad)


def _matmul_residual_kernel(x_ref, a_ref, w_ref, rg_ref, rw_ref, rb_ref, tri_ref,
                            o_ref, ids_ref, wts_ref, cnt_ref, wb_ref):
    _cast_weight_once(w_ref, wb_ref)
    x1 = x_ref[...] + _dot(a_ref[...], wb_ref[...])
    o_ref[...] = x1
    _route_tile(x1, rg_ref, rw_ref, rb_ref, tri_ref, ids_ref, wts_ref, cnt_ref)


def _matmul_residual(x, a, w_stack, layer, router, tm=512):
    t, d = x.shape
    kdim = a.shape[1]
    r_ops, r_specs = _router_operands(*router, tm)
    ro_specs, ro_shapes = _router_outputs(t, tm)
    return pl.pallas_call(
        _matmul_residual_kernel,
        grid=(t // tm,),
        in_specs=[
            pl.BlockSpec((tm, d), lambda i: (i, 0)),
            pl.BlockSpec((tm, kdim), lambda i: (i, 0)),
            _layer_weight_spec(w_stack, layer),
        ] + r_specs,
        out_specs=[pl.BlockSpec((tm, d), lambda i: (i, 0))] + ro_specs,
        out_shape=[jax.ShapeDtypeStruct((t, d), F32)] + ro_shapes,
        scratch_shapes=[pltpu.VMEM((kdim, d), BF16)],
        compiler_params=_cparams(1, 32),
        name="matmul_residual",
    )(x, a, w_stack, *r_ops)


def _nat_layer(x, norm_g, layer, w_qkv_stack, q_norm, k_norm, rpb, w_o_stack, router):
    q, k, v = _qkv(x, norm_g, w_qkv_stack, layer, q_norm, k_norm)
    o = _nat_attn(q, k, v, rpb)
    return _matmul_residual(x, o, w_o_stack, layer, router)


def _route_tile(x, g_ref, w_ref, b_ref, tri_ref, ids_ref, wts_ref, cnt_ref):
    @pl.when(pl.program_id(0) == 0)
    def _():
        cnt_ref[...] = jnp.zeros(cnt_ref.shape, F32)

    def split(a):
        hi = a.astype(BF16)
        return hi, (a - hi.astype(F32)).astype(BF16)

    def dot_nt(a, b):
        return lax.dot_general(a, b, (((1,), (1,)), ((), ())), preferred_element_type=F32)

    h_hi, h_lo = split(_rms(x, g_ref[...]))
    w_hi, w_lo = split(w_ref[...])
    lg = dot_nt(w_hi, h_hi) + (dot_nt(w_hi, h_lo) + dot_nt(w_lo, h_hi)) + b_ref[...]
    row = lax.broadcasted_iota(jnp.int32, lg.shape, 0)
    big = 4 * V7X_LANES
    is_g = (row >= N_EXPERTS) & (row < N_EXPERTS + N_GROUPS)
    gmax = jnp.max(jnp.where(is_g, lg, MASK_VALUE), axis=0, keepdims=True)
    gsum = jnp.sum(jnp.where(is_g, jnp.exp(lg - gmax), 0.0), axis=0, keepdims=True)
    g_val = 1.0 / gsum
    g_idx = jnp.min(jnp.where(is_g & (lg == gmax), row, big), axis=0, keepdims=True) - N_EXPERTS
    in_grp = (row < N_EXPERTS) & ((row // EXPERTS_PER_GROUP) == g_idx)
    m1 = jnp.max(jnp.where(in_grp, lg, MASK_VALUE), axis=0, keepdims=True)
    i1 = jnp.min(jnp.where(in_grp & (lg == m1), row, big), axis=0, keepdims=True)
    rest = in_grp & (row != i1)
    m2 = jnp.max(jnp.where(rest, lg, MASK_VALUE), axis=0, keepdims=True)
    i2 = jnp.min(jnp.where(rest & (lg == m2), row, big), axis=0, keepdims=True)
    z = jnp.sum(jnp.where(in_grp, jnp.exp(lg - m1), 0.0), axis=0, keepdims=True)
    p1 = 1.0 / z
    p2 = jnp.exp(m2 - m1) / z
    den = p1 + p2
    w1 = g_val * (p1 / den)
    w2 = g_val * (p2 / den)

    sel1 = (row == i1)[:N_EXPERTS]
    sel2 = (row == i2)[:N_EXPERTS]
    onehot = jnp.where(sel1 | sel2, 1.0, 0.0)
    before = _dot(onehot.astype(BF16), tri_ref[...]) + cnt_ref[:, 0:1]
    r1 = jnp.sum(jnp.where(sel1, before, 0.0), axis=0, keepdims=True)
    r2 = jnp.sum(jnp.where(sel2, before, 0.0), axis=0, keepdims=True)
    cnt_ref[...] = cnt_ref[...] + jnp.sum(onehot, axis=1, keepdims=True)

    orow = lax.broadcasted_iota(jnp.int32, ids_ref.shape, 0)
    ids_ref[...] = jnp.where(orow == 0, i1, jnp.where(orow == 1, i2, jnp.where(
        orow == 2, r1.astype(jnp.int32), jnp.where(orow == 3, r2.astype(jnp.int32), 0))))
    wts_ref[...] = jnp.where(orow == 0, w1, jnp.where(orow == 1, w2, 0.0))


def _router_operands(g, w_group, b_group, w_expert, b_expert, tm):
    d = g.shape[0]
    rows = MOE_ROUTER_ROWS
    pad = rows - N_EXPERTS - N_GROUPS
    w = jnp.pad(jnp.concatenate([w_expert, w_group], axis=1).astype(F32).T, ((0, pad), (0, 0)))
    b = jnp.pad(jnp.concatenate([b_expert, b_group]).astype(F32), (0, pad)).reshape(rows, 1)
    tri = (jnp.arange(tm)[:, None] < jnp.arange(tm)[None, :]).astype(BF16)
    specs = [
        pl.BlockSpec((1, d), lambda i: (0, 0)),
        pl.BlockSpec((rows, d), lambda i: (0, 0)),
        pl.BlockSpec((rows, 1), lambda i: (0, 0)),
        pl.BlockSpec((tm, tm), lambda i: (0, 0)),
    ]
    return [g.reshape(1, d).astype(F32), w, b, tri], specs


def _router_outputs(t, tm):
    specs = [
        pl.BlockSpec((V7X_SUBLANES, tm), lambda i: (0, i)),
        pl.BlockSpec((V7X_SUBLANES, tm), lambda i: (0, i)),
        pl.BlockSpec((N_EXPERTS, V7X_LANES), lambda i: (0, 0)),
    ]
    shapes = [
        jax.ShapeDtypeStruct((V7X_SUBLANES, t), jnp.int32),
        jax.ShapeDtypeStruct((V7X_SUBLANES, t), F32),
        jax.ShapeDtypeStruct((N_EXPERTS, V7X_LANES), F32),
    ]
    return specs, shapes


def _moe_tables(cnt, t):
    tm = MOE_ROW_TILE
    n_rows = 2 * t
    counts = cnt[:, 0].astype(jnp.int32)
    ends = jnp.cumsum(counts)
    starts = ends - counts
    brk = jnp.concatenate([jnp.arange(n_rows // tm, dtype=jnp.int32) * tm, starts])
    idx = jnp.arange(brk.shape[0])
    before = (brk[None, :] < brk[:, None]) | ((brk[None, :] == brk[:, None]) & (idx[None, :] < idx[:, None]))
    rank = jnp.sum(before.astype(jnp.int32), axis=1)
    lo = jnp.sum(jnp.where(rank[None, :] == idx[:, None], brk[None, :], 0), axis=1)
    hi = jnp.concatenate([lo[1:], jnp.full((1,), n_rows, jnp.int32)])
    anchor = jnp.minimum(lo, n_rows - 1)
    tile = anchor // tm
    expert = jnp.minimum(jnp.sum((ends[None, :] <= anchor[:, None]).astype(jnp.int32), axis=1), N_EXPERTS - 1)
    lo_in = lo - tile * tm
    hi_in = hi - tile * tm
    first = ((hi > lo) & (lo_in == 0)).astype(jnp.int32)
    last = ((hi > lo) & (hi_in == tm)).astype(jnp.int32)
    newexp = jnp.concatenate([jnp.ones((1,), jnp.int32), (expert[1:] != expert[:-1]).astype(jnp.int32)])
    table = jnp.stack([tile, expert, lo_in, hi_in, first, last, newexp]).astype(jnp.int32)
    return starts, table


def _moe_pos_kernel(starts_ref, ids_ref, pos_ref):
    e = ids_ref[0:2, :]
    acc = jnp.zeros(e.shape, jnp.int32)
    for ex in range(N_EXPERTS):
        acc = jnp.where(e == ex, starts_ref[ex], acc)
    pos_ref[0:2, :] = acc + ids_ref[2:4, :]
    pos_ref[2:8, :] = jnp.zeros((6, e.shape[1]), jnp.int32)


def _moe_positions(ids, starts, tm=2048):
    t = ids.shape[1]
    tm = min(tm, t)
    grid_spec = pltpu.PrefetchScalarGridSpec(
        num_scalar_prefetch=1,
        grid=(t // tm,),
        in_specs=[pl.BlockSpec((8, tm), lambda i, st: (0, i))],
        out_specs=pl.BlockSpec((8, tm), lambda i, st: (0, i)),
    )
    return pl.pallas_call(
        _moe_pos_kernel,
        grid_spec=grid_spec,
        out_shape=jax.ShapeDtypeStruct((8, t), jnp.int32),
        compiler_params=_cparams(1, 16),
        name="moe_positions",
    )(starts, ids)


def _tile_positions(pos, tm):
    return pos.reshape(2, -1, tm).transpose(1, 0, 2)


def _to_row_tiles(x, dst_ref):
    rows = x.shape[0]
    for c in range(x.shape[1] // V7X_LANES):
        dst_ref[pl.ds(c, rows, stride=V7X_SUBLANES), :] = x[:, c * V7X_LANES:(c + 1) * V7X_LANES]


def _from_row_tiles(src_ref, rows, base=0):
    return jnp.concatenate([src_ref[pl.ds(base + c, rows, stride=V7X_SUBLANES), :] for c in range(V7X_SUBLANES)], axis=1)


def _row_tile(ref, r):
    return ref.at[pl.ds(pl.multiple_of(r * V7X_SUBLANES, V7X_SUBLANES), V7X_SUBLANES), :]


def _moe_dispatch_kernel(pos_ref, x_ref, g_ref, xs_ref, xt_ref, sem):
    tm = x_ref.shape[0]
    _to_row_tiles(_rms(x_ref[...], g_ref[...]), xt_ref)
    for j in range(tm):
        for k in range(2):
            pltpu.make_async_copy(_row_tile(xt_ref, j), _row_tile(xs_ref, pos_ref[0, k, j]), sem).start(priority=k)
    for k in range(2):
        pltpu.make_async_copy(xt_ref, xs_ref.at[pl.ds(0, V7X_SUBLANES * tm), :], sem).wait()


def _moe_dispatch(x, g, pos, tm=MOE_TOKEN_TILE):
    t, d = x.shape
    assert d == V7X_SUBLANES * V7X_LANES
    return pl.pallas_call(
        _moe_dispatch_kernel,
        grid=(t // tm,),
        in_specs=[
            pl.BlockSpec((1, 2, tm), lambda i: (i, 0, 0), memory_space=pltpu.SMEM),
            pl.BlockSpec((tm, d), lambda i: (i, 0)),
            pl.BlockSpec((1, d), lambda i: (0, 0)),
        ],
        out_specs=pl.BlockSpec(memory_space=pl.ANY),
        out_shape=jax.ShapeDtypeStruct((2 * t * V7X_SUBLANES, V7X_LANES), F32),
        scratch_shapes=[pltpu.VMEM((V7X_SUBLANES * tm, V7X_LANES), F32), pltpu.SemaphoreType.DMA(())],
        compiler_params=_cparams(1, 32),
        name="moe_dispatch",
    )(_tile_positions(pos, tm), x, g.reshape(1, d))


def _moe_ffn_kernel(tab_ref, xs_ref, wg_ref, wu_ref, wd_ref, ys_ref, wgb_ref, wub_ref, wdb_ref, acc_ref):
    i = pl.program_id(0)
    lo = tab_ref[2, i]
    hi = tab_ref[3, i]
    first = tab_ref[4, i] == 1
    last = tab_ref[5, i] == 1
    tm = acc_ref.shape[0]

    @pl.when(tab_ref[6, i] == 1)
    def _():
        wgb_ref[...] = wg_ref[0, 0, 0].astype(BF16)
        wub_ref[...] = wu_ref[0, 0, 0].astype(BF16)
        wdb_ref[...] = wd_ref[0, 0, 0].astype(BF16)

    @pl.when(hi > lo)
    def _():
        h = _from_row_tiles(xs_ref, tm).astype(BF16)
        a = _dot(h, wgb_ref[...])
        u = _dot(h, wub_ref[...])
        rowid = lax.broadcasted_iota(jnp.int32, a.shape, 0)
        act = jnp.where((rowid >= lo) & (rowid < hi), jax.nn.silu(a) * u, 0.0).astype(BF16)
        y = _dot(act, wdb_ref[...])

        @pl.when(first & last)
        def _():
            _to_row_tiles(y, ys_ref)

        @pl.when(first & jnp.logical_not(last))
        def _():
            acc_ref[...] = y

        @pl.when(jnp.logical_not(first) & jnp.logical_not(last))
        def _():
            acc_ref[...] += y

        @pl.when(jnp.logical_not(first) & last)
        def _():
            _to_row_tiles(acc_ref[...] + y, ys_ref)


def _moe_ffn(xs, table, w_gate, w_up, w_down, layer):
    d, ff = w_gate.shape[-2:]
    tm = MOE_ROW_TILE
    n_items = table.shape[1]
    epg = EXPERTS_PER_GROUP
    wmap = lambda i, tab: (layer, tab[1, i] // epg, tab[1, i] % epg, 0, 0)
    grid_spec = pltpu.PrefetchScalarGridSpec(
        num_scalar_prefetch=1,
        grid=(n_items,),
        in_specs=[
            pl.BlockSpec((V7X_SUBLANES * tm, V7X_LANES), lambda i, tab: (tab[0, i], 0)),
            pl.BlockSpec((1, 1, 1, d, ff), wmap),
            pl.BlockSpec((1, 1, 1, d, ff), wmap),
            pl.BlockSpec((1, 1, 1, ff, d), wmap),
        ],
        out_specs=pl.BlockSpec((V7X_SUBLANES * tm, V7X_LANES), lambda i, tab: (tab[0, i], 0)),
        scratch_shapes=[pltpu.VMEM((d, ff), BF16), pltpu.VMEM((d, ff), BF16), pltpu.VMEM((ff, d), BF16),
                        pltpu.VMEM((tm, d), F32)],
    )
    return pl.pallas_call(
        _moe_ffn_kernel,
        grid_spec=grid_spec,
        out_shape=jax.ShapeDtypeStruct(xs.shape, F32),
        compiler_params=_cparams(1, 32),
        name="moe_ffn",
    )(table, xs, w_gate, w_up, w_down)


def _combine_ple_kernel(pos_ref, x_ref, wc_ref, ys_ref, g_ref, p_ref, wg_ref, wp_ref, o_ref,
                        ya_ref, yb_ref, wgb_ref, wpb_ref, sems):
    tm = x_ref.shape[0]
    hm = tm // 2
    for j in range(tm):
        sem = sems.at[j // hm]
        pltpu.make_async_copy(_row_tile(ys_ref, pos_ref[0, 0, j]), _row_tile(ya_ref, j), sem).start(priority=0)
        pltpu.make_async_copy(_row_tile(ys_ref, pos_ref[0, 1, j]), _row_tile(yb_ref, j), sem).start(priority=1)
    _cast_weight_once(wg_ref, wgb_ref)
    _cast_weight_once(wp_ref, wpb_ref)
    for half in range(2):
        rows = pl.ds(half * hm, hm)
        tiles = pl.ds(half * V7X_SUBLANES * hm, V7X_SUBLANES * hm)
        proj = _dot(p_ref[0, 0, rows, :].astype(BF16), wpb_ref[...])
        pltpu.make_async_copy(ys_ref.at[pl.ds(0, V7X_SUBLANES * hm), :], ya_ref.at[tiles, :], sems.at[half]).wait()
        pltpu.make_async_copy(ys_ref.at[pl.ds(0, V7X_SUBLANES * hm), :], yb_ref.at[tiles, :], sems.at[half]).wait()
        w = wc_ref[rows, :]
        x2 = (x_ref[rows, :] + w[:, 0:1] * _from_row_tiles(ya_ref, hm, half * V7X_SUBLANES * hm)
              + w[:, 1:2] * _from_row_tiles(yb_ref, hm, half * V7X_SUBLANES * hm))
        h = _rms(x2, g_ref[...]).astype(BF16)
        gate = jax.nn.sigmoid(_dot(h, wgb_ref[...]))
        o_ref[rows, :] = x2 + gate * proj


def _combine_ple(x, pos, wts, ys, norm_g, p_stack, w_proj_stack, w_gate_stack, layer, tm=MOE_TOKEN_TILE):
    t, d = x.shape
    pd = p_stack.shape[-1]
    wc = wts[0:2].T
    return pl.pallas_call(
        _combine_ple_kernel,
        grid=(t // tm,),
        in_specs=[
            pl.BlockSpec((1, 2, tm), lambda i: (i, 0, 0), memory_space=pltpu.SMEM),
            pl.BlockSpec((tm, d), lambda i: (i, 0)),
            pl.BlockSpec((tm, 2), lambda i: (i, 0)),
            pl.BlockSpec(memory_space=pl.ANY),
            pl.BlockSpec((1, d), lambda i: (0, 0)),
            pl.BlockSpec((1, 1, tm, pd), lambda i: (layer, 0, i, 0)),
            _layer_weight_spec(w_gate_stack, layer),
            _layer_weight_spec(w_proj_stack, layer),
        ],
        out_specs=pl.BlockSpec((tm, d), lambda i: (i, 0)),
        out_shape=jax.ShapeDtypeStruct((t, d), F32),
        scratch_shapes=[pltpu.VMEM((V7X_SUBLANES * tm, V7X_LANES), F32), pltpu.VMEM((V7X_SUBLANES * tm, V7X_LANES), F32),
                        pltpu.VMEM((d, d), BF16), pltpu.VMEM((pd, d), BF16), pltpu.SemaphoreType.DMA((2,))],
        compiler_params=_cparams(1, 40),
        name="moe_combine_ple",
    )(_tile_positions(pos, tm), x, wc, ys, norm_g.reshape(1, d), p_stack, w_gate_stack, w_proj_stack)


def _moe_ple_layer(x, routing, layer, norm_ffn, w_gate, w_up, w_down, norm_ple, p_stack, ple_w_proj, ple_w_gate):
    t = x.shape[0]
    ids, wts, cnt = routing
    starts, table = _moe_tables(cnt, t)
    pos = _moe_positions(ids, starts)[0:2]
    xs = _moe_dispatch(x, norm_ffn, pos)
    ys = _moe_ffn(xs, table, w_gate, w_up, w_down, layer)
    return _combine_ple(x, pos, wts, ys, norm_ple, p_stack, ple_w_proj, ple_w_gate, layer)


def kernel(x, p, norm_mix, norm_ffn, norm_ple, s5_w_in, s5_lam_re, s5_lam_im, s5_log_dt, s5_b_re, s5_b_im, s5_c_re, s5_c_im, s5_d, s5_w_out, nat_w_qkv, nat_q_norm, nat_k_norm, nat_rpb, nat_w_o, moe_w_group, moe_b_group, moe_w_expert, moe_b_expert, moe_w_gate, moe_w_up, moe_w_down, ple_w_proj, ple_w_gate):
    bsz, seq, d = x.shape
    depth = p.shape[0]
    assert bsz == 1 and d == D_MODEL
    xs = x.reshape(seq, d).astype(F32)
    for i in range(depth):
        j = i // 2
        router = (norm_ffn[i], moe_w_group[i], moe_b_group[i], moe_w_expert[i], moe_b_expert[i])
        if i % 2 == 0:
            xs, *routing = _s5_layer(xs, norm_mix[i], j, s5_w_in, s5_lam_re[j], s5_lam_im[j], s5_log_dt[j],
                                     s5_b_re[j], s5_b_im[j], s5_c_re[j], s5_c_im[j], s5_d[j], s5_w_out, router)
        else:
            xs, *routing = _nat_layer(xs, norm_mix[i], j, nat_w_qkv, nat_q_norm[j], nat_k_norm[j], nat_rpb[j],
                                      nat_w_o, router)
        xs = _moe_ple_layer(xs, routing, i, norm_ffn[i], moe_w_gate, moe_w_up, moe_w_down,
                            norm_ple[i], p, ple_w_proj, ple_w_gate)
    return xs.reshape(bsz, seq, d).astype(x.dtype)
```

```python
import functools
import math

import jax
import jax.numpy as jnp
from jax import lax
from jax.experimental import pallas as pl
from jax.experimental.pallas import tpu as pltpu

F32 = jnp.float32
BF16 = jnp.bfloat16

D_MODEL = 1024
GRID_W = 64
S5_GROUP = 16
S5_STATE = 64
S5_GROUPS = 32
HEAD_DIM = 64
WIN_ROWS = 8
WIN_COLS = 16
N_GROUPS = 4
EXPERTS_PER_GROUP = 8
N_EXPERTS = N_GROUPS * EXPERTS_PER_GROUP
RMS_EPS = 1e-6
MASK_VALUE = -1e30

V7X_LANES = 128
V7X_SUBLANES = 8
V7X_VMEM_BYTES = 64 * 1024 * 1024

S5_CHUNK = 64
S5_CHUNK_WIDTH = S5_CHUNK * S5_GROUP

NAT_ROW_UNROLL = 32

MOE_ROUTER_ROWS = 40
MOE_ROW_TILE = 512
MOE_TOKEN_TILE = 512
MOE_FFN_RING = 3


def _cparams(n_axes, vmem_mib):
    return pltpu.CompilerParams(
        dimension_semantics=("arbitrary",) * n_axes,
        vmem_limit_bytes=min(vmem_mib * 1024 * 1024, V7X_VMEM_BYTES - 4 * 1024 * 1024),
    )


def _dot(a, b):
    return jnp.dot(a, b, preferred_element_type=F32)


def _rms(x, g):
    ms = jnp.mean(x * x, axis=-1, keepdims=True)
    return x * lax.rsqrt(ms + RMS_EPS) * g


def _layer_weight_spec(w_stack, layer):
    _, k, n = w_stack.shape
    return pl.BlockSpec((1, k, n), lambda i: (layer, 0, 0), pipeline_mode=pl.Buffered(1))


def _cast_weight_once(w_ref, wb_ref):
    @pl.when(pl.program_id(0) == 0)
    def _():
        wb_ref[...] = w_ref[0].astype(BF16)


def _s5_param_tables(lam_re, lam_im, log_dt, b_re, b_im, c_re, c_im, n_steps):
    q = S5_CHUNK
    g, p = S5_GROUPS, S5_STATE
    dt = jnp.exp(log_dt.astype(F32))[..., None]
    lam_re = lam_re.astype(F32)
    lam_im = lam_im.astype(F32)
    zr = lam_re * dt
    zi = lam_im * dt
    k = jnp.arange(q + 1, dtype=F32)[:, None]
    mag = jnp.exp(zr[:, :, None, :] * k)
    ang = zi[:, :, None, :] * k
    tr = mag * jnp.cos(ang)
    ti = mag * jnp.sin(ang)
    nr = tr[:, :, 1] - 1.0
    ni = ti[:, :, 1]
    den = lam_re * lam_re + lam_im * lam_im
    cr = (nr * lam_re + ni * lam_im) / den
    ci = (ni * lam_re - nr * lam_im) / den
    b_re = b_re.astype(F32)
    b_im = b_im.astype(F32)
    bbr = cr[..., None] * b_re - ci[..., None] * b_im
    bbi = cr[..., None] * b_im + ci[..., None] * b_re

    def both(fwd, rev):
        return jnp.concatenate([fwd, rev], axis=-1)

    def lag_rows(x):
        fwd = jnp.pad(x[0][:, :q], ((0, 0), (q - 1, 1), (0, 0)))
        rev = jnp.pad(x[1][:, :q][:, ::-1], ((0, 0), (0, q), (0, 0)))
        return both(fwd, rev)

    def f_rows(x):
        return both(x[0][:, :q][:, ::-1], x[1][:, :q])

    def e_rows(x):
        return both(x[0][:, 1:q + 1], x[1][:, 1:q + 1][:, ::-1])

    tk = jnp.stack([lag_rows(tr), lag_rows(ti)], axis=1)
    ta = jnp.stack([f_rows(tr), f_rows(ti), e_rows(tr), e_rows(ti)], axis=1)
    cq = jnp.stack([both(c_re[0], c_re[1]), both(c_im[0], c_im[1])], axis=1).astype(F32)
    bbr_t = jnp.swapaxes(bbr, 2, 3)
    bbi_t = jnp.swapaxes(bbi, 2, 3)
    bq = jnp.stack([both(bbr_t[0], bbr_t[1]), both(bbi_t[0], bbi_t[1])], axis=1)
    bm = jnp.concatenate([bq[:, 0], bq[:, 1]], axis=-1)

    steps = (q * 2.0 ** jnp.arange(n_steps, dtype=F32))[:, None, None, None]
    smag = jnp.exp(zr[None] * steps)
    sang = zi[None] * steps
    aq = jnp.stack([both(smag[:, 0] * jnp.cos(sang[:, 0]), smag[:, 1] * jnp.cos(sang[:, 1])),
                    both(smag[:, 0] * jnp.sin(sang[:, 0]), smag[:, 1] * jnp.sin(sang[:, 1]))], axis=2)
    return tk, ta, cq, bq, bm, aq.transpose(1, 0, 2, 3)


def _s5_ops_kernel(tk_ref, ta_ref, cq_ref, bq_ref, bm_ref, m_ref, f_ref, et_ref):
    w = V7X_LANES
    hh = S5_GROUP
    q = S5_CHUNK

    def outer(t, c):
        return (c[:, None, :] * t[None, :, :]).reshape(hh * t.shape[0], w)

    cre, cim = cq_ref[0, 0], cq_ref[0, 1]
    bre, bim = bq_ref[0, 0], bq_ref[0, 1]
    tre, tim = tk_ref[0, 0], tk_ref[0, 1]
    w_re = outer(tre, cre) - outer(tim, cim)
    w_im = outer(tre, cim) + outer(tim, cre)
    clt = jnp.concatenate([w_re, -w_im], axis=1)
    bk = lax.dot_general(bm_ref[0], clt, (((1,), (1,)), ((), ())), precision=lax.Precision.HIGHEST,
                         preferred_element_type=F32)
    lo = lax.broadcasted_iota(jnp.int32, (q, w), 1) < q
    for hi in range(hh):
        for j in range(hh // 2):
            ka = jnp.broadcast_to(bk[hi:hi + 1, (2 * j) * w:(2 * j + 1) * w], (q, w))
            kb = jnp.broadcast_to(bk[hi:hi + 1, (2 * j + 1) * w:(2 * j + 2) * w], (q, w))
            tile = jnp.where(lo, pltpu.roll(ka, w - (q - 1), 1, stride=1, stride_axis=0),
                             pltpu.roll(kb, 1, 1, stride=1, stride_axis=0))
            m_ref[0, hi * q:(hi + 1) * q, j * w:(j + 1) * w] = tile.astype(BF16)

    fr, fi, er, ei = ta_ref[0, 0], ta_ref[0, 1], ta_ref[0, 2], ta_ref[0, 3]
    f_ref[0, :, 0:w] = (outer(fr, bre) - outer(fi, bim)).astype(BF16)
    f_ref[0, :, w:2 * w] = (outer(fr, bim) + outer(fi, bre)).astype(BF16)
    et_ref[0, :, 0:w] = (outer(er, cre) - outer(ei, cim)).astype(BF16)
    et_ref[0, :, w:2 * w] = (-(outer(er, cim) + outer(ei, cre))).astype(BF16)


def _s5_ops(tk, ta, cq, bq, bm):
    g = tk.shape[0]
    p2 = 2 * S5_STATE
    cw = S5_CHUNK_WIDTH

    def spec(a):
        return pl.BlockSpec((1,) + a.shape[1:], lambda i: (i,) + (0,) * (a.ndim - 1))

    return pl.pallas_call(
        _s5_ops_kernel,
        grid=(g,),
        in_specs=[spec(tk), spec(ta), spec(cq), spec(bq), spec(bm)],
        out_specs=[
            pl.BlockSpec((1, cw, cw), lambda i: (i, 0, 0)),
            pl.BlockSpec((1, cw, 2 * p2), lambda i: (i, 0, 0)),
            pl.BlockSpec((1, cw, 2 * p2), lambda i: (i, 0, 0)),
        ],
        out_shape=[
            jax.ShapeDtypeStruct((g, cw, cw), BF16),
            jax.ShapeDtypeStruct((g, cw, 2 * p2), BF16),
            jax.ShapeDtypeStruct((g, cw, 2 * p2), BF16),
        ],
        compiler_params=_cparams(1, 32),
        name="s5_ops",
    )(tk, ta, cq, bq, bm)


def _s5_chunk_core_kernel(u_ref, m_ref, f_ref, et_ref, a_ref, y_ref, sa_ref, sb_ref, *, n_chunks, n_steps):
    w = V7X_LANES
    n = n_chunks

    @pl.when(pl.program_id(0) == 0)
    def _():
        sa_ref[...] = jnp.zeros(sa_ref.shape, F32)
        sb_ref[...] = jnp.zeros(sb_ref.shape, F32)

    fwd = lax.broadcasted_iota(jnp.int32, (n, w), 1) < S5_STATE
    u = _from_row_tiles(u_ref.at[0], n).astype(BF16)
    sa_ref[n:2 * n, :] = _dot(u, f_ref[0])

    def neighbours(ref, s, lanes):
        return jnp.where(fwd, ref[n - s:2 * n - s, lanes], ref[n + s:2 * n + s, lanes])

    re, im = slice(0, w), slice(w, 2 * w)
    src, dst = sa_ref, sb_ref
    for k in range(n_steps):
        s = 1 << k
        a = a_ref[0, k]
        ar, ai = a[0:1], a[1:2]
        pr = neighbours(src, s, re)
        pi = neighbours(src, s, im)
        dst[n:2 * n, re] = src[n:2 * n, re] + ar * pr - ai * pi
        dst[n:2 * n, im] = src[n:2 * n, im] + ar * pi + ai * pr
        src, dst = dst, src
    s_in = jnp.concatenate([neighbours(src, 1, re), neighbours(src, 1, im)], axis=1).astype(BF16)
    y = _dot(u, m_ref[0]) + lax.dot_general(s_in, et_ref[0], (((1,), (1,)), ((), ())),
                                            preferred_element_type=F32)
    _to_row_tiles(y, y_ref.at[0])


def _s5_chunk_core(ug, mg, fg, etg, aq):
    g = ug.shape[0]
    n_chunks = ug.shape[1] // V7X_SUBLANES
    cw = S5_CHUNK_WIDTH
    n_steps = aq.shape[1]
    assert (1 << n_steps) == n_chunks
    sw = 2 * V7X_LANES
    return pl.pallas_call(
        functools.partial(_s5_chunk_core_kernel, n_chunks=n_chunks, n_steps=n_steps),
        grid=(g,),
        in_specs=[
            pl.BlockSpec((1, V7X_SUBLANES * n_chunks, V7X_LANES), lambda i: (i, 0, 0)),
            pl.BlockSpec((1, cw, cw), lambda i: (i, 0, 0)),
            pl.BlockSpec((1, cw, sw), lambda i: (i, 0, 0)),
            pl.BlockSpec((1, cw, sw), lambda i: (i, 0, 0)),
            pl.BlockSpec((1, n_steps, 2, V7X_LANES), lambda i: (i, 0, 0, 0)),
        ],
        out_specs=pl.BlockSpec((1, V7X_SUBLANES * n_chunks, V7X_LANES), lambda i: (i, 0, 0)),
        out_shape=jax.ShapeDtypeStruct((g, V7X_SUBLANES * n_chunks, V7X_LANES), F32),
        scratch_shapes=[pltpu.VMEM((3 * n_chunks, sw), F32), pltpu.VMEM((3 * n_chunks, sw), F32)],
        compiler_params=_cparams(1, 32),
        name="s5_core",
    )(ug, mg, fg, etg, aq)


S5_RELAYOUT_TOKENS = 1024
S5_RELAYOUT_PITCH = 136


def _s5_to_groups_kernel(u_ref, o_ref, xt_ref):
    w = V7X_LANES
    q = S5_CHUNK
    lo = lax.broadcasted_iota(jnp.int32, (8, w), 1) < q
    n_blk = S5_RELAYOUT_TOKENS // w
    pitch = S5_RELAYOUT_PITCH
    for sg in range(u_ref.shape[1] // w):
        for b in range(n_blk):
            xt_ref[b * pitch:b * pitch + w, :] = u_ref[b * w:(b + 1) * w, sg * w:(sg + 1) * w].T
        for gp in range(w // S5_GROUP):
            for hp in range(S5_GROUP // 2):
                a0 = xt_ref[pl.ds(gp * S5_GROUP + 2 * hp, n_blk, stride=pitch), :]
                a1 = xt_ref[pl.ds(gp * S5_GROUP + 2 * hp + 1, n_blk, stride=pitch), :]
                even = jnp.where(lo, a0, pltpu.roll(a1, q, 1))
                odd = jnp.where(lo, pltpu.roll(a0, q, 1), a1)
                g = sg * (w // S5_GROUP) + gp
                o_ref[g, pl.ds(hp, n_blk, stride=2 * V7X_SUBLANES), :] = even
                o_ref[g, pl.ds(V7X_SUBLANES + hp, n_blk, stride=2 * V7X_SUBLANES), :] = odd


def _s5_to_tokens_kernel(y_ref, o_ref, yt_ref):
    w = V7X_LANES
    q = S5_CHUNK
    lo = lax.broadcasted_iota(jnp.int32, (8, w), 1) < q
    n_blk = S5_RELAYOUT_TOKENS // w
    pitch = S5_RELAYOUT_PITCH
    for sg in range(o_ref.shape[1] // w):
        for gp in range(w // S5_GROUP):
            for hp in range(S5_GROUP // 2):
                g = sg * (w // S5_GROUP) + gp
                even = y_ref[g, pl.ds(hp, n_blk, stride=2 * V7X_SUBLANES), :]
                odd = y_ref[g, pl.ds(V7X_SUBLANES + hp, n_blk, stride=2 * V7X_SUBLANES), :]
                yt_ref[pl.ds(gp * S5_GROUP + 2 * hp, n_blk, stride=pitch), :] = jnp.where(lo, even, pltpu.roll(odd, q, 1))
                yt_ref[pl.ds(gp * S5_GROUP + 2 * hp + 1, n_blk, stride=pitch), :] = jnp.where(lo, pltpu.roll(even, q, 1), odd)
        for b in range(n_blk):
            o_ref[b * w:(b + 1) * w, sg * w:(sg + 1) * w] = yt_ref[b * pitch:b * pitch + w, :].T


def _s5_in_kernel(x_ref, g_ref, w_ref, u_ref, ug_ref, wb_ref, xt_ref):
    _cast_weight_once(w_ref, wb_ref)
    h = _rms(x_ref[...], g_ref[...]).astype(BF16)
    u_ref[...] = _dot(h, wb_ref[...])
    _s5_to_groups_kernel(u_ref, ug_ref, xt_ref)


def _s5_in(x, g, w_stack, layer):
    t, d = x.shape
    n = w_stack.shape[2]
    tt = S5_RELAYOUT_TOKENS
    rows = V7X_SUBLANES * tt // S5_CHUNK
    return pl.pallas_call(
        _s5_in_kernel,
        grid=(t // tt,),
        in_specs=[
            pl.BlockSpec((tt, d), lambda i: (i, 0)),
            pl.BlockSpec((1, d), lambda i: (0, 0)),
            _layer_weight_spec(w_stack, layer),
        ],
        out_specs=[
            pl.BlockSpec((tt, n), lambda i: (i, 0)),
            pl.BlockSpec((S5_GROUPS, rows, V7X_LANES), lambda i: (0, i, 0)),
        ],
        out_shape=[
            jax.ShapeDtypeStruct((t, n), F32),
            jax.ShapeDtypeStruct((S5_GROUPS, V7X_SUBLANES * t // S5_CHUNK, V7X_LANES), F32),
        ],
        scratch_shapes=[pltpu.VMEM((d, n), BF16),
                        pltpu.VMEM((tt // V7X_LANES * S5_RELAYOUT_PITCH, V7X_LANES), F32)],
        compiler_params=_cparams(1, 40),
        name="s5_in",
    )(x, g.reshape(1, d), w_stack)


def _s5_out_kernel(x_ref, yg_ref, u_ref, d_ref, w_ref, rg_ref, rw_ref, rb_ref, tri_ref,
                   o_ref, ids_ref, wts_ref, cnt_ref, wb_ref, y_ref, yt_ref):
    _cast_weight_once(w_ref, wb_ref)
    _s5_to_tokens_kernel(yg_ref, y_ref, yt_ref)
    y = y_ref[...] + d_ref[...] * u_ref[...]
    act = jax.nn.gelu(y).astype(BF16)
    vg = _dot(act, wb_ref[...])
    x1 = x_ref[...] + vg[:, :D_MODEL] * jax.nn.sigmoid(vg[:, D_MODEL:])
    o_ref[...] = x1
    _route_tile(x1, rg_ref, rw_ref, rb_ref, tri_ref, ids_ref, wts_ref, cnt_ref)


def _s5_out(x, yg, u, d_skip, w_out_stack, layer, router):
    t, d = x.shape
    sw = u.shape[1]
    tt = S5_RELAYOUT_TOKENS
    rows = V7X_SUBLANES * tt // S5_CHUNK
    r_ops, r_specs = _router_operands(*router, tt)
    ro_specs, ro_shapes = _router_outputs(t, tt)
    return pl.pallas_call(
        _s5_out_kernel,
        grid=(t // tt,),
        in_specs=[
            pl.BlockSpec((tt, d), lambda i: (i, 0)),
            pl.BlockSpec((S5_GROUPS, rows, V7X_LANES), lambda i: (0, i, 0)),
            pl.BlockSpec((tt, sw), lambda i: (i, 0)),
            pl.BlockSpec((1, sw), lambda i: (0, 0)),
            _layer_weight_spec(w_out_stack, layer),
        ] + r_specs,
        out_specs=[pl.BlockSpec((tt, d), lambda i: (i, 0))] + ro_specs,
        out_shape=[jax.ShapeDtypeStruct((t, d), F32)] + ro_shapes,
        scratch_shapes=[pltpu.VMEM((sw, 2 * d), BF16), pltpu.VMEM((tt, sw), F32),
                        pltpu.VMEM((tt // V7X_LANES * S5_RELAYOUT_PITCH, V7X_LANES), F32)],
        compiler_params=_cparams(1, 56),
        name="s5_out",
    )(x, yg, u, d_skip.reshape(1, sw).astype(F32), w_out_stack, *r_ops)


def _s5_layer(x, norm_g, layer, w_in_stack, lam_re, lam_im, log_dt, b_re, b_im, c_re, c_im, d_skip, w_out_stack,
              router):
    t = x.shape[0]
    n_chunks = t // S5_CHUNK
    u, ug = _s5_in(x, norm_g, w_in_stack, layer)
    n_steps = n_chunks.bit_length() - 1
    assert (1 << n_steps) == n_chunks
    tk, ta, cq, bq, bm, aq = _s5_param_tables(lam_re, lam_im, log_dt, b_re, b_im, c_re, c_im, n_steps)
    mg, fg, etg = _s5_ops(tk, ta, cq, bq, bm)
    yg = _s5_chunk_core(ug, mg, fg, etg, aq)
    return _s5_out(x, yg, u, d_skip, w_out_stack, layer, router)


def _qkv_kernel(x_ref, g_ref, w_ref, qg_ref, kg_ref, q_ref, k_ref, v_ref, wb_ref):
    _cast_weight_once(w_ref, wb_ref)
    h = _rms(x_ref[...], g_ref[...]).astype(BF16)
    qkv = _dot(h, wb_ref[...])
    tm = qkv.shape[0]
    lo = lax.broadcasted_iota(jnp.int32, (tm, V7X_LANES), 1) < HEAD_DIM
    scale = 1.0 / math.sqrt(HEAD_DIM)

    def headnorm(xt, gt):
        sq = xt * xt
        s0 = jnp.sum(jnp.where(lo, sq, 0.0), axis=-1, keepdims=True)
        s1 = jnp.sum(jnp.where(lo, 0.0, sq), axis=-1, keepdims=True)
        rs = jnp.where(lo, lax.rsqrt(s0 / HEAD_DIM + RMS_EPS), lax.rsqrt(s1 / HEAD_DIM + RMS_EPS))
        return xt * rs * gt

    for t in range(D_MODEL // V7X_LANES):
        lanes = slice(t * V7X_LANES, (t + 1) * V7X_LANES)
        q_ref[:, lanes] = (headnorm(qkv[:, lanes], qg_ref[...]) * scale).astype(BF16)
        klanes = slice(D_MODEL + t * V7X_LANES, D_MODEL + (t + 1) * V7X_LANES)
        k_ref[:, lanes] = headnorm(qkv[:, klanes], kg_ref[...]).astype(BF16)
    v_ref[...] = qkv[:, 2 * D_MODEL:].astype(BF16)


def _qkv(x, g, w_qkv_stack, layer, q_norm, k_norm, tm=512):
    t, d = x.shape
    qg = jnp.tile(q_norm.astype(F32), 2).reshape(1, V7X_LANES)
    kg = jnp.tile(k_norm.astype(F32), 2).reshape(1, V7X_LANES)
    out = jax.ShapeDtypeStruct((t, d), BF16)
    ospec = pl.BlockSpec((tm, d), lambda i: (i, 0))
    return pl.pallas_call(
        _qkv_kernel,
        grid=(t // tm,),
        in_specs=[
            pl.BlockSpec((tm, d), lambda i: (i, 0)),
            pl.BlockSpec((1, d), lambda i: (0, 0)),
            _layer_weight_spec(w_qkv_stack, layer),
            pl.BlockSpec((1, V7X_LANES), lambda i: (0, 0)),
            pl.BlockSpec((1, V7X_LANES), lambda i: (0, 0)),
        ],
        out_specs=[ospec, ospec, ospec],
        out_shape=[out, out, out],
        scratch_shapes=[pltpu.VMEM((d, 3 * d), BF16)],
        compiler_params=_cparams(1, 52),
        name="nat_qkv",
    )(x, g.reshape(1, d), w_qkv_stack, qg, kg)


def _nat_bias_table(rpb_ref, b_ref):
    w = V7X_LANES
    c = lax.broadcasted_iota(jnp.int32, (GRID_W, w), 0)
    lane = lax.broadcasted_iota(jnp.int32, (GRID_W, w), 1)
    lo = lane < GRID_W
    kc = jnp.where(lo, lane, lane - GRID_W)
    ws = jnp.clip(c - WIN_COLS // 2, 0, GRID_W - WIN_COLS)
    valid = (kc >= ws) & (kc < ws + WIN_COLS)
    n_ri = 2 * WIN_ROWS - 1
    for h in range(2):
        t_lo, t_hi = [], []
        for ri in range(n_ri):
            vb = jnp.broadcast_to(rpb_ref[h, ri:ri + 1, :], (GRID_W, w))
            t_lo.append(pltpu.roll(vb, w - (WIN_COLS - 1), 1, stride=1, stride_axis=0))
            t_hi.append(pltpu.roll(vb, GRID_W - (WIN_COLS - 1), 1, stride=1, stride_axis=0))
        for o in range(WIN_ROWS):
            for j in range(WIN_ROWS // 2):
                tile = jnp.where(lo, t_lo[o + 2 * j], t_hi[o + 2 * j + 1])
                b_ref[h, o, :, j * w:(j + 1) * w] = jnp.where(valid, tile, MASK_VALUE)


def _nat_attn_kernel(q_ref, k_ref, v_ref, rpb_ref, o_ref, b_ref, *, rows, unroll):
    lo = lax.broadcasted_iota(jnp.int32, (GRID_W, V7X_LANES), 1) < HEAD_DIM
    head_mask = (jnp.where(lo, 1.0, 0.0), jnp.where(lo, 0.0, 1.0))
    nkeys = WIN_ROWS * GRID_W
    _nat_bias_table(rpb_ref, b_ref)

    def body(rb, carry):
        chains = []
        for u in range(unroll):
            r = rb * unroll + u
            rs = jnp.clip(r - WIN_ROWS // 2, 0, rows - WIN_ROWS)
            off = rs - r + (WIN_ROWS - 1)
            q = q_ref[pl.ds(pl.multiple_of(r * GRID_W, GRID_W), GRID_W), :].astype(F32)
            k = k_ref[pl.ds(pl.multiple_of(rs * GRID_W, GRID_W), nkeys), :]
            for h in range(2):
                chains.append(dict(r=r, rs=rs, off=off, h=h, q=(q * head_mask[h]).astype(BF16), k=k))
        for c in chains:
            c["s"] = lax.dot_general(c["q"], c["k"], (((1,), (1,)), ((), ())), preferred_element_type=F32)
        for c in chains:
            c["s"] = c["s"] + b_ref[c["h"], c["off"]]
        for c in chains:
            c["m"] = jnp.max(c["s"], axis=-1, keepdims=True)
        for c in chains:
            c["p"] = jnp.exp(c["s"] - c["m"])
        for c in chains:
            c["l"] = jnp.sum(c["p"], axis=-1, keepdims=True)
        for c in chains:
            v = v_ref[pl.ds(pl.multiple_of(c["rs"] * GRID_W, GRID_W), nkeys), :]
            c["o"] = _dot(c["p"].astype(BF16), v) / c["l"]
        for u in range(unroll):
            c0, c1 = chains[2 * u], chains[2 * u + 1]
            o = jnp.where(lo, c0["o"], c1["o"])
            o_ref[pl.ds(pl.multiple_of(c0["r"] * GRID_W, GRID_W), GRID_W), :] = o.astype(BF16)
        return carry

    lax.fori_loop(0, rows // unroll, body, 0)


def _nat_attn(q, k, v, rpb):
    t, d = q.shape
    rows = t // GRID_W
    assert rows >= WIN_ROWS
    unroll = math.gcd(rows, NAT_ROW_UNROLL)
    n_ri, n_ci = rpb.shape[1], rpb.shape[2]
    rpb_pad = jnp.pad(rpb.astype(F32), ((0, 0), (0, 2 * WIN_ROWS - n_ri), (0, V7X_LANES - n_ci)))
    spec = pl.BlockSpec((t, V7X_LANES), lambda i: (0, i))
    return pl.pallas_call(
        functools.partial(_nat_attn_kernel, rows=rows, unroll=unroll),
        grid=(d // V7X_LANES,),
        in_specs=[spec, spec, spec,
                  pl.BlockSpec((2, 2 * WIN_ROWS, V7X_LANES), lambda i: (i, 0, 0))],
        out_specs=spec,
        out_shape=jax.ShapeDtypeStruct((t, d), BF16),
        scratch_shapes=[pltpu.VMEM((2, WIN_ROWS, GRID_W, WIN_ROWS * GRID_W), F32)],
        compiler_params=_cparams(1, 48),
        name="nat_attn",
    )(q, k, v, rpb_pad)


def _matmul_residual_kernel(x_ref, a_ref, w_ref, rg_ref, rw_ref, rb_ref, tri_ref,
                            o_ref, ids_ref, wts_ref, cnt_ref, wb_ref):
    _cast_weight_once(w_ref, wb_ref)
    x1 = x_ref[...] + _dot(a_ref[...], wb_ref[...])
    o_ref[...] = x1
    _route_tile(x1, rg_ref, rw_ref, rb_ref, tri_ref, ids_ref, wts_ref, cnt_ref)


def _matmul_residual(x, a, w_stack, layer, router, tm=512):
    t, d = x.shape
    kdim = a.shape[1]
    r_ops, r_specs = _router_operands(*router, tm)
    ro_specs, ro_shapes = _router_outputs(t, tm)
    return pl.pallas_call(
        _matmul_residual_kernel,
        grid=(t // tm,),
        in_specs=[
            pl.BlockSpec((tm, d), lambda i: (i, 0)),
            pl.BlockSpec((tm, kdim), lambda i: (i, 0)),
            _layer_weight_spec(w_stack, layer),
        ] + r_specs,
        out_specs=[pl.BlockSpec((tm, d), lambda i: (i, 0))] + ro_specs,
        out_shape=[jax.ShapeDtypeStruct((t, d), F32)] + ro_shapes,
        scratch_shapes=[pltpu.VMEM((kdim, d), BF16)],
        compiler_params=_cparams(1, 32),
        name="matmul_residual",
    )(x, a, w_stack, *r_ops)


def _nat_layer(x, norm_g, layer, w_qkv_stack, q_norm, k_norm, rpb, w_o_stack, router):
    q, k, v = _qkv(x, norm_g, w_qkv_stack, layer, q_norm, k_norm)
    o = _nat_attn(q, k, v, rpb)
    return _matmul_residual(x, o, w_o_stack, layer, router)


def _route_tile(x, g_ref, w_ref, b_ref, tri_ref, ids_ref, wts_ref, cnt_ref):
    @pl.when(pl.program_id(0) == 0)
    def _():
        cnt_ref[...] = jnp.zeros(cnt_ref.shape, F32)

    def split(a):
        hi = a.astype(BF16)
        return hi, (a - hi.astype(F32)).astype(BF16)

    def dot_nt(a, b):
        return lax.dot_general(a, b, (((1,), (1,)), ((), ())), preferred_element_type=F32)

    h_hi, h_lo = split(_rms(x, g_ref[...]))
    w_hi, w_lo = split(w_ref[...])
    lg = dot_nt(w_hi, h_hi) + (dot_nt(w_hi, h_lo) + dot_nt(w_lo, h_hi)) + b_ref[...]
    row = lax.broadcasted_iota(jnp.int32, lg.shape, 0)
    big = 4 * V7X_LANES
    is_g = (row >= N_EXPERTS) & (row < N_EXPERTS + N_GROUPS)
    gmax = jnp.max(jnp.where(is_g, lg, MASK_VALUE), axis=0, keepdims=True)
    gsum = jnp.sum(jnp.where(is_g, jnp.exp(lg - gmax), 0.0), axis=0, keepdims=True)
    g_val = 1.0 / gsum
    g_idx = jnp.min(jnp.where(is_g & (lg == gmax), row, big), axis=0, keepdims=True) - N_EXPERTS
    in_grp = (row < N_EXPERTS) & ((row // EXPERTS_PER_GROUP) == g_idx)
    m1 = jnp.max(jnp.where(in_grp, lg, MASK_VALUE), axis=0, keepdims=True)
    i1 = jnp.min(jnp.where(in_grp & (lg == m1), row, big), axis=0, keepdims=True)
    rest = in_grp & (row != i1)
    m2 = jnp.max(jnp.where(rest, lg, MASK_VALUE), axis=0, keepdims=True)
    i2 = jnp.min(jnp.where(rest & (lg == m2), row, big), axis=0, keepdims=True)
    z = jnp.sum(jnp.where(in_grp, jnp.exp(lg - m1), 0.0), axis=0, keepdims=True)
    p1 = 1.0 / z
    p2 = jnp.exp(m2 - m1) / z
    den = p1 + p2
    w1 = g_val * (p1 / den)
    w2 = g_val * (p2 / den)

    sel1 = (row == i1)[:N_EXPERTS]
    sel2 = (row == i2)[:N_EXPERTS]
    onehot = jnp.where(sel1 | sel2, 1.0, 0.0)
    before = _dot(onehot.astype(BF16), tri_ref[...]) + cnt_ref[:, 0:1]
    r1 = jnp.sum(jnp.where(sel1, before, 0.0), axis=0, keepdims=True)
    r2 = jnp.sum(jnp.where(sel2, before, 0.0), axis=0, keepdims=True)
    cnt_ref[...] = cnt_ref[...] + jnp.sum(onehot, axis=1, keepdims=True)

    orow = lax.broadcasted_iota(jnp.int32, ids_ref.shape, 0)
    ids_ref[...] = jnp.where(orow == 0, i1, jnp.where(orow == 1, i2, jnp.where(
        orow == 2, r1.astype(jnp.int32), jnp.where(orow == 3, r2.astype(jnp.int32), 0))))
    wts_ref[...] = jnp.where(orow == 0, w1, jnp.where(orow == 1, w2, 0.0))


def _router_operands(g, w_group, b_group, w_expert, b_expert, tm):
    d = g.shape[0]
    rows = MOE_ROUTER_ROWS
    pad = rows - N_EXPERTS - N_GROUPS
    w = jnp.pad(jnp.concatenate([w_expert, w_group], axis=1).astype(F32).T, ((0, pad), (0, 0)))
    b = jnp.pad(jnp.concatenate([b_expert, b_group]).astype(F32), (0, pad)).reshape(rows, 1)
    tri = (jnp.arange(tm)[:, None] < jnp.arange(tm)[None, :]).astype(BF16)
    specs = [
        pl.BlockSpec((1, d), lambda i: (0, 0)),
        pl.BlockSpec((rows, d), lambda i: (0, 0)),
        pl.BlockSpec((rows, 1), lambda i: (0, 0)),
        pl.BlockSpec((tm, tm), lambda i: (0, 0)),
    ]
    return [g.reshape(1, d).astype(F32), w, b, tri], specs


def _router_outputs(t, tm):
    specs = [
        pl.BlockSpec((V7X_SUBLANES, tm), lambda i: (0, i)),
        pl.BlockSpec((V7X_SUBLANES, tm), lambda i: (0, i)),
        pl.BlockSpec((N_EXPERTS, V7X_LANES), lambda i: (0, 0)),
    ]
    shapes = [
        jax.ShapeDtypeStruct((V7X_SUBLANES, t), jnp.int32),
        jax.ShapeDtypeStruct((V7X_SUBLANES, t), F32),
        jax.ShapeDtypeStruct((N_EXPERTS, V7X_LANES), F32),
    ]
    return specs, shapes


def _moe_tables(cnt, t):
    tm = MOE_ROW_TILE
    n_rows = 2 * t
    counts = cnt[:, 0].astype(jnp.int32)
    ends = jnp.cumsum(counts)
    starts = ends - counts
    brk = jnp.concatenate([jnp.arange(n_rows // tm, dtype=jnp.int32) * tm, starts])
    idx = jnp.arange(brk.shape[0])
    before = (brk[None, :] < brk[:, None]) | ((brk[None, :] == brk[:, None]) & (idx[None, :] < idx[:, None]))
    rank = jnp.sum(before.astype(jnp.int32), axis=1)
    lo = jnp.sum(jnp.where(rank[None, :] == idx[:, None], brk[None, :], 0), axis=1)
    hi = jnp.concatenate([lo[1:], jnp.full((1,), n_rows, jnp.int32)])
    anchor = jnp.minimum(lo, n_rows - 1)
    tile = anchor // tm
    expert = jnp.minimum(jnp.sum((ends[None, :] <= anchor[:, None]).astype(jnp.int32), axis=1), N_EXPERTS - 1)
    lo_in = lo - tile * tm
    hi_in = hi - tile * tm
    first = ((hi > lo) & (lo_in == 0)).astype(jnp.int32)
    last = ((hi > lo) & (hi_in == tm)).astype(jnp.int32)
    newexp = jnp.concatenate([jnp.ones((1,), jnp.int32), (expert[1:] != expert[:-1]).astype(jnp.int32)])
    table = jnp.stack([tile, expert, lo_in, hi_in, first, last, newexp]).astype(jnp.int32)
    return starts, table


def _moe_pos_kernel(starts_ref, ids_ref, pos_ref):
    e = ids_ref[0:2, :]
    acc = jnp.zeros(e.shape, jnp.int32)
    for ex in range(N_EXPERTS):
        acc = jnp.where(e == ex, starts_ref[ex], acc)
    pos_ref[0:2, :] = acc + ids_ref[2:4, :]
    pos_ref[2:8, :] = jnp.zeros((6, e.shape[1]), jnp.int32)


def _moe_positions(ids, starts, tm=2048):
    t = ids.shape[1]
    tm = min(tm, t)
    grid_spec = pltpu.PrefetchScalarGridSpec(
        num_scalar_prefetch=1,
        grid=(t // tm,),
        in_specs=[pl.BlockSpec((8, tm), lambda i, st: (0, i))],
        out_specs=pl.BlockSpec((8, tm), lambda i, st: (0, i)),
    )
    return pl.pallas_call(
        _moe_pos_kernel,
        grid_spec=grid_spec,
        out_shape=jax.ShapeDtypeStruct((8, t), jnp.int32),
        compiler_params=_cparams(1, 16),
        name="moe_positions",
    )(starts, ids)


def _tile_positions(pos, tm):
    return pos.reshape(2, -1, tm).transpose(1, 0, 2)


def _to_row_tiles(x, dst_ref):
    rows = x.shape[0]
    for c in range(x.shape[1] // V7X_LANES):
        dst_ref[pl.ds(c, rows, stride=V7X_SUBLANES), :] = x[:, c * V7X_LANES:(c + 1) * V7X_LANES]


def _from_row_tiles(src_ref, rows, base=0):
    return jnp.concatenate([src_ref[pl.ds(base + c, rows, stride=V7X_SUBLANES), :] for c in range(V7X_SUBLANES)], axis=1)


def _row_tile(ref, r):
    return ref.at[pl.ds(pl.multiple_of(r * V7X_SUBLANES, V7X_SUBLANES), V7X_SUBLANES), :]


def _moe_dispatch_kernel(pos_ref, x_ref, g_ref, xs_ref, xt_ref, sem):
    tm = x_ref.shape[0]
    _to_row_tiles(_rms(x_ref[...], g_ref[...]), xt_ref)
    for j in range(tm):
        for k in range(2):
            pltpu.make_async_copy(_row_tile(xt_ref, j), _row_tile(xs_ref, pos_ref[0, k, j]), sem).start(priority=k)
    for k in range(2):
        pltpu.make_async_copy(xt_ref, xs_ref.at[pl.ds(0, V7X_SUBLANES * tm), :], sem).wait()


def _moe_dispatch(x, g, pos, tm=MOE_TOKEN_TILE):
    t, d = x.shape
    assert d == V7X_SUBLANES * V7X_LANES
    return pl.pallas_call(
        _moe_dispatch_kernel,
        grid=(t // tm,),
        in_specs=[
            pl.BlockSpec((1, 2, tm), lambda i: (i, 0, 0), memory_space=pltpu.SMEM),
            pl.BlockSpec((tm, d), lambda i: (i, 0)),
            pl.BlockSpec((1, d), lambda i: (0, 0)),
        ],
        out_specs=pl.BlockSpec(memory_space=pl.ANY),
        out_shape=jax.ShapeDtypeStruct((2 * t * V7X_SUBLANES, V7X_LANES), F32),
        scratch_shapes=[pltpu.VMEM((V7X_SUBLANES * tm, V7X_LANES), F32), pltpu.SemaphoreType.DMA(())],
        compiler_params=_cparams(1, 32),
        name="moe_dispatch",
    )(_tile_positions(pos, tm), x, g.reshape(1, d))


def _moe_ffn_kernel(tab_ref, xs_ref, wg_ref, wu_ref, wd_ref, ys_ref, wgb_ref, wub_ref, wdb_ref, acc_ref, xbuf_ref, sems):
    i = pl.program_id(0)
    n_items = pl.num_programs(0)
    tile_rows = xbuf_ref.shape[1]

    def tile_copy(item):
        slot = item % MOE_FFN_RING
        start = pl.multiple_of(tab_ref[0, item] * tile_rows, tile_rows)
        return pltpu.make_async_copy(xs_ref.at[pl.ds(start, tile_rows), :], xbuf_ref.at[slot], sems.at[slot])

    @pl.when(i == 0)
    def _():
        for k in range(MOE_FFN_RING - 1):
            tile_copy(k).start()

    @pl.when(i + (MOE_FFN_RING - 1) < n_items)
    def _():
        tile_copy(i + (MOE_FFN_RING - 1)).start()

    tile_copy(i).wait()
    x_tile_ref = xbuf_ref.at[i % MOE_FFN_RING]
    lo = tab_ref[2, i]
    hi = tab_ref[3, i]
    first = tab_ref[4, i] == 1
    last = tab_ref[5, i] == 1
    tm = acc_ref.shape[0]

    @pl.when(tab_ref[6, i] == 1)
    def _():
        wgb_ref[...] = wg_ref[0, 0, 0].astype(BF16)
        wub_ref[...] = wu_ref[0, 0, 0].astype(BF16)
        wdb_ref[...] = wd_ref[0, 0, 0].astype(BF16)

    @pl.when(hi > lo)
    def _():
        h = _from_row_tiles(x_tile_ref, tm).astype(BF16)
        a = _dot(h, wgb_ref[...])
        u = _dot(h, wub_ref[...])
        rowid = lax.broadcasted_iota(jnp.int32, a.shape, 0)
        act = jnp.where((rowid >= lo) & (rowid < hi), jax.nn.silu(a) * u, 0.0).astype(BF16)
        y = _dot(act, wdb_ref[...])

        @pl.when(first & last)
        def _():
            _to_row_tiles(y, ys_ref)

        @pl.when(first & jnp.logical_not(last))
        def _():
            acc_ref[...] = y

        @pl.when(jnp.logical_not(first) & jnp.logical_not(last))
        def _():
            acc_ref[...] += y

        @pl.when(jnp.logical_not(first) & last)
        def _():
            _to_row_tiles(acc_ref[...] + y, ys_ref)


def _moe_ffn(xs, table, w_gate, w_up, w_down, layer):
    d, ff = w_gate.shape[-2:]
    tm = MOE_ROW_TILE
    n_items = table.shape[1]
    epg = EXPERTS_PER_GROUP
    wmap = lambda i, tab: (layer, tab[1, i] // epg, tab[1, i] % epg, 0, 0)
    grid_spec = pltpu.PrefetchScalarGridSpec(
        num_scalar_prefetch=1,
        grid=(n_items,),
        in_specs=[
            pl.BlockSpec(memory_space=pl.ANY),
            pl.BlockSpec((1, 1, 1, d, ff), wmap),
            pl.BlockSpec((1, 1, 1, d, ff), wmap),
            pl.BlockSpec((1, 1, 1, ff, d), wmap),
        ],
        out_specs=pl.BlockSpec((V7X_SUBLANES * tm, V7X_LANES), lambda i, tab: (tab[0, i], 0)),
        scratch_shapes=[pltpu.VMEM((d, ff), BF16), pltpu.VMEM((d, ff), BF16), pltpu.VMEM((ff, d), BF16),
                        pltpu.VMEM((tm, d), F32),
                        pltpu.VMEM((MOE_FFN_RING, V7X_SUBLANES * tm, V7X_LANES), F32),
                        pltpu.SemaphoreType.DMA((MOE_FFN_RING,))],
    )
    return pl.pallas_call(
        _moe_ffn_kernel,
        grid_spec=grid_spec,
        out_shape=jax.ShapeDtypeStruct(xs.shape, F32),
        compiler_params=_cparams(1, 32),
        name="moe_ffn",
    )(table, xs, w_gate, w_up, w_down)


def _combine_ple_kernel(pos_ref, x_ref, wc_ref, ys_ref, g_ref, p_ref, wg_ref, wp_ref, o_ref,
                        ya_ref, yb_ref, wgb_ref, wpb_ref, sems):
    tm = x_ref.shape[0]
    hm = tm // 2
    for j in range(tm):
        sem = sems.at[j // hm]
        pltpu.make_async_copy(_row_tile(ys_ref, pos_ref[0, 0, j]), _row_tile(ya_ref, j), sem).start(priority=0)
        pltpu.make_async_copy(_row_tile(ys_ref, pos_ref[0, 1, j]), _row_tile(yb_ref, j), sem).start(priority=1)
    _cast_weight_once(wg_ref, wgb_ref)
    _cast_weight_once(wp_ref, wpb_ref)
    for half in range(2):
        rows = pl.ds(half * hm, hm)
        tiles = pl.ds(half * V7X_SUBLANES * hm, V7X_SUBLANES * hm)
        proj = _dot(p_ref[0, 0, rows, :].astype(BF16), wpb_ref[...])
        pltpu.make_async_copy(ys_ref.at[pl.ds(0, V7X_SUBLANES * hm), :], ya_ref.at[tiles, :], sems.at[half]).wait()
        pltpu.make_async_copy(ys_ref.at[pl.ds(0, V7X_SUBLANES * hm), :], yb_ref.at[tiles, :], sems.at[half]).wait()
        w = wc_ref[rows, :]
        x2 = (x_ref[rows, :] + w[:, 0:1] * _from_row_tiles(ya_ref, hm, half * V7X_SUBLANES * hm)
              + w[:, 1:2] * _from_row_tiles(yb_ref, hm, half * V7X_SUBLANES * hm))
        h = _rms(x2, g_ref[...]).astype(BF16)
        gate = jax.nn.sigmoid(_dot(h, wgb_ref[...]))
        o_ref[rows, :] = x2 + gate * proj


def _combine_ple(x, pos, wts, ys, norm_g, p_stack, w_proj_stack, w_gate_stack, layer, tm=MOE_TOKEN_TILE):
    t, d = x.shape
    pd = p_stack.shape[-1]
    wc = wts[0:2].T
    return pl.pallas_call(
        _combine_ple_kernel,
        grid=(t // tm,),
        in_specs=[
            pl.BlockSpec((1, 2, tm), lambda i: (i, 0, 0), memory_space=pltpu.SMEM),
            pl.BlockSpec((tm, d), lambda i: (i, 0)),
            pl.BlockSpec((tm, 2), lambda i: (i, 0)),
            pl.BlockSpec(memory_space=pl.ANY),
            pl.BlockSpec((1, d), lambda i: (0, 0)),
            pl.BlockSpec((1, 1, tm, pd), lambda i: (layer, 0, i, 0)),
            _layer_weight_spec(w_gate_stack, layer),
            _layer_weight_spec(w_proj_stack, layer),
        ],
        out_specs=pl.BlockSpec((tm, d), lambda i: (i, 0)),
        out_shape=jax.ShapeDtypeStruct((t, d), F32),
        scratch_shapes=[pltpu.VMEM((V7X_SUBLANES * tm, V7X_LANES), F32), pltpu.VMEM((V7X_SUBLANES * tm, V7X_LANES), F32),
                        pltpu.VMEM((d, d), BF16), pltpu.VMEM((pd, d), BF16), pltpu.SemaphoreType.DMA((2,))],
        compiler_params=_cparams(1, 40),
        name="moe_combine_ple",
    )(_tile_positions(pos, tm), x, wc, ys, norm_g.reshape(1, d), p_stack, w_gate_stack, w_proj_stack)


def _moe_ple_layer(x, routing, layer, norm_ffn, w_gate, w_up, w_down, norm_ple, p_stack, ple_w_proj, ple_w_gate):
    t = x.shape[0]
    ids, wts, cnt = routing
    starts, table = _moe_tables(cnt, t)
    pos = _moe_positions(ids, starts)[0:2]
    xs = _moe_dispatch(x, norm_ffn, pos)
    ys = _moe_ffn(xs, table, w_gate, w_up, w_down, layer)
    return _combine_ple(x, pos, wts, ys, norm_ple, p_stack, ple_w_proj, ple_w_gate, layer)


def kernel(x, p, norm_mix, norm_ffn, norm_ple, s5_w_in, s5_lam_re, s5_lam_im, s5_log_dt, s5_b_re, s5_b_im, s5_c_re, s5_c_im, s5_d, s5_w_out, nat_w_qkv, nat_q_norm, nat_k_norm, nat_rpb, nat_w_o, moe_w_group, moe_b_group, moe_w_expert, moe_b_expert, moe_w_gate, moe_w_up, moe_w_down, ple_w_proj, ple_w_gate):
    bsz, seq, d = x.shape
    depth = p.shape[0]
    assert bsz == 1 and d == D_MODEL
    xs = x.reshape(seq, d).astype(F32)
    for i in range(depth):
        j = i // 2
        router = (norm_ffn[i], moe_w_group[i], moe_b_group[i], moe_w_expert[i], moe_b_expert[i])
        if i % 2 == 0:
            xs, *routing = _s5_layer(xs, norm_mix[i], j, s5_w_in, s5_lam_re[j], s5_lam_im[j], s5_log_dt[j],
                                     s5_b_re[j], s5_b_im[j], s5_c_re[j], s5_c_im[j], s5_d[j], s5_w_out, router)
        else:
            xs, *routing = _nat_layer(xs, norm_mix[i], j, nat_w_qkv, nat_q_norm[j], nat_k_norm[j], nat_rpb[j],
                                      nat_w_o, router)
        xs = _moe_ple_layer(xs, routing, i, norm_ffn[i], moe_w_gate, moe_w_up, moe_w_down,
                            norm_ple[i], p, ple_w_proj, ple_w_gate)
    return xs.reshape(bsz, seq, d).astype(x.dtype)
```

```python
import functools
import math

import jax
import jax.numpy as jnp
from jax import lax
from jax.experimental import pallas as pl
from jax.experimental.pallas import tpu as pltpu

F32 = jnp.float32
BF16 = jnp.bfloat16

D_MODEL = 1024
GRID_W = 64
S5_GROUP = 16
S5_STATE = 64
S5_GROUPS = 32
HEAD_DIM = 64
WIN_ROWS = 8
WIN_COLS = 16
N_GROUPS = 4
EXPERTS_PER_GROUP = 8
N_EXPERTS = N_GROUPS * EXPERTS_PER_GROUP
RMS_EPS = 1e-6
MASK_VALUE = -1e30

V7X_LANES = 128
V7X_SUBLANES = 8
V7X_VMEM_BYTES = 64 * 1024 * 1024

S5_CHUNK = 64
S5_CHUNK_WIDTH = S5_CHUNK * S5_GROUP

NAT_ROW_UNROLL = 32

MOE_ROUTER_ROWS = 40
MOE_ROW_TILE = 512
MOE_TOKEN_TILE = 512
MOE_FFN_RING = 3


def _cparams(n_axes, vmem_mib):
    return pltpu.CompilerParams(
        dimension_semantics=("arbitrary",) * n_axes,
        vmem_limit_bytes=min(vmem_mib * 1024 * 1024, V7X_VMEM_BYTES - 4 * 1024 * 1024),
    )


def _dot(a, b):
    return jnp.dot(a, b, preferred_element_type=F32)


def _rms(x, g):
    ms = jnp.mean(x * x, axis=-1, keepdims=True)
    return x * lax.rsqrt(ms + RMS_EPS) * g


def _layer_weight_spec(w_stack, layer):
    _, k, n = w_stack.shape
    return pl.BlockSpec((1, k, n), lambda i: (layer, 0, 0), pipeline_mode=pl.Buffered(1))


def _cast_weight_once(w_ref, wb_ref):
    @pl.when(pl.program_id(0) == 0)
    def _():
        wb_ref[...] = w_ref[0].astype(BF16)


def _s5_param_tables(lam_re, lam_im, log_dt, b_re, b_im, c_re, c_im, n_steps):
    q = S5_CHUNK
    g, p = S5_GROUPS, S5_STATE
    dt = jnp.exp(log_dt.astype(F32))[..., None]
    lam_re = lam_re.astype(F32)
    lam_im = lam_im.astype(F32)
    zr = lam_re * dt
    zi = lam_im * dt
    k = jnp.arange(q + 1, dtype=F32)[:, None]
    mag = jnp.exp(zr[:, :, None, :] * k)
    ang = zi[:, :, None, :] * k
    tr = mag * jnp.cos(ang)
    ti = mag * jnp.sin(ang)
    nr = tr[:, :, 1] - 1.0
    ni = ti[:, :, 1]
    den = lam_re * lam_re + lam_im * lam_im
    cr = (nr * lam_re + ni * lam_im) / den
    ci = (ni * lam_re - nr * lam_im) / den
    b_re = b_re.astype(F32)
    b_im = b_im.astype(F32)
    bbr = cr[..., None] * b_re - ci[..., None] * b_im
    bbi = cr[..., None] * b_im + ci[..., None] * b_re

    def both(fwd, rev):
        return jnp.concatenate([fwd, rev], axis=-1)

    def lag_rows(x):
        fwd = jnp.pad(x[0][:, :q], ((0, 0), (q - 1, 1), (0, 0)))
        rev = jnp.pad(x[1][:, :q][:, ::-1], ((0, 0), (0, q), (0, 0)))
        return both(fwd, rev)

    def f_rows(x):
        return both(x[0][:, :q][:, ::-1], x[1][:, :q])

    def e_rows(x):
        return both(x[0][:, 1:q + 1], x[1][:, 1:q + 1][:, ::-1])

    tk = jnp.stack([lag_rows(tr), lag_rows(ti)], axis=1)
    ta = jnp.stack([f_rows(tr), f_rows(ti), e_rows(tr), e_rows(ti)], axis=1)
    cq = jnp.stack([both(c_re[0], c_re[1]), both(c_im[0], c_im[1])], axis=1).astype(F32)
    bbr_t = jnp.swapaxes(bbr, 2, 3)
    bbi_t = jnp.swapaxes(bbi, 2, 3)
    bq = jnp.stack([both(bbr_t[0], bbr_t[1]), both(bbi_t[0], bbi_t[1])], axis=1)
    bm = jnp.concatenate([bq[:, 0], bq[:, 1]], axis=-1)

    steps = (q * 2.0 ** jnp.arange(n_steps, dtype=F32))[:, None, None, None]
    smag = jnp.exp(zr[None] * steps)
    sang = zi[None] * steps
    aq = jnp.stack([both(smag[:, 0] * jnp.cos(sang[:, 0]), smag[:, 1] * jnp.cos(sang[:, 1])),
                    both(smag[:, 0] * jnp.sin(sang[:, 0]), smag[:, 1] * jnp.sin(sang[:, 1]))], axis=2)
    return tk, ta, cq, bq, bm, aq.transpose(1, 0, 2, 3)


def _s5_ops_kernel(tk_ref, ta_ref, cq_ref, bq_ref, bm_ref, m_ref, f_ref, et_ref):
    w = V7X_LANES
    hh = S5_GROUP
    q = S5_CHUNK

    def outer(t, c):
        return (c[:, None, :] * t[None, :, :]).reshape(hh * t.shape[0], w)

    cre, cim = cq_ref[0, 0], cq_ref[0, 1]
    bre, bim = bq_ref[0, 0], bq_ref[0, 1]
    tre, tim = tk_ref[0, 0], tk_ref[0, 1]
    w_re = outer(tre, cre) - outer(tim, cim)
    w_im = outer(tre, cim) + outer(tim, cre)
    clt = jnp.concatenate([w_re, -w_im], axis=1)
    bk = lax.dot_general(bm_ref[0], clt, (((1,), (1,)), ((), ())), precision=lax.Precision.HIGHEST,
                         preferred_element_type=F32)
    lo = lax.broadcasted_iota(jnp.int32, (q, w), 1) < q
    for hi in range(hh):
        for j in range(hh // 2):
            ka = jnp.broadcast_to(bk[hi:hi + 1, (2 * j) * w:(2 * j + 1) * w], (q, w))
            kb = jnp.broadcast_to(bk[hi:hi + 1, (2 * j + 1) * w:(2 * j + 2) * w], (q, w))
            tile = jnp.where(lo, pltpu.roll(ka, w - (q - 1), 1, stride=1, stride_axis=0),
                             pltpu.roll(kb, 1, 1, stride=1, stride_axis=0))
            m_ref[0, hi * q:(hi + 1) * q, j * w:(j + 1) * w] = tile.astype(BF16)

    fr, fi, er, ei = ta_ref[0, 0], ta_ref[0, 1], ta_ref[0, 2], ta_ref[0, 3]
    f_ref[0, :, 0:w] = (outer(fr, bre) - outer(fi, bim)).astype(BF16)
    f_ref[0, :, w:2 * w] = (outer(fr, bim) + outer(fi, bre)).astype(BF16)
    et_ref[0, :, 0:w] = (outer(er, cre) - outer(ei, cim)).astype(BF16)
    et_ref[0, :, w:2 * w] = (-(outer(er, cim) + outer(ei, cre))).astype(BF16)


def _s5_ops(tk, ta, cq, bq, bm):
    g = tk.shape[0]
    p2 = 2 * S5_STATE
    cw = S5_CHUNK_WIDTH

    def spec(a):
        return pl.BlockSpec((1,) + a.shape[1:], lambda i: (i,) + (0,) * (a.ndim - 1))

    return pl.pallas_call(
        _s5_ops_kernel,
        grid=(g,),
        in_specs=[spec(tk), spec(ta), spec(cq), spec(bq), spec(bm)],
        out_specs=[
            pl.BlockSpec((1, cw, cw), lambda i: (i, 0, 0)),
            pl.BlockSpec((1, cw, 2 * p2), lambda i: (i, 0, 0)),
            pl.BlockSpec((1, cw, 2 * p2), lambda i: (i, 0, 0)),
        ],
        out_shape=[
            jax.ShapeDtypeStruct((g, cw, cw), BF16),
            jax.ShapeDtypeStruct((g, cw, 2 * p2), BF16),
            jax.ShapeDtypeStruct((g, cw, 2 * p2), BF16),
        ],
        compiler_params=_cparams(1, 32),
        name="s5_ops",
    )(tk, ta, cq, bq, bm)


def _s5_chunk_core_kernel(u_ref, m_ref, f_ref, et_ref, a_ref, y_ref, sa_ref, sb_ref, *, n_chunks, n_steps):
    w = V7X_LANES
    n = n_chunks

    @pl.when(pl.program_id(0) == 0)
    def _():
        sa_ref[...] = jnp.zeros(sa_ref.shape, F32)
        sb_ref[...] = jnp.zeros(sb_ref.shape, F32)

    fwd = lax.broadcasted_iota(jnp.int32, (n, w), 1) < S5_STATE
    u = _from_row_tiles(u_ref.at[0], n).astype(BF16)
    sa_ref[n:2 * n, :] = _dot(u, f_ref[0])

    def neighbours(ref, s, lanes):
        return jnp.where(fwd, ref[n - s:2 * n - s, lanes], ref[n + s:2 * n + s, lanes])

    re, im = slice(0, w), slice(w, 2 * w)
    src, dst = sa_ref, sb_ref
    for k in range(n_steps):
        s = 1 << k
        a = a_ref[0, k]
        ar, ai = a[0:1], a[1:2]
        pr = neighbours(src, s, re)
        pi = neighbours(src, s, im)
        dst[n:2 * n, re] = src[n:2 * n, re] + ar * pr - ai * pi
        dst[n:2 * n, im] = src[n:2 * n, im] + ar * pi + ai * pr
        src, dst = dst, src
    s_in = jnp.concatenate([neighbours(src, 1, re), neighbours(src, 1, im)], axis=1).astype(BF16)
    y = _dot(u, m_ref[0]) + lax.dot_general(s_in, et_ref[0], (((1,), (1,)), ((), ())),
                                            preferred_element_type=F32)
    _to_row_tiles(y, y_ref.at[0])


def _s5_chunk_core(ug, mg, fg, etg, aq):
    g = ug.shape[0]
    n_chunks = ug.shape[1] // V7X_SUBLANES
    cw = S5_CHUNK_WIDTH
    n_steps = aq.shape[1]
    assert (1 << n_steps) == n_chunks
    sw = 2 * V7X_LANES
    return pl.pallas_call(
        functools.partial(_s5_chunk_core_kernel, n_chunks=n_chunks, n_steps=n_steps),
        grid=(g,),
        in_specs=[
            pl.BlockSpec((1, V7X_SUBLANES * n_chunks, V7X_LANES), lambda i: (i, 0, 0)),
            pl.BlockSpec((1, cw, cw), lambda i: (i, 0, 0)),
            pl.BlockSpec((1, cw, sw), lambda i: (i, 0, 0)),
            pl.BlockSpec((1, cw, sw), lambda i: (i, 0, 0)),
            pl.BlockSpec((1, n_steps, 2, V7X_LANES), lambda i: (i, 0, 0, 0)),
        ],
        out_specs=pl.BlockSpec((1, V7X_SUBLANES * n_chunks, V7X_LANES), lambda i: (i, 0, 0)),
        out_shape=jax.ShapeDtypeStruct((g, V7X_SUBLANES * n_chunks, V7X_LANES), F32),
        scratch_shapes=[pltpu.VMEM((3 * n_chunks, sw), F32), pltpu.VMEM((3 * n_chunks, sw), F32)],
        compiler_params=_cparams(1, 32),
        name="s5_core",
    )(ug, mg, fg, etg, aq)


S5_RELAYOUT_TOKENS = 1024
S5_RELAYOUT_PITCH = 136


def _s5_to_groups_kernel(u_ref, o_ref, xt_ref):
    w = V7X_LANES
    q = S5_CHUNK
    lo = lax.broadcasted_iota(jnp.int32, (8, w), 1) < q
    n_blk = S5_RELAYOUT_TOKENS // w
    pitch = S5_RELAYOUT_PITCH
    for sg in range(u_ref.shape[1] // w):
        for b in range(n_blk):
            xt_ref[b * pitch:b * pitch + w, :] = u_ref[b * w:(b + 1) * w, sg * w:(sg + 1) * w].T
        for gp in range(w // S5_GROUP):
            for hp in range(S5_GROUP // 2):
                a0 = xt_ref[pl.ds(gp * S5_GROUP + 2 * hp, n_blk, stride=pitch), :]
                a1 = xt_ref[pl.ds(gp * S5_GROUP + 2 * hp + 1, n_blk, stride=pitch), :]
                even = jnp.where(lo, a0, pltpu.roll(a1, q, 1))
                odd = jnp.where(lo, pltpu.roll(a0, q, 1), a1)
                g = sg * (w // S5_GROUP) + gp
                o_ref[g, pl.ds(hp, n_blk, stride=2 * V7X_SUBLANES), :] = even
                o_ref[g, pl.ds(V7X_SUBLANES + hp, n_blk, stride=2 * V7X_SUBLANES), :] = odd


def _s5_to_tokens_kernel(y_ref, o_ref, yt_ref):
    w = V7X_LANES
    q = S5_CHUNK
    lo = lax.broadcasted_iota(jnp.int32, (8, w), 1) < q
    n_blk = S5_RELAYOUT_TOKENS // w
    pitch = S5_RELAYOUT_PITCH
    for sg in range(o_ref.shape[1] // w):
        for gp in range(w // S5_GROUP):
            for hp in range(S5_GROUP // 2):
                g = sg * (w // S5_GROUP) + gp
                even = y_ref[g, pl.ds(hp, n_blk, stride=2 * V7X_SUBLANES), :]
                odd = y_ref[g, pl.ds(V7X_SUBLANES + hp, n_blk, stride=2 * V7X_SUBLANES), :]
                yt_ref[pl.ds(gp * S5_GROUP + 2 * hp, n_blk, stride=pitch), :] = jnp.where(lo, even, pltpu.roll(odd, q, 1))
                yt_ref[pl.ds(gp * S5_GROUP + 2 * hp + 1, n_blk, stride=pitch), :] = jnp.where(lo, pltpu.roll(even, q, 1), odd)
        for b in range(n_blk):
            o_ref[b * w:(b + 1) * w, sg * w:(sg + 1) * w] = yt_ref[b * pitch:b * pitch + w, :].T


def _s5_in_kernel(x_ref, g_ref, w_ref, u_ref, ug_ref, wb_ref, xt_ref):
    _cast_weight_once(w_ref, wb_ref)
    h = _rms(x_ref[...], g_ref[...]).astype(BF16)
    u_ref[...] = _dot(h, wb_ref[...])
    _s5_to_groups_kernel(u_ref, ug_ref, xt_ref)


def _s5_in(x, g, w_stack, layer):
    t, d = x.shape
    n = w_stack.shape[2]
    tt = S5_RELAYOUT_TOKENS
    rows = V7X_SUBLANES * tt // S5_CHUNK
    return pl.pallas_call(
        _s5_in_kernel,
        grid=(t // tt,),
        in_specs=[
            pl.BlockSpec((tt, d), lambda i: (i, 0)),
            pl.BlockSpec((1, d), lambda i: (0, 0)),
            _layer_weight_spec(w_stack, layer),
        ],
        out_specs=[
            pl.BlockSpec((tt, n), lambda i: (i, 0)),
            pl.BlockSpec((S5_GROUPS, rows, V7X_LANES), lambda i: (0, i, 0)),
        ],
        out_shape=[
            jax.ShapeDtypeStruct((t, n), F32),
            jax.ShapeDtypeStruct((S5_GROUPS, V7X_SUBLANES * t // S5_CHUNK, V7X_LANES), F32),
        ],
        scratch_shapes=[pltpu.VMEM((d, n), BF16),
                        pltpu.VMEM((tt // V7X_LANES * S5_RELAYOUT_PITCH, V7X_LANES), F32)],
        compiler_params=_cparams(1, 40),
        name="s5_in",
    )(x, g.reshape(1, d), w_stack)


def _s5_out_kernel(x_ref, yg_ref, u_ref, d_ref, w_ref, rg_ref, rw_ref, rb_ref, tri_ref,
                   o_ref, ids_ref, wts_ref, cnt_ref, wb_ref, y_ref, yt_ref):
    _cast_weight_once(w_ref, wb_ref)
    _s5_to_tokens_kernel(yg_ref, y_ref, yt_ref)
    y = y_ref[...] + d_ref[...] * u_ref[...]
    act = jax.nn.gelu(y).astype(BF16)
    vg = _dot(act, wb_ref[...])
    x1 = x_ref[...] + vg[:, :D_MODEL] * jax.nn.sigmoid(vg[:, D_MODEL:])
    o_ref[...] = x1
    hm = x1.shape[0] // 2
    for half in range(2):
        cols = pl.ds(half * hm, hm)
        _route_tile(x1[half * hm:(half + 1) * hm], rg_ref, rw_ref, rb_ref, tri_ref,
                    ids_ref.at[:, cols], wts_ref.at[:, cols], cnt_ref, reset=(half == 0))


def _s5_out(x, yg, u, d_skip, w_out_stack, layer, router):
    t, d = x.shape
    sw = u.shape[1]
    tt = S5_RELAYOUT_TOKENS
    rows = V7X_SUBLANES * tt // S5_CHUNK
    r_ops, r_specs = _router_operands(*router, tt // 2)
    ro_specs, ro_shapes = _router_outputs(t, tt)
    return pl.pallas_call(
        _s5_out_kernel,
        grid=(t // tt,),
        in_specs=[
            pl.BlockSpec((tt, d), lambda i: (i, 0)),
            pl.BlockSpec((S5_GROUPS, rows, V7X_LANES), lambda i: (0, i, 0)),
            pl.BlockSpec((tt, sw), lambda i: (i, 0)),
            pl.BlockSpec((1, sw), lambda i: (0, 0)),
            _layer_weight_spec(w_out_stack, layer),
        ] + r_specs,
        out_specs=[pl.BlockSpec((tt, d), lambda i: (i, 0))] + ro_specs,
        out_shape=[jax.ShapeDtypeStruct((t, d), F32)] + ro_shapes,
        scratch_shapes=[pltpu.VMEM((sw, 2 * d), BF16), pltpu.VMEM((tt, sw), F32),
                        pltpu.VMEM((tt // V7X_LANES * S5_RELAYOUT_PITCH, V7X_LANES), F32)],
        compiler_params=_cparams(1, 56),
        name="s5_out",
    )(x, yg, u, d_skip.reshape(1, sw).astype(F32), w_out_stack, *r_ops)


def _s5_layer(x, norm_g, layer, w_in_stack, lam_re, lam_im, log_dt, b_re, b_im, c_re, c_im, d_skip, w_out_stack,
              router):
    t = x.shape[0]
    n_chunks = t // S5_CHUNK
    u, ug = _s5_in(x, norm_g, w_in_stack, layer)
    n_steps = n_chunks.bit_length() - 1
    assert (1 << n_steps) == n_chunks
    tk, ta, cq, bq, bm, aq = _s5_param_tables(lam_re, lam_im, log_dt, b_re, b_im, c_re, c_im, n_steps)
    mg, fg, etg = _s5_ops(tk, ta, cq, bq, bm)
    yg = _s5_chunk_core(ug, mg, fg, etg, aq)
    return _s5_out(x, yg, u, d_skip, w_out_stack, layer, router)


def _qkv_kernel(x_ref, g_ref, w_ref, qg_ref, kg_ref, q_ref, k_ref, v_ref, wb_ref):
    _cast_weight_once(w_ref, wb_ref)
    h = _rms(x_ref[...], g_ref[...]).astype(BF16)
    qkv = _dot(h, wb_ref[...])
    tm = qkv.shape[0]
    lo = lax.broadcasted_iota(jnp.int32, (tm, V7X_LANES), 1) < HEAD_DIM
    scale = 1.0 / math.sqrt(HEAD_DIM)

    def headnorm(xt, gt):
        sq = xt * xt
        s0 = jnp.sum(jnp.where(lo, sq, 0.0), axis=-1, keepdims=True)
        s1 = jnp.sum(jnp.where(lo, 0.0, sq), axis=-1, keepdims=True)
        rs = jnp.where(lo, lax.rsqrt(s0 / HEAD_DIM + RMS_EPS), lax.rsqrt(s1 / HEAD_DIM + RMS_EPS))
        return xt * rs * gt

    for t in range(D_MODEL // V7X_LANES):
        lanes = slice(t * V7X_LANES, (t + 1) * V7X_LANES)
        q_ref[:, lanes] = (headnorm(qkv[:, lanes], qg_ref[...]) * scale).astype(BF16)
        klanes = slice(D_MODEL + t * V7X_LANES, D_MODEL + (t + 1) * V7X_LANES)
        k_ref[:, lanes] = headnorm(qkv[:, klanes], kg_ref[...]).astype(BF16)
    v_ref[...] = qkv[:, 2 * D_MODEL:].astype(BF16)


def _qkv(x, g, w_qkv_stack, layer, q_norm, k_norm, tm=512):
    t, d = x.shape
    qg = jnp.tile(q_norm.astype(F32), 2).reshape(1, V7X_LANES)
    kg = jnp.tile(k_norm.astype(F32), 2).reshape(1, V7X_LANES)
    out = jax.ShapeDtypeStruct((t, d), BF16)
    ospec = pl.BlockSpec((tm, d), lambda i: (i, 0))
    return pl.pallas_call(
        _qkv_kernel,
        grid=(t // tm,),
        in_specs=[
            pl.BlockSpec((tm, d), lambda i: (i, 0)),
            pl.BlockSpec((1, d), lambda i: (0, 0)),
            _layer_weight_spec(w_qkv_stack, layer),
            pl.BlockSpec((1, V7X_LANES), lambda i: (0, 0)),
            pl.BlockSpec((1, V7X_LANES), lambda i: (0, 0)),
        ],
        out_specs=[ospec, ospec, ospec],
        out_shape=[out, out, out],
        scratch_shapes=[pltpu.VMEM((d, 3 * d), BF16)],
        compiler_params=_cparams(1, 52),
        name="nat_qkv",
    )(x, g.reshape(1, d), w_qkv_stack, qg, kg)


def _nat_bias_table(rpb_ref, b_ref):
    w = V7X_LANES
    c = lax.broadcasted_iota(jnp.int32, (GRID_W, w), 0)
    lane = lax.broadcasted_iota(jnp.int32, (GRID_W, w), 1)
    lo = lane < GRID_W
    kc = jnp.where(lo, lane, lane - GRID_W)
    ws = jnp.clip(c - WIN_COLS // 2, 0, GRID_W - WIN_COLS)
    valid = (kc >= ws) & (kc < ws + WIN_COLS)
    n_ri = 2 * WIN_ROWS - 1
    for h in range(2):
        t_lo, t_hi = [], []
        for ri in range(n_ri):
            vb = jnp.broadcast_to(rpb_ref[h, ri:ri + 1, :], (GRID_W, w))
            t_lo.append(pltpu.roll(vb, w - (WIN_COLS - 1), 1, stride=1, stride_axis=0))
            t_hi.append(pltpu.roll(vb, GRID_W - (WIN_COLS - 1), 1, stride=1, stride_axis=0))
        for o in range(WIN_ROWS):
            for j in range(WIN_ROWS // 2):
                tile = jnp.where(lo, t_lo[o + 2 * j], t_hi[o + 2 * j + 1])
                b_ref[h, o, :, j * w:(j + 1) * w] = jnp.where(valid, tile, MASK_VALUE)


def _nat_attn_kernel(q_ref, k_ref, v_ref, rpb_ref, o_ref, b_ref, *, rows, unroll):
    lo = lax.broadcasted_iota(jnp.int32, (GRID_W, V7X_LANES), 1) < HEAD_DIM
    head_mask = (jnp.where(lo, 1.0, 0.0), jnp.where(lo, 0.0, 1.0))
    nkeys = WIN_ROWS * GRID_W
    _nat_bias_table(rpb_ref, b_ref)

    def body(rb, carry):
        chains = []
        for u in range(unroll):
            r = rb * unroll + u
            rs = jnp.clip(r - WIN_ROWS // 2, 0, rows - WIN_ROWS)
            off = rs - r + (WIN_ROWS - 1)
            q = q_ref[pl.ds(pl.multiple_of(r * GRID_W, GRID_W), GRID_W), :].astype(F32)
            k = k_ref[pl.ds(pl.multiple_of(rs * GRID_W, GRID_W), nkeys), :]
            for h in range(2):
                chains.append(dict(r=r, rs=rs, off=off, h=h, q=(q * head_mask[h]).astype(BF16), k=k))
        for c in chains:
            c["s"] = lax.dot_general(c["q"], c["k"], (((1,), (1,)), ((), ())), preferred_element_type=F32)
        for c in chains:
            c["s"] = c["s"] + b_ref[c["h"], c["off"]]
        for c in chains:
            c["m"] = jnp.max(c["s"], axis=-1, keepdims=True)
        for c in chains:
            c["p"] = jnp.exp(c["s"] - c["m"])
        for c in chains:
            c["l"] = jnp.sum(c["p"], axis=-1, keepdims=True)
        for c in chains:
            v = v_ref[pl.ds(pl.multiple_of(c["rs"] * GRID_W, GRID_W), nkeys), :]
            c["o"] = _dot(c["p"].astype(BF16), v) / c["l"]
        for u in range(unroll):
            c0, c1 = chains[2 * u], chains[2 * u + 1]
            o = jnp.where(lo, c0["o"], c1["o"])
            o_ref[pl.ds(pl.multiple_of(c0["r"] * GRID_W, GRID_W), GRID_W), :] = o.astype(BF16)
        return carry

    lax.fori_loop(0, rows // unroll, body, 0)


def _nat_attn(q, k, v, rpb):
    t, d = q.shape
    rows = t // GRID_W
    assert rows >= WIN_ROWS
    unroll = math.gcd(rows, NAT_ROW_UNROLL)
    n_ri, n_ci = rpb.shape[1], rpb.shape[2]
    rpb_pad = jnp.pad(rpb.astype(F32), ((0, 0), (0, 2 * WIN_ROWS - n_ri), (0, V7X_LANES - n_ci)))
    spec = pl.BlockSpec((t, V7X_LANES), lambda i: (0, i))
    return pl.pallas_call(
        functools.partial(_nat_attn_kernel, rows=rows, unroll=unroll),
        grid=(d // V7X_LANES,),
        in_specs=[spec, spec, spec,
                  pl.BlockSpec((2, 2 * WIN_ROWS, V7X_LANES), lambda i: (i, 0, 0))],
        out_specs=spec,
        out_shape=jax.ShapeDtypeStruct((t, d), BF16),
        scratch_shapes=[pltpu.VMEM((2, WIN_ROWS, GRID_W, WIN_ROWS * GRID_W), F32)],
        compiler_params=_cparams(1, 48),
        name="nat_attn",
    )(q, k, v, rpb_pad)


def _matmul_residual_kernel(x_ref, a_ref, w_ref, rg_ref, rw_ref, rb_ref, tri_ref,
                            o_ref, ids_ref, wts_ref, cnt_ref, wb_ref):
    _cast_weight_once(w_ref, wb_ref)
    x1 = x_ref[...] + _dot(a_ref[...], wb_ref[...])
    o_ref[...] = x1
    _route_tile(x1, rg_ref, rw_ref, rb_ref, tri_ref, ids_ref, wts_ref, cnt_ref)


def _matmul_residual(x, a, w_stack, layer, router, tm=512):
    t, d = x.shape
    kdim = a.shape[1]
    r_ops, r_specs = _router_operands(*router, tm)
    ro_specs, ro_shapes = _router_outputs(t, tm)
    return pl.pallas_call(
        _matmul_residual_kernel,
        grid=(t // tm,),
        in_specs=[
            pl.BlockSpec((tm, d), lambda i: (i, 0)),
            pl.BlockSpec((tm, kdim), lambda i: (i, 0)),
            _layer_weight_spec(w_stack, layer),
        ] + r_specs,
        out_specs=[pl.BlockSpec((tm, d), lambda i: (i, 0))] + ro_specs,
        out_shape=[jax.ShapeDtypeStruct((t, d), F32)] + ro_shapes,
        scratch_shapes=[pltpu.VMEM((kdim, d), BF16)],
        compiler_params=_cparams(1, 32),
        name="matmul_residual",
    )(x, a, w_stack, *r_ops)


def _nat_layer(x, norm_g, layer, w_qkv_stack, q_norm, k_norm, rpb, w_o_stack, router):
    q, k, v = _qkv(x, norm_g, w_qkv_stack, layer, q_norm, k_norm)
    o = _nat_attn(q, k, v, rpb)
    return _matmul_residual(x, o, w_o_stack, layer, router)


def _route_tile(x, g_ref, w_ref, b_ref, tri_ref, ids_ref, wts_ref, cnt_ref, reset=True):
    if reset:
        @pl.when(pl.program_id(0) == 0)
        def _():
            cnt_ref[...] = jnp.zeros(cnt_ref.shape, F32)

    def split(a):
        hi = a.astype(BF16)
        return hi, (a - hi.astype(F32)).astype(BF16)

    def dot_nt(a, b):
        return lax.dot_general(a, b, (((1,), (1,)), ((), ())), preferred_element_type=F32)

    h_hi, h_lo = split(_rms(x, g_ref[...]))
    w_hi, w_lo = split(w_ref[...])
    lg = dot_nt(w_hi, h_hi) + (dot_nt(w_hi, h_lo) + dot_nt(w_lo, h_hi)) + b_ref[...]
    row = lax.broadcasted_iota(jnp.int32, lg.shape, 0)
    big = 4 * V7X_LANES
    is_g = (row >= N_EXPERTS) & (row < N_EXPERTS + N_GROUPS)
    gmax = jnp.max(jnp.where(is_g, lg, MASK_VALUE), axis=0, keepdims=True)
    gsum = jnp.sum(jnp.where(is_g, jnp.exp(lg - gmax), 0.0), axis=0, keepdims=True)
    g_val = 1.0 / gsum
    g_idx = jnp.min(jnp.where(is_g & (lg == gmax), row, big), axis=0, keepdims=True) - N_EXPERTS
    in_grp = (row < N_EXPERTS) & ((row // EXPERTS_PER_GROUP) == g_idx)
    m1 = jnp.max(jnp.where(in_grp, lg, MASK_VALUE), axis=0, keepdims=True)
    i1 = jnp.min(jnp.where(in_grp & (lg == m1), row, big), axis=0, keepdims=True)
    rest = in_grp & (row != i1)
    m2 = jnp.max(jnp.where(rest, lg, MASK_VALUE), axis=0, keepdims=True)
    i2 = jnp.min(jnp.where(rest & (lg == m2), row, big), axis=0, keepdims=True)
    z = jnp.sum(jnp.where(in_grp, jnp.exp(lg - m1), 0.0), axis=0, keepdims=True)
    p1 = 1.0 / z
    p2 = jnp.exp(m2 - m1) / z
    den = p1 + p2
    w1 = g_val * (p1 / den)
    w2 = g_val * (p2 / den)

    sel1 = (row == i1)[:N_EXPERTS]
    sel2 = (row == i2)[:N_EXPERTS]
    onehot = jnp.where(sel1 | sel2, 1.0, 0.0)
    before = _dot(onehot.astype(BF16), tri_ref[...]) + cnt_ref[:, 0:1]
    r1 = jnp.sum(jnp.where(sel1, before, 0.0), axis=0, keepdims=True)
    r2 = jnp.sum(jnp.where(sel2, before, 0.0), axis=0, keepdims=True)
    cnt_ref[...] = cnt_ref[...] + jnp.sum(onehot, axis=1, keepdims=True)

    orow = lax.broadcasted_iota(jnp.int32, ids_ref.shape, 0)
    ids_ref[...] = jnp.where(orow == 0, i1, jnp.where(orow == 1, i2, jnp.where(
        orow == 2, r1.astype(jnp.int32), jnp.where(orow == 3, r2.astype(jnp.int32), 0))))
    wts_ref[...] = jnp.where(orow == 0, w1, jnp.where(orow == 1, w2, 0.0))


def _router_operands(g, w_group, b_group, w_expert, b_expert, tm):
    d = g.shape[0]
    rows = MOE_ROUTER_ROWS
    pad = rows - N_EXPERTS - N_GROUPS
    w = jnp.pad(jnp.concatenate([w_expert, w_group], axis=1).astype(F32).T, ((0, pad), (0, 0)))
    b = jnp.pad(jnp.concatenate([b_expert, b_group]).astype(F32), (0, pad)).reshape(rows, 1)
    tri = (jnp.arange(tm)[:, None] < jnp.arange(tm)[None, :]).astype(BF16)
    specs = [
        pl.BlockSpec((1, d), lambda i: (0, 0)),
        pl.BlockSpec((rows, d), lambda i: (0, 0)),
        pl.BlockSpec((rows, 1), lambda i: (0, 0)),
        pl.BlockSpec((tm, tm), lambda i: (0, 0)),
    ]
    return [g.reshape(1, d).astype(F32), w, b, tri], specs


def _router_outputs(t, tm):
    specs = [
        pl.BlockSpec((V7X_SUBLANES, tm), lambda i: (0, i)),
        pl.BlockSpec((V7X_SUBLANES, tm), lambda i: (0, i)),
        pl.BlockSpec((N_EXPERTS, V7X_LANES), lambda i: (0, 0)),
    ]
    shapes = [
        jax.ShapeDtypeStruct((V7X_SUBLANES, t), jnp.int32),
        jax.ShapeDtypeStruct((V7X_SUBLANES, t), F32),
        jax.ShapeDtypeStruct((N_EXPERTS, V7X_LANES), F32),
    ]
    return specs, shapes


def _moe_tables(cnt, t):
    tm = MOE_ROW_TILE
    n_rows = 2 * t
    counts = cnt[:, 0].astype(jnp.int32)
    ends = jnp.cumsum(counts)
    starts = ends - counts
    brk = jnp.concatenate([jnp.arange(n_rows // tm, dtype=jnp.int32) * tm, starts])
    idx = jnp.arange(brk.shape[0])
    before = (brk[None, :] < brk[:, None]) | ((brk[None, :] == brk[:, None]) & (idx[None, :] < idx[:, None]))
    rank = jnp.sum(before.astype(jnp.int32), axis=1)
    lo = jnp.sum(jnp.where(rank[None, :] == idx[:, None], brk[None, :], 0), axis=1)
    hi = jnp.concatenate([lo[1:], jnp.full((1,), n_rows, jnp.int32)])
    anchor = jnp.minimum(lo, n_rows - 1)
    tile = anchor // tm
    expert = jnp.minimum(jnp.sum((ends[None, :] <= anchor[:, None]).astype(jnp.int32), axis=1), N_EXPERTS - 1)
    lo_in = lo - tile * tm
    hi_in = hi - tile * tm
    first = ((hi > lo) & (lo_in == 0)).astype(jnp.int32)
    last = ((hi > lo) & (hi_in == tm)).astype(jnp.int32)
    newexp = jnp.concatenate([jnp.ones((1,), jnp.int32), (expert[1:] != expert[:-1]).astype(jnp.int32)])
    table = jnp.stack([tile, expert, lo_in, hi_in, first, last, newexp]).astype(jnp.int32)
    return starts, table


def _moe_pos_kernel(starts_ref, ids_ref, pos_ref):
    e = ids_ref[0:2, :]
    acc = jnp.zeros(e.shape, jnp.int32)
    for ex in range(N_EXPERTS):
        acc = jnp.where(e == ex, starts_ref[ex], acc)
    pos_ref[0:2, :] = acc + ids_ref[2:4, :]
    pos_ref[2:8, :] = jnp.zeros((6, e.shape[1]), jnp.int32)


def _moe_positions(ids, starts, tm=2048):
    t = ids.shape[1]
    tm = min(tm, t)
    grid_spec = pltpu.PrefetchScalarGridSpec(
        num_scalar_prefetch=1,
        grid=(t // tm,),
        in_specs=[pl.BlockSpec((8, tm), lambda i, st: (0, i))],
        out_specs=pl.BlockSpec((8, tm), lambda i, st: (0, i)),
    )
    return pl.pallas_call(
        _moe_pos_kernel,
        grid_spec=grid_spec,
        out_shape=jax.ShapeDtypeStruct((8, t), jnp.int32),
        compiler_params=_cparams(1, 16),
        name="moe_positions",
    )(starts, ids)


def _tile_positions(pos, tm):
    return pos.reshape(2, -1, tm).transpose(1, 0, 2)


def _to_row_tiles(x, dst_ref):
    rows = x.shape[0]
    for c in range(x.shape[1] // V7X_LANES):
        dst_ref[pl.ds(c, rows, stride=V7X_SUBLANES), :] = x[:, c * V7X_LANES:(c + 1) * V7X_LANES]


def _from_row_tiles(src_ref, rows, base=0):
    return jnp.concatenate([src_ref[pl.ds(base + c, rows, stride=V7X_SUBLANES), :] for c in range(V7X_SUBLANES)], axis=1)


def _row_tile(ref, r):
    return ref.at[pl.ds(pl.multiple_of(r * V7X_SUBLANES, V7X_SUBLANES), V7X_SUBLANES), :]


def _moe_dispatch_kernel(pos_ref, x_ref, g_ref, xs_ref, xt_ref, sem):
    tm = x_ref.shape[0]
    _to_row_tiles(_rms(x_ref[...], g_ref[...]), xt_ref)
    for j in range(tm):
        for k in range(2):
            pltpu.make_async_copy(_row_tile(xt_ref, j), _row_tile(xs_ref, pos_ref[0, k, j]), sem).start(priority=k)
    for k in range(2):
        pltpu.make_async_copy(xt_ref, xs_ref.at[pl.ds(0, V7X_SUBLANES * tm), :], sem).wait()


def _moe_dispatch(x, g, pos, tm=MOE_TOKEN_TILE):
    t, d = x.shape
    assert d == V7X_SUBLANES * V7X_LANES
    return pl.pallas_call(
        _moe_dispatch_kernel,
        grid=(t // tm,),
        in_specs=[
            pl.BlockSpec((1, 2, tm), lambda i: (i, 0, 0), memory_space=pltpu.SMEM),
            pl.BlockSpec((tm, d), lambda i: (i, 0)),
            pl.BlockSpec((1, d), lambda i: (0, 0)),
        ],
        out_specs=pl.BlockSpec(memory_space=pl.ANY),
        out_shape=jax.ShapeDtypeStruct((2 * t * V7X_SUBLANES, V7X_LANES), F32),
        scratch_shapes=[pltpu.VMEM((V7X_SUBLANES * tm, V7X_LANES), F32), pltpu.SemaphoreType.DMA(())],
        compiler_params=_cparams(1, 32),
        name="moe_dispatch",
    )(_tile_positions(pos, tm), x, g.reshape(1, d))


def _moe_ffn_kernel(tab_ref, xs_ref, wg_ref, wu_ref, wd_ref, ys_ref, wgb_ref, wub_ref, wdb_ref, acc_ref, xbuf_ref, sems):
    i = pl.program_id(0)
    n_items = pl.num_programs(0)
    tile_rows = xbuf_ref.shape[1]

    def tile_copy(item):
        slot = item % MOE_FFN_RING
        start = pl.multiple_of(tab_ref[0, item] * tile_rows, tile_rows)
        return pltpu.make_async_copy(xs_ref.at[pl.ds(start, tile_rows), :], xbuf_ref.at[slot], sems.at[slot])

    @pl.when(i == 0)
    def _():
        for k in range(MOE_FFN_RING - 1):
            tile_copy(k).start()

    @pl.when(i + (MOE_FFN_RING - 1) < n_items)
    def _():
        tile_copy(i + (MOE_FFN_RING - 1)).start()

    tile_copy(i).wait()
    x_tile_ref = xbuf_ref.at[i % MOE_FFN_RING]
    lo = tab_ref[2, i]
    hi = tab_ref[3, i]
    first = tab_ref[4, i] == 1
    last = tab_ref[5, i] == 1
    tm = acc_ref.shape[0]

    @pl.when(tab_ref[6, i] == 1)
    def _():
        wgb_ref[...] = wg_ref[0, 0, 0].astype(BF16)
        wub_ref[...] = wu_ref[0, 0, 0].astype(BF16)
        wdb_ref[...] = wd_ref[0, 0, 0].astype(BF16)

    @pl.when(hi > lo)
    def _():
        h = _from_row_tiles(x_tile_ref, tm).astype(BF16)
        a = _dot(h, wgb_ref[...])
        u = _dot(h, wub_ref[...])
        rowid = lax.broadcasted_iota(jnp.int32, a.shape, 0)
        act = jnp.where((rowid >= lo) & (rowid < hi), jax.nn.silu(a) * u, 0.0).astype(BF16)
        y = _dot(act, wdb_ref[...])

        @pl.when(first & last)
        def _():
            _to_row_tiles(y, ys_ref)

        @pl.when(first & jnp.logical_not(last))
        def _():
            acc_ref[...] = y

        @pl.when(jnp.logical_not(first) & jnp.logical_not(last))
        def _():
            acc_ref[...] += y

        @pl.when(jnp.logical_not(first) & last)
        def _():
            _to_row_tiles(acc_ref[...] + y, ys_ref)


def _moe_ffn(xs, table, w_gate, w_up, w_down, layer):
    d, ff = w_gate.shape[-2:]
    tm = MOE_ROW_TILE
    n_items = table.shape[1]
    epg = EXPERTS_PER_GROUP
    wmap = lambda i, tab: (layer, tab[1, i] // epg, tab[1, i] % epg, 0, 0)
    grid_spec = pltpu.PrefetchScalarGridSpec(
        num_scalar_prefetch=1,
        grid=(n_items,),
        in_specs=[
            pl.BlockSpec(memory_space=pl.ANY),
            pl.BlockSpec((1, 1, 1, d, ff), wmap),
            pl.BlockSpec((1, 1, 1, d, ff), wmap),
            pl.BlockSpec((1, 1, 1, ff, d), wmap),
        ],
        out_specs=pl.BlockSpec((V7X_SUBLANES * tm, V7X_LANES), lambda i, tab: (tab[0, i], 0)),
        scratch_shapes=[pltpu.VMEM((d, ff), BF16), pltpu.VMEM((d, ff), BF16), pltpu.VMEM((ff, d), BF16),
                        pltpu.VMEM((tm, d), F32),
                        pltpu.VMEM((MOE_FFN_RING, V7X_SUBLANES * tm, V7X_LANES), F32),
                        pltpu.SemaphoreType.DMA((MOE_FFN_RING,))],
    )
    return pl.pallas_call(
        _moe_ffn_kernel,
        grid_spec=grid_spec,
        out_shape=jax.ShapeDtypeStruct(xs.shape, F32),
        compiler_params=_cparams(1, 32),
        name="moe_ffn",
    )(table, xs, w_gate, w_up, w_down)


def _combine_ple_kernel(pos_ref, x_ref, wc_ref, ys_ref, g_ref, p_ref, wg_ref, wp_ref, o_ref,
                        ya_ref, yb_ref, wgb_ref, wpb_ref, sems):
    tm = x_ref.shape[0]
    hm = tm // 2
    for j in range(tm):
        sem = sems.at[j // hm]
        pltpu.make_async_copy(_row_tile(ys_ref, pos_ref[0, 0, j]), _row_tile(ya_ref, j), sem).start(priority=0)
        pltpu.make_async_copy(_row_tile(ys_ref, pos_ref[0, 1, j]), _row_tile(yb_ref, j), sem).start(priority=1)
    _cast_weight_once(wg_ref, wgb_ref)
    _cast_weight_once(wp_ref, wpb_ref)
    for half in range(2):
        rows = pl.ds(half * hm, hm)
        tiles = pl.ds(half * V7X_SUBLANES * hm, V7X_SUBLANES * hm)
        proj = _dot(p_ref[0, 0, rows, :].astype(BF16), wpb_ref[...])
        pltpu.make_async_copy(ys_ref.at[pl.ds(0, V7X_SUBLANES * hm), :], ya_ref.at[tiles, :], sems.at[half]).wait()
        pltpu.make_async_copy(ys_ref.at[pl.ds(0, V7X_SUBLANES * hm), :], yb_ref.at[tiles, :], sems.at[half]).wait()
        w = wc_ref[rows, :]
        x2 = (x_ref[rows, :] + w[:, 0:1] * _from_row_tiles(ya_ref, hm, half * V7X_SUBLANES * hm)
              + w[:, 1:2] * _from_row_tiles(yb_ref, hm, half * V7X_SUBLANES * hm))
        h = _rms(x2, g_ref[...]).astype(BF16)
        gate = jax.nn.sigmoid(_dot(h, wgb_ref[...]))
        o_ref[rows, :] = x2 + gate * proj


def _combine_ple(x, pos, wts, ys, norm_g, p_stack, w_proj_stack, w_gate_stack, layer, tm=MOE_TOKEN_TILE):
    t, d = x.shape
    pd = p_stack.shape[-1]
    wc = wts[0:2].T
    return pl.pallas_call(
        _combine_ple_kernel,
        grid=(t // tm,),
        in_specs=[
            pl.BlockSpec((1, 2, tm), lambda i: (i, 0, 0), memory_space=pltpu.SMEM),
            pl.BlockSpec((tm, d), lambda i: (i, 0)),
            pl.BlockSpec((tm, 2), lambda i: (i, 0)),
            pl.BlockSpec(memory_space=pl.ANY),
            pl.BlockSpec((1, d), lambda i: (0, 0)),
            pl.BlockSpec((1, 1, tm, pd), lambda i: (layer, 0, i, 0)),
            _layer_weight_spec(w_gate_stack, layer),
            _layer_weight_spec(w_proj_stack, layer),
        ],
        out_specs=pl.BlockSpec((tm, d), lambda i: (i, 0)),
        out_shape=jax.ShapeDtypeStruct((t, d), F32),
        scratch_shapes=[pltpu.VMEM((V7X_SUBLANES * tm, V7X_LANES), F32), pltpu.VMEM((V7X_SUBLANES * tm, V7X_LANES), F32),
                        pltpu.VMEM((d, d), BF16), pltpu.VMEM((pd, d), BF16), pltpu.SemaphoreType.DMA((2,))],
        compiler_params=_cparams(1, 40),
        name="moe_combine_ple",
    )(_tile_positions(pos, tm), x, wc, ys, norm_g.reshape(1, d), p_stack, w_gate_stack, w_proj_stack)


def _moe_ple_layer(x, routing, layer, norm_ffn, w_gate, w_up, w_down, norm_ple, p_stack, ple_w_proj, ple_w_gate):
    t = x.shape[0]
    ids, wts, cnt = routing
    starts, table = _moe_tables(cnt, t)
    pos = _moe_positions(ids, starts)[0:2]
    xs = _moe_dispatch(x, norm_ffn, pos)
    ys = _moe_ffn(xs, table, w_gate, w_up, w_down, layer)
    return _combine_ple(x, pos, wts, ys, norm_ple, p_stack, ple_w_proj, ple_w_gate, layer)


def kernel(x, p, norm_mix, norm_ffn, norm_ple, s5_w_in, s5_lam_re, s5_lam_im, s5_log_dt, s5_b_re, s5_b_im, s5_c_re, s5_c_im, s5_d, s5_w_out, nat_w_qkv, nat_q_norm, nat_k_norm, nat_rpb, nat_w_o, moe_w_group, moe_b_group, moe_w_expert, moe_b_expert, moe_w_gate, moe_w_up, moe_w_down, ple_w_proj, ple_w_gate):
    bsz, seq, d = x.shape
    depth = p.shape[0]
    assert bsz == 1 and d == D_MODEL
    xs = x.reshape(seq, d).astype(F32)
    for i in range(depth):
        j = i // 2
        router = (norm_ffn[i], moe_w_group[i], moe_b_group[i], moe_w_expert[i], moe_b_expert[i])
        if i % 2 == 0:
            xs, *routing = _s5_layer(xs, norm_mix[i], j, s5_w_in, s5_lam_re[j], s5_lam_im[j], s5_log_dt[j],
                                     s5_b_re[j], s5_b_im[j], s5_c_re[j], s5_c_im[j], s5_d[j], s5_w_out, router)
        else:
            xs, *routing = _nat_layer(xs, norm_mix[i], j, nat_w_qkv, nat_q_norm[j], nat_k_norm[j], nat_rpb[j],
                                      nat_w_o, router)
        xs = _moe_ple_layer(xs, routing, i, norm_ffn[i], moe_w_gate, moe_w_up, moe_w_down,
                            norm_ple[i], p, ple_w_proj, ple_w_gate)
    return xs.reshape(bsz, seq, d).astype(x.dtype)
```
